```python
import jax, jax.numpy as jnp
from jax import lax
import numpy as np

D_MODEL = 1024
BATCH = 8
SEQ = 4096
DEPTH = 4

GRID_W = 64
CTX_LEN = 256
HEAD_DIM = 64
A_HEADS = 8
A_KV_HEADS = 2
B_HEADS = 8
Q_BLOCK = 128
NA_ROWS_MAX = 8
NA_COLS = 16
ROPE_THETA = 10000.0
GLA_HEADS = 4
GLA_DK = D_MODEL // 2
GLA_DV = D_MODEL
GLA_GATE_RANK = 16
GLA_GATE_NORM = 16.0
GLA_CHUNK = 64
N_EXPERTS = 16
EC_CAPACITY = 2
EXPERT_FF = 1024
N_EVEN = (DEPTH + 1) // 2
N_ODD = DEPTH // 2
ATTN_SIZES = (A_HEADS * HEAD_DIM, A_KV_HEADS * HEAD_DIM, A_KV_HEADS * HEAD_DIM, B_HEADS * HEAD_DIM, B_HEADS * HEAD_DIM, B_HEADS * HEAD_DIM)
ATTN_HEADS = (A_HEADS, A_KV_HEADS, A_KV_HEADS, B_HEADS, B_HEADS, B_HEADS)
ATTN_IN = A_HEADS * HEAD_DIM + 2 * A_KV_HEADS * HEAD_DIM + 3 * B_HEADS * HEAD_DIM
ATTN_MIX = (A_HEADS + B_HEADS) * HEAD_DIM
GLA_SIZES = (GLA_DK, GLA_DK, GLA_DV, GLA_DV, GLA_GATE_RANK, GLA_GATE_RANK)
GLA_IN = 2 * GLA_DK + 2 * GLA_DV + 2 * GLA_GATE_RANK
EPS = 1e-6

kernel_name = 'hybrid_gqa_natten_gla_ecmoe_prefix'


def rms_norm(x, g):
    xf = x.astype(jnp.float32)
    y = xf * lax.rsqrt(jnp.mean(xf * xf, axis=-1, keepdims=True) + EPS)
    return (y * g.astype(jnp.float32)).astype(x.dtype)


def split_cols(p, sizes):
    out, start = [], 0
    for s in sizes:
        out.append(p[..., start:start + s])
        start += s
    return out


def to_heads(a, n):
    return a.reshape(a.shape[:-1] + (n, a.shape[-1] // n))


def axial_rope_tables(n):
    half = HEAD_DIM // 2
    t = jnp.arange(n, dtype=jnp.int32)
    inv = ROPE_THETA ** (-jnp.arange(0, half, 2, dtype=jnp.float32) / half)
    def tab(pos):
        ang = pos.astype(jnp.float32)[:, None] * inv[None, :]
        ang = jnp.concatenate([ang, ang], axis=-1)
        return jnp.cos(ang)[:, None, :], jnp.sin(ang)[:, None, :]
    cos_r, sin_r = tab(t // GRID_W)
    cos_c, sin_c = tab(t % GRID_W)
    return cos_r, sin_r, cos_c, sin_c


def apply_axial_rope(x, tables):
    cos_r, sin_r, cos_c, sin_c = tables
    half = x.shape[-1] // 2
    def rot(u, cos, sin):
        u1, u2 = jnp.split(u, 2, axis=-1)
        return u * cos + jnp.concatenate([-u2, u1], axis=-1) * sin
    xf = x.astype(jnp.float32)
    y = jnp.concatenate([rot(xf[..., :half], cos_r, sin_r), rot(xf[..., half:], cos_c, sin_c)], axis=-1)
    return y.astype(x.dtype)


def gqa_softmax(q, k, v):
    b, t, hq, dh = q.shape
    hk = k.shape[2]
    qg = q.reshape(b, t, hk, hq // hk, dh)
    s = jnp.einsum('btkgd,bskd->bkgts', qg, k).astype(jnp.float32) * (dh ** -0.5)
    p = jax.nn.softmax(s, axis=-1).astype(v.dtype)
    o = jnp.einsum('bkgts,bskd->btkgd', p, v)
    return o.reshape(b, t, hq, dh)


def neighbourhood_attention(q, k, v, k_ctx, v_ctx, rpb):
    b, n, h, dh = q.shape
    rows = n // GRID_W
    wr = min(NA_ROWS_MAX, rows)
    wc = NA_COLS
    qg = q.reshape(b, rows, GRID_W, h, dh)
    kg = k.reshape(b, rows, GRID_W, h, dh)
    vg = v.reshape(b, rows, GRID_W, h, dh)
    col = np.arange(GRID_W)
    col_idx = np.clip(col - wc // 2, 0, GRID_W - wc)[:, None] + np.arange(wc)[None, :]
    dc_idx = col_idx - col[:, None] + (NA_COLS - 1)
    scale = dh ** -0.5
    def row_block(args):
        r, q_r = args
        rs = jnp.clip(r - wr // 2, 0, rows - wr)
        k_win = lax.dynamic_slice_in_dim(kg, rs, wr, axis=1)[:, :, col_idx]
        v_win = lax.dynamic_slice_in_dim(vg, rs, wr, axis=1)[:, :, col_idx]
        dr_idx = rs + jnp.arange(wr) - r + (NA_ROWS_MAX - 1)
        bias = rpb[:, dr_idx[:, None, None], dc_idx[None, :, :]].transpose(0, 2, 1, 3)
        s_win = jnp.einsum('bwhd,biwjhd->bhwij', q_r, k_win).astype(jnp.float32) * scale + bias.astype(jnp.float32)
        s_ctx = jnp.einsum('bwhd,blhd->bhwl', q_r, k_ctx).astype(jnp.float32) * scale
        s = jnp.concatenate([s_win.reshape(b, h, GRID_W, wr * wc), s_ctx], axis=-1)
        p = jax.nn.softmax(s, axis=-1).astype(v.dtype)
        p_win = p[..., :wr * wc].reshape(b, h, GRID_W, wr, wc)
        p_ctx = p[..., wr * wc:]
        return jnp.einsum('bhwij,biwjhd->bwhd', p_win, v_win) + jnp.einsum('bhwl,blhd->bwhd', p_ctx, v_ctx)
    out = lax.map(row_block, (jnp.arange(rows, dtype=jnp.int32), qg.transpose(1, 0, 2, 3, 4)))
    return out.transpose(1, 0, 2, 3, 4).reshape(b, n, h, dh)


def attn_mixer(u, uc, w_in, w_out, qa_g, ka_g, qb_g, kb_g, rpb, rope, need_ctx_out):
    def project(hh):
        qa, ka, va, qb, kb, vb = [to_heads(p, nh) for p, nh in zip(split_cols(hh @ w_in, ATTN_SIZES), ATTN_HEADS)]
        return rms_norm(qa, qa_g), rms_norm(ka, ka_g), va, rms_norm(qb, qb_g), rms_norm(kb, kb_g), vb
    qa, ka, va, qb, kb, vb = project(u)
    cqa, cka, cva, cqb, ckb, cvb = project(uc)
    qa = apply_axial_rope(qa, rope)
    ka = apply_axial_rope(ka, rope)
    b, n = u.shape[0], u.shape[1]
    k_all = jnp.concatenate([ka, cka], axis=1)
    v_all = jnp.concatenate([va, cva], axis=1)
    q_blocks = qa.reshape(b, n // Q_BLOCK, Q_BLOCK, A_HEADS, HEAD_DIM).transpose(1, 0, 2, 3, 4)
    oa = lax.map(lambda qi: gqa_softmax(qi, k_all, v_all), q_blocks)
    oa = oa.transpose(1, 0, 2, 3, 4).reshape(b, n, A_HEADS * HEAD_DIM)
    ob = neighbourhood_attention(qb, kb, vb, ckb, cvb, rpb).reshape(b, n, B_HEADS * HEAD_DIM)
    y = jnp.concatenate([oa, ob], axis=-1) @ w_out
    yc = None
    if need_ctx_out:
        l = uc.shape[1]
        coa = gqa_softmax(cqa, cka, cva).reshape(b, l, A_HEADS * HEAD_DIM)
        cob = gqa_softmax(cqb, ckb, cvb).reshape(b, l, B_HEADS * HEAD_DIM)
        yc = jnp.concatenate([coa, cob], axis=-1) @ w_out
    return y, yc


def gla_chunked(q, k, v, g, s0):
    b, t, h, dk = q.shape
    dv = v.shape[-1]
    n = t // GLA_CHUNK
    def chunks(a):
        return a.astype(jnp.float32).reshape(b, n, GLA_CHUNK, h, a.shape[-1]).transpose(1, 0, 3, 2, 4)
    qc = chunks(q) * (dk ** -0.5)
    kc, vc, gc = chunks(k), chunks(v), chunks(g)
    cum = jnp.cumsum(gc, axis=3)
    cum_last = cum[:, :, :, -1:, :]
    q_dec = qc * jnp.exp(cum)
    k_inv = kc * jnp.exp(-cum)
    k_state = kc * jnp.exp(cum_last - cum)
    mask = jnp.tril(jnp.ones((GLA_CHUNK, GLA_CHUNK), dtype=bool))
    a_intra = jnp.where(mask, jnp.einsum('nbhid,nbhjd->nbhij', q_dec, k_inv), 0.0)
    o_intra = jnp.einsum('nbhij,nbhje->nbhie', a_intra, vc)
    def step(s, xs):
        q_d, k_s, v_c, c_last = xs
        o_inter = jnp.einsum('bhid,bhde->bhie', q_d, s)
        s = s * jnp.swapaxes(jnp.exp(c_last), -1, -2) + jnp.einsum('bhjd,bhje->bhde', k_s, v_c)
        return s, o_inter
    s_fin, o_inter = lax.scan(step, s0, (q_dec, k_state, vc, cum_last))
    o = (o_intra + o_inter).transpose(1, 0, 3, 2, 4).reshape(b, t, h, dv)
    return o.astype(v.dtype), s_fin


def gla_final_state(k, v, g):
    cum = jnp.cumsum(g.astype(jnp.float32), axis=1)
    kd = k.astype(jnp.float32) * jnp.exp(cum[:, -1:] - cum)
    return jnp.einsum('bthd,bthe->bhde', kd, v.astype(jnp.float32))


def gla_mixer(u, uc, w_in, gk_w_f, gk_b_f, gk_w_b, gk_b_b, onorm_g, w_out, need_ctx_out):
    def project(hh):
        q, k, v, og, zf, zb = split_cols(hh @ w_in, GLA_SIZES)
        gf = jax.nn.log_sigmoid((zf @ gk_w_f + gk_b_f).astype(jnp.float32)) / GLA_GATE_NORM
        gb = jax.nn.log_sigmoid((zb @ gk_w_b + gk_b_b).astype(jnp.float32)) / GLA_GATE_NORM
        return (to_heads(q, GLA_HEADS), to_heads(k, GLA_HEADS), to_heads(v, GLA_HEADS), og,
                to_heads(gf, GLA_HEADS), to_heads(gb, GLA_HEADS))
    def merge(o_f, o_b, og):
        o = rms_norm(o_f + o_b, onorm_g)
        o = o.reshape(o.shape[:2] + (GLA_DV,)) * jax.nn.silu(og)
        return o @ w_out
    flip = lambda a: a[:, ::-1]
    cq, ck, cv, cog, cgf, cgb = project(uc)
    q, k, v, og, gf, gb = project(u)
    b = u.shape[0]
    yc = None
    if need_ctx_out:
        s0 = jnp.zeros((b, GLA_HEADS, GLA_DK // GLA_HEADS, GLA_DV // GLA_HEADS), jnp.float32)
        co_f, s_f = gla_chunked(cq, ck, cv, cgf, s0)
        co_b, s_b = gla_chunked(flip(cq), flip(ck), flip(cv), flip(cgb), s0)
        yc = merge(co_f, flip(co_b), cog)
    else:
        s_f = gla_final_state(ck, cv, cgf)
        s_b = gla_final_state(flip(ck), flip(cv), flip(cgb))
    o_f, _ = gla_chunked(q, k, v, gf, s_f)
    o_b, _ = gla_chunked(flip(q), flip(k), flip(v), flip(gb), s_b)
    return merge(o_f, flip(o_b), og), yc


def expert_choice_ffn(hh, w_router, w_gate, w_up, w_down):
    b, n, d = hh.shape
    cap = EC_CAPACITY * n // N_EXPERTS
    aff = jax.nn.softmax((hh @ w_router).astype(jnp.float32), axis=-1)
    gate, idx = lax.top_k(aff.transpose(0, 2, 1), cap)
    xg = jax.vmap(lambda hb, ib: hb[ib])(hh, idx)
    a = jnp.einsum('becd,edf->becf', xg, w_gate)
    up = jnp.einsum('becd,edf->becf', xg, w_up)
    y = jnp.einsum('becf,efd->becd', jax.nn.silu(a) * up, w_down) * gate[..., None].astype(hh.dtype)
    return jax.vmap(lambda yb, ib: jnp.zeros((n, d), hh.dtype).at[ib.reshape(-1)].add(yb.reshape(-1, d)))(y, idx)


def setup_inputs(seed: int = 0):
    key = jax.random.key(seed)
    ks = jax.random.split(key, 26)
    f32 = jnp.float32
    def w(k, shape, fan_in, g=1.0):
        return jax.random.normal(k, shape, f32) * (g * fan_in ** -0.5)
    def gain(k, shape):
        return 1.0 + 0.02 * jax.random.normal(k, shape, f32)
    def small(k, shape, s):
        return s * jax.random.normal(k, shape, f32)
    return {
        'x': jax.random.normal(ks[0], (BATCH, SEQ, D_MODEL), f32),
        'c': jax.random.normal(ks[1], (BATCH, D_MODEL), f32),
        'ctx': jax.random.normal(ks[2], (BATCH, CTX_LEN, D_MODEL), f32),
        'c_ctx': jax.random.normal(ks[3], (D_MODEL,), f32),
        'ada_w': w(ks[4], (DEPTH, D_MODEL, 6 * D_MODEL), D_MODEL, 0.5),
        'ada_b': small(ks[5], (DEPTH, 6 * D_MODEL), 0.02),
        'norm1_g': gain(ks[6], (DEPTH, D_MODEL)),
        'norm2_g': gain(ks[7], (DEPTH, D_MODEL)),
        'attn_w_in': w(ks[8], (N_EVEN, D_MODEL, ATTN_IN), D_MODEL),
        'attn_w_out': w(ks[9], (N_EVEN, ATTN_MIX, D_MODEL), ATTN_MIX),
        'a_q_norm': gain(ks[10], (N_EVEN, HEAD_DIM)),
        'a_k_norm': gain(ks[11], (N_EVEN, HEAD_DIM)),
        'b_q_norm': gain(ks[12], (N_EVEN, HEAD_DIM)),
        'b_k_norm': gain(ks[13], (N_EVEN, HEAD_DIM)),
        'na_rpb': small(ks[14], (N_EVEN, B_HEADS, 2 * NA_ROWS_MAX - 1, 2 * NA_COLS - 1), 0.1),
        'gla_w_in': w(ks[15], (N_ODD, D_MODEL, GLA_IN), D_MODEL),
        'gla_gk_w_f': w(ks[16], (N_ODD, GLA_GATE_RANK, GLA_DK), GLA_GATE_RANK),
        'gla_gk_b_f': small(ks[17], (N_ODD, GLA_DK), 0.1),
        'gla_gk_w_b': w(ks[18], (N_ODD, GLA_GATE_RANK, GLA_DK), GLA_GATE_RANK),
        'gla_gk_b_b': small(ks[19], (N_ODD, GLA_DK), 0.1),
        'gla_o_norm': gain(ks[20], (N_ODD, GLA_DV // GLA_HEADS)),
        'gla_w_out': w(ks[21], (N_ODD, GLA_DV, D_MODEL), GLA_DV),
        'moe_router': w(ks[22], (DEPTH, D_MODEL, N_EXPERTS), D_MODEL),
        'moe_w_gate': w(ks[23], (DEPTH, N_EXPERTS, D_MODEL, EXPERT_FF), D_MODEL),
        'moe_w_up': w(ks[24], (DEPTH, N_EXPERTS, D_MODEL, EXPERT_FF), D_MODEL),
        'moe_w_down': w(ks[25], (DEPTH, N_EXPERTS, EXPERT_FF, D_MODEL), EXPERT_FF),
    }


def reference(x, c, ctx, c_ctx, ada_w, ada_b, norm1_g, norm2_g, attn_w_in, attn_w_out,
              a_q_norm, a_k_norm, b_q_norm, b_k_norm, na_rpb, gla_w_in, gla_gk_w_f, gla_gk_b_f,
              gla_gk_w_b, gla_gk_b_b, gla_o_norm, gla_w_out, moe_router, moe_w_gate, moe_w_up, moe_w_down):
    n = x.shape[1]
    rope = axial_rope_tables(n)
    s_lat = jax.nn.silu(c)
    s_ctx = jax.nn.silu(c_ctx)[None, :]
    h, hc = x, ctx
    for layer in range(DEPTH):
        last = layer == DEPTH - 1
        i = layer // 2
        m = jnp.split((s_lat @ ada_w[layer] + ada_b[layer])[:, None, :], 6, axis=-1)
        mc = jnp.split((s_ctx @ ada_w[layer] + ada_b[layer])[:, None, :], 6, axis=-1)
        u = rms_norm(h, norm1_g[layer]) * (1 + m[1]) + m[0]
        uc = rms_norm(hc, norm1_g[layer]) * (1 + mc[1]) + mc[0]
        if layer % 2 == 0:
            y, yc = attn_mixer(u, uc, attn_w_in[i], attn_w_out[i], a_q_norm[i], a_k_norm[i],
                               b_q_norm[i], b_k_norm[i], na_rpb[i], rope, not last)
        else:
            y, yc = gla_mixer(u, uc, gla_w_in[i], gla_gk_w_f[i], gla_gk_b_f[i], gla_gk_w_b[i],
                              gla_gk_b_b[i], gla_o_norm[i], gla_w_out[i], not last)
        h = h + m[2] * y
        u = rms_norm(h, norm2_g[layer]) * (1 + m[4]) + m[3]
        h = h + m[5] * expert_choice_ffn(u, moe_router[layer], moe_w_gate[layer], moe_w_up[layer], moe_w_down[layer])
        if not last:
            hc = hc + mc[2] * yc
            uc = rms_norm(hc, norm2_g[layer]) * (1 + mc[4]) + mc[3]
            hc = hc + mc[5] * expert_choice_ffn(uc, moe_router[layer], moe_w_gate[layer], moe_w_up[layer], moe_w_down[layer])
    return h
```

```python
import functools

import numpy as np
import jax
import jax.numpy as jnp
from jax import lax
from jax.experimental import pallas as pl
from jax.experimental.pallas import tpu as pltpu

F32 = jnp.float32
BF16 = jnp.bfloat16
I32 = jnp.int32

GRID_W = 64
HEAD_DIM = 64
A_HEADS = 8
A_KV_HEADS = 2
B_HEADS = 8
NA_ROWS = 8
NA_COLS = 16
ROPE_THETA = 10000.0
GLA_HEADS = 4
GLA_GATE_RANK = 16
GLA_GATE_NORM = 16.0
GLA_CHUNK = 64
N_EXPERTS = 16
EC_CAPACITY = 2
EPS = 1e-6

LANES = 128
BF16_ROWS = 16
NA_QROWS = 4
TOK_BLK = LANES
MASK_VALUE = -1e30
VMEM_LIMIT = 56 * 1024 * 1024

_NT = (((1,), (1,)), ((), ()))
_TN = (((0,), (0,)), ((), ()))


def _cparams(sem):
    return pltpu.CompilerParams(dimension_semantics=sem, vmem_limit_bytes=VMEM_LIMIT)


def _dot(a, b):
    return jnp.dot(a, b, preferred_element_type=F32)


def _dot_nt(a, b):
    return lax.dot_general(a, b, _NT, preferred_element_type=F32)


def _dot_tn(a, b):
    return lax.dot_general(a, b, _TN, preferred_element_type=F32)


def _split2(a):
    hi = a.astype(BF16)
    lo = (a - hi.astype(F32)).astype(BF16)
    return hi, lo


def _silu(a):
    return a / (1.0 + jnp.exp(-a))


def _rms_mod(x, g, shift, scale):
    ms = jnp.mean(x * x, axis=-1, keepdims=True)
    return (x * lax.rsqrt(ms + EPS) * g) * (1.0 + scale) + shift


def _mods_kernel(c_ref, w_ref, b_ref, o_ref):
    c = c_ref[...]
    s_hi, s_lo = _split2(_silu(c))
    w_hi, w_lo = _split2(w_ref[0])
    o_ref[0] = _dot(s_hi, w_hi) + _dot(s_lo, w_hi) + _dot(s_hi, w_lo) + b_ref[0]


def _mods(cvec, ada_w, ada_b):
    depth, d, d6 = ada_w.shape
    rows = cvec.shape[0]
    tn = 1536 if d6 % 1536 == 0 else d6
    return pl.pallas_call(
        _mods_kernel,
        grid=(depth, d6 // tn),
        in_specs=[pl.BlockSpec((rows, d), lambda l, j: (0, 0)),
                  pl.BlockSpec((1, d, tn), lambda l, j: (l, 0, j)),
                  pl.BlockSpec((1, 1, tn), lambda l, j: (l, 0, j))],
        out_specs=pl.BlockSpec((1, rows, tn), lambda l, j: (l, 0, j)),
        out_shape=jax.ShapeDtypeStruct((depth, rows, d6), F32),
        compiler_params=_cparams(("arbitrary", "arbitrary")),
        name="mods",
    )(cvec, ada_w, ada_b.reshape(depth, 1, d6))


_ATTN_COLS = (A_HEADS + 2 * A_KV_HEADS + 3 * B_HEADS) * HEAD_DIM
_QA = (0, 4)
_KA = (4, 5)
_VA = (5, 6)
_QB = (6, 10)
_KB = (10, 14)
_VB = (14, 18)


def _attn_proj_kernel(*refs, rope):
    if rope:
        x_ref, mod_ref, g1_ref, w_ref, gain_ref, bd_ref, cos_ref, sin_ref = refs[:8]
        outs = refs[8:]
    else:
        x_ref, mod_ref, g1_ref, w_ref, gain_ref, bd_ref = refs[:6]
        outs = refs[6:]
    qa_ref, ka_ref, va_ref, qb_ref, kb_ref, vb_ref = outs
    m = mod_ref[0]
    u = _rms_mod(x_ref[0], g1_ref[...], m[0:1], m[1:2]).astype(BF16)
    acc = _dot(u, w_ref[...])
    bd = bd_ref[...]
    if rope:
        cos = cos_ref[...]
        sin = sin_ref[...]
        lane = lax.broadcasted_iota(I32, cos.shape, 1)
        first = (lane & 31) < 16

    def chunk(j, norm, rot):
        cch = acc[:, j * LANES:(j + 1) * LANES]
        if norm:
            sq_hi, sq_lo = _split2(cch * cch)
            ss = _dot(sq_hi, bd) + _dot(sq_lo, bd)
            cch = cch * lax.rsqrt(ss * (1.0 / HEAD_DIM) + EPS) * gain_ref[:, j * LANES:(j + 1) * LANES]
        if rot:
            partner = jnp.where(first, pltpu.roll(cch, LANES - 16, 1), pltpu.roll(cch, 16, 1))
            cch = cch * cos + partner * sin
        return cch.astype(BF16)

    for (lo, hi), ref, norm, rot in ((_QA, qa_ref, True, rope), (_KA, ka_ref, True, rope), (_VA, va_ref, False, False),
                                     (_QB, qb_ref, True, False), (_KB, kb_ref, True, False), (_VB, vb_ref, False, False)):
        for j in range(lo, hi):
            ref[0, :, (j - lo) * LANES:(j - lo + 1) * LANES] = chunk(j, norm, rot)


def _attn_proj(h, mod, g1, w, gains, bd, cos=None, sin=None):
    b, t, d = h.shape
    tm = min(512, t)
    rope = cos is not None
    nb_mod = mod.shape[0]
    midx = (lambda i, j: (i, 0, 0)) if nb_mod > 1 else (lambda i, j: (0, 0, 0))
    in_specs = [pl.BlockSpec((1, tm, d), lambda i, j: (i, j, 0)),
                pl.BlockSpec((1, 6, d), midx),
                pl.BlockSpec((1, d), lambda i, j: (0, 0)),
                pl.BlockSpec((d, _ATTN_COLS), lambda i, j: (0, 0)),
                pl.BlockSpec((1, _ATTN_COLS), lambda i, j: (0, 0)),
                pl.BlockSpec((LANES, LANES), lambda i, j: (0, 0))]
    args = [h, mod, g1.reshape(1, d), w, gains, bd]
    if rope:
        in_specs += [pl.BlockSpec((tm, LANES), lambda i, j: (j, 0))] * 2
        args += [cos, sin]
    widths = [(hi - lo) * LANES for lo, hi in (_QA, _KA, _VA, _QB, _KB, _VB)]
    return pl.pallas_call(
        functools.partial(_attn_proj_kernel, rope=rope),
        grid=(b, t // tm),
        in_specs=in_specs,
        out_specs=[pl.BlockSpec((1, tm, wd), lambda i, j: (i, j, 0)) for wd in widths],
        out_shape=[jax.ShapeDtypeStruct((b, t, wd), BF16) for wd in widths],
        compiler_params=_cparams(("parallel", "parallel")),
        name="attn_proj_rope" if rope else "attn_proj",
    )(*args)


def _gqa_kernel(q_ref, k_ref, v_ref, o_ref, *, hq, hk):
    grp = hq // hk
    for h in range(hq):
        kv = h // grp
        q = q_ref[0, :, h * HEAD_DIM:(h + 1) * HEAD_DIM]
        k = k_ref[0, :, kv * HEAD_DIM:(kv + 1) * HEAD_DIM]
        v = v_ref[0, :, kv * HEAD_DIM:(kv + 1) * HEAD_DIM]
        s = _dot_nt(q, k)
        p = jnp.exp(s - jnp.max(s, axis=-1, keepdims=True))
        l = jnp.sum(p, axis=-1, keepdims=True)
        o = _dot(p.astype(BF16), v) / l
        o_ref[0, :, h * HEAD_DIM:(h + 1) * HEAD_DIM] = o.astype(BF16)


def _gqa(q, k, v, hq, hk):
    b, t, _ = q.shape
    s = k.shape[1]
    tq = min(256, t)
    return pl.pallas_call(
        functools.partial(_gqa_kernel, hq=hq, hk=hk),
        grid=(b, t // tq),
        in_specs=[pl.BlockSpec((1, tq, hq * HEAD_DIM), lambda i, j: (i, j, 0)),
                  pl.BlockSpec((1, s, hk * HEAD_DIM), lambda i, j: (i, 0, 0)),
                  pl.BlockSpec((1, s, hk * HEAD_DIM), lambda i, j: (i, 0, 0))],
        out_specs=pl.BlockSpec((1, tq, hq * HEAD_DIM), lambda i, j: (i, j, 0)),
        out_shape=jax.ShapeDtypeStruct((b, t, hq * HEAD_DIM), BF16),
        compiler_params=_cparams(("parallel", "parallel")),
        name="gqa",
    )(q, k, v)


def _na_geometry(rows):
    kr = min(NA_QROWS + NA_ROWS - 1, rows)
    nblk = rows // NA_QROWS
    return kr, nblk


def _na_bias_table(rpb, rows):
    kr, nblk = _na_geometry(rows)
    wr = min(NA_ROWS, rows)
    tabs = []
    for rb in (0, min(1, nblk - 1), nblk - 1):
        qr0 = rb * NA_QROWS
        kr0 = int(np.clip(qr0 - NA_ROWS // 2, 0, rows - kr))
        qr = qr0 + np.arange(NA_QROWS)[:, None, None, None]
        qc = np.arange(GRID_W)[None, :, None, None]
        krr = kr0 + np.arange(kr)[None, None, :, None]
        kc = np.arange(GRID_W)[None, None, None, :]
        rs = np.clip(qr - wr // 2, 0, rows - wr)
        cs = np.clip(qc - NA_COLS // 2, 0, GRID_W - NA_COLS)
        valid = (krr >= rs) & (krr < rs + wr) & (kc >= cs) & (kc < cs + NA_COLS)
        dr = np.clip(krr - qr + (NA_ROWS - 1), 0, 2 * NA_ROWS - 2)
        dc = np.clip(kc - qc + (NA_COLS - 1), 0, 2 * NA_COLS - 2)
        shape = (NA_QROWS, GRID_W, kr, GRID_W)
        valid = np.broadcast_to(valid, shape).reshape(NA_QROWS * GRID_W, kr * GRID_W)
        dr = np.broadcast_to(dr, shape).reshape(NA_QROWS * GRID_W, kr * GRID_W)
        dc = np.broadcast_to(dc, shape).reshape(NA_QROWS * GRID_W, kr * GRID_W)
        tabs.append(jnp.where(valid[None], rpb[:, dr, dc].astype(F32), MASK_VALUE))
    return jnp.stack(tabs)


def _na_kernel(q_ref, k_ref, v_ref, kc_ref, vc_ref, bias_ref, o_ref, *, rows, heads):
    kr, _ = _na_geometry(rows)
    rb = pl.program_id(1)
    kr0 = jnp.clip(rb * NA_QROWS - NA_ROWS // 2, 0, rows - kr)
    start = pl.multiple_of(kr0 * GRID_W, GRID_W)
    kw = k_ref[0, pl.ds(start, kr * GRID_W), :]
    vw = v_ref[0, pl.ds(start, kr * GRID_W), :]
    for h in range(heads):
        sl = slice(h * HEAD_DIM, (h + 1) * HEAD_DIM)
        q = q_ref[0, :, sl]
        s_w = _dot_nt(q, kw[:, sl]) + bias_ref[0, h]
        s_c = _dot_nt(q, kc_ref[0, :, sl])
        m = jnp.maximum(jnp.max(s_w, axis=-1, keepdims=True), jnp.max(s_c, axis=-1, keepdims=True))
        p_w = jnp.exp(s_w - m)
        p_c = jnp.exp(s_c - m)
        l = jnp.sum(p_w, axis=-1, keepdims=True) + jnp.sum(p_c, axis=-1, keepdims=True)
        o = (_dot(p_w.astype(BF16), vw[:, sl]) + _dot(p_c.astype(BF16), vc_ref[0, :, sl])) / l
        o_ref[0, :, sl] = o.astype(BF16)


def _na(q, k, v, kc, vc, bias):
    b, n, c = q.shape
    l = kc.shape[1]
    rows = n // GRID_W
    kr, nblk = _na_geometry(rows)
    tq = NA_QROWS * GRID_W
    heads = c // HEAD_DIM

    def pattern(i, j):
        return (jnp.where(j == 0, 0, jnp.where(j == nblk - 1, 2, 1)), 0, 0, 0)

    return pl.pallas_call(
        functools.partial(_na_kernel, rows=rows, heads=heads),
        grid=(b, nblk),
        in_specs=[pl.BlockSpec((1, tq, c), lambda i, j: (i, j, 0)),
                  pl.BlockSpec((1, n, c), lambda i, j: (i, 0, 0)),
                  pl.BlockSpec((1, n, c), lambda i, j: (i, 0, 0)),
                  pl.BlockSpec((1, l, c), lambda i, j: (i, 0, 0)),
                  pl.BlockSpec((1, l, c), lambda i, j: (i, 0, 0)),
                  pl.BlockSpec((1, heads, tq, kr * GRID_W), pattern)],
        out_specs=pl.BlockSpec((1, tq, c), lambda i, j: (i, j, 0)),
        out_shape=jax.ShapeDtypeStruct((b, n, c), BF16),
        compiler_params=_cparams(("parallel", "arbitrary")),
        name="na_attn",
    )(q, k, v, kc, vc, bias)


def _outproj_kernel(*refs, n_o, route):
    o_refs = refs[:n_o]
    h_ref, mod_ref, g2_ref, w_ref = refs[n_o:n_o + 4]
    rest = refs[n_o + 4:]
    if route:
        wr_hi_ref, wr_lo_ref, hn_ref, u2_ref, lg_ref = rest
    else:
        hn_ref, = rest
    m = mod_ref[0]
    o = o_refs[0][0] if n_o == 1 else jnp.concatenate([r[0] for r in o_refs], axis=1)
    hn = h_ref[0] + m[2:3] * _dot(o, w_ref[...])
    hn_ref[0] = hn
    if route:
        u2 = _rms_mod(hn, g2_ref[...], m[3:4], m[4:5])
        u_hi, u_lo = _split2(u2)
        u2_ref[0] = u_hi
        wr_hi = wr_hi_ref[...]
        lg_ref[0] = _dot_nt(wr_hi, u_hi) + _dot_nt(wr_hi, u_lo) + _dot_nt(wr_lo_ref[...], u_hi)


def _outproj(o_list, h, mod, g2, w, wr_hi=None, wr_lo=None):
    b, t, d = h.shape
    tm = min(512, t)
    route = wr_hi is not None
    nb_mod = mod.shape[0]
    midx = (lambda i, j: (i, 0, 0)) if nb_mod > 1 else (lambda i, j: (0, 0, 0))
    in_specs = [pl.BlockSpec((1, tm, o.shape[2]), lambda i, j: (i, j, 0)) for o in o_list]
    in_specs += [pl.BlockSpec((1, tm, d), lambda i, j: (i, j, 0)),
                 pl.BlockSpec((1, 6, d), midx),
                 pl.BlockSpec((1, d), lambda i, j: (0, 0)),
                 pl.BlockSpec((d, d), lambda i, j: (0, 0))]
    args = list(o_list) + [h, mod, g2.reshape(1, d), w]
    out_specs = [pl.BlockSpec((1, tm, d), lambda i, j: (i, j, 0))]
    out_shape = [jax.ShapeDtypeStruct((b, t, d), F32)]
    if route:
        in_specs += [pl.BlockSpec((N_EXPERTS, d), lambda i, j: (0, 0))] * 2
        args += [wr_hi, wr_lo]
        out_specs += [pl.BlockSpec((1, tm, d), lambda i, j: (i, j, 0)),
                      pl.BlockSpec((1, N_EXPERTS, tm), lambda i, j: (i, 0, j))]
        out_shape += [jax.ShapeDtypeStruct((b, t, d), BF16), jax.ShapeDtypeStruct((b, N_EXPERTS, t), F32)]
    return pl.pallas_call(
        functools.partial(_outproj_kernel, n_o=len(o_list), route=route),
        grid=(b, t // tm),
        in_specs=in_specs,
        out_specs=out_specs,
        out_shape=out_shape,
        compiler_params=_cparams(("parallel", "parallel")),
        name="outproj",
    )(*args)


def _cumsum_excl(x):
    e, n = x.shape
    nb = n // LANES
    st = jnp.concatenate([x[:, j * LANES:(j + 1) * LANES] for j in range(nb)], axis=0)
    ii = lax.broadcasted_iota(I32, (LANES, LANES), 0)
    jj = lax.broadcasted_iota(I32, (LANES, LANES), 1)
    incl = _dot(st.astype(BF16), jnp.where(ii <= jj, 1.0, 0.0).astype(BF16))
    tot = jnp.broadcast_to(incl[:, LANES - 1:LANES], incl.shape)
    r = lax.broadcasted_iota(I32, (nb * e, nb * e), 0)
    c = lax.broadcasted_iota(I32, (nb * e, nb * e), 1)
    sh = e.bit_length() - 1
    same = (r & (e - 1)) == (c & (e - 1))
    prev = jnp.where(same & (lax.shift_right_logical(c, sh) < lax.shift_right_logical(r, sh)), 1.0, 0.0).astype(BF16)
    starts = _dot(prev, tot.astype(BF16))
    excl_st = incl - st + starts
    excl = jnp.concatenate([excl_st[j * e:(j + 1) * e, :] for j in range(nb)], axis=1)
    return excl, starts


def _route_kernel(lg_ref, dest_ref, gate_ref, offs_ref, *, cap):
    x = lg_ref[0]
    ex = jnp.exp(x - jnp.max(x, axis=0, keepdims=True))
    aff = ex / jnp.sum(ex, axis=0, keepdims=True)
    bits = lax.bitcast_convert_type(aff, I32)

    def search(i, cur):
        cand = cur | lax.shift_left(jnp.int32(1), 30 - i)
        cnt = jnp.sum(jnp.where(bits >= cand, 1.0, 0.0), axis=1, keepdims=True)
        return jnp.where(cnt >= cap, cand, cur)

    thr = lax.fori_loop(0, 31, search, jnp.zeros((x.shape[0], 1), I32))
    gt = bits > thr
    eq = bits == thr
    need = cap - jnp.sum(jnp.where(gt, 1.0, 0.0), axis=1, keepdims=True)
    rank_eq, _ = _cumsum_excl(jnp.where(eq, 1.0, 0.0))
    sel = gt | (eq & (rank_eq < need))
    pos, starts = _cumsum_excl(jnp.where(sel, 1.0, 0.0))
    dest_ref[0] = jnp.where(sel, pos.astype(I32), -1)
    gate_ref[0] = jnp.where(sel, aff, 0.0)
    offs_ref[0] = starts.astype(I32)


def _route(lg, cap):
    b, e, n = lg.shape
    nb = n // LANES
    return pl.pallas_call(
        functools.partial(_route_kernel, cap=cap),
        grid=(b,),
        in_specs=[pl.BlockSpec((1, e, n), lambda i: (i, 0, 0))],
        out_specs=[pl.BlockSpec((1, e, n), lambda i: (i, 0, 0)),
                   pl.BlockSpec((1, e, n), lambda i: (i, 0, 0)),
                   pl.BlockSpec((1, nb * e, LANES), lambda i: (i, 0, 0))],
        out_shape=[jax.ShapeDtypeStruct((b, e, n), I32), jax.ShapeDtypeStruct((b, e, n), F32),
                   jax.ShapeDtypeStruct((b, nb * e, LANES), I32)],
        compiler_params=_cparams(("parallel",)),
        name="route",
    )(lg)


def _window(offs_ref, b, j, e, nblk, cap, win):
    off = offs_ref[(b * nblk + j) * N_EXPERTS + e]
    off_al = jnp.minimum(lax.shift_right_logical(off, 4) * BF16_ROWS, cap - win)
    return pl.multiple_of(off_al, BF16_ROWS)


def _onehot(dest_row, off_al, win):
    rows = lax.broadcasted_iota(I32, (win, TOK_BLK), 0) + off_al
    return jnp.where(rows == dest_row, 1.0, 0.0)


def _moe_ffn_kernel(offs_ref, u_ref, dest_ref, gate_ref, wg_ref, wu_ref, wd_ref, y_ref, xg_ref, gb_ref,
                    *, cap, nblk, win, nbatch, batch_ffn):
    e = pl.program_id(0)
    b = pl.program_id(1)
    row0 = pl.multiple_of(b * cap, BF16_ROWS) if batch_ffn else 0
    xg_ref[pl.ds(row0, cap), :] = jnp.zeros((cap, xg_ref.shape[1]), F32)
    gb_ref[pl.ds(row0, cap), :] = jnp.zeros((cap, LANES), F32)

    def gather(j, carry):
        off_al = _window(offs_ref, b, j, e, nblk, cap, win)
        g = _onehot(dest_ref[0, 0, pl.ds(j, 1), :], off_al, win)
        uj = u_ref[0, pl.ds(pl.multiple_of(j * TOK_BLK, TOK_BLK), TOK_BLK), :]
        r = pl.multiple_of(row0 + off_al, BF16_ROWS)
        xg_ref[pl.ds(r, win), :] += _dot(g.astype(BF16), uj)
        gsum = jnp.sum(g * gate_ref[0, 0, pl.ds(j, 1), :], axis=1, keepdims=True)
        gb_ref[pl.ds(r, win), :] += jnp.broadcast_to(gsum, (win, LANES))
        return carry

    lax.fori_loop(0, nblk, gather, 0)

    def ffn(r0, m):
        x = xg_ref[pl.ds(r0, m), :].astype(BF16)
        ff = wg_ref.shape[2]
        fc = min(512, ff)
        acc = jnp.zeros((m, wd_ref.shape[2]), F32)
        for c in range(ff // fc):
            a = _dot(x, wg_ref[0, :, c * fc:(c + 1) * fc])
            up = _dot(x, wu_ref[0, :, c * fc:(c + 1) * fc])
            acc = acc + _dot((_silu(a) * up).astype(BF16), wd_ref[0, c * fc:(c + 1) * fc, :])
        return (acc * gb_ref[pl.ds(r0, m), 0:1]).astype(BF16)

    if batch_ffn:
        @pl.when(b == nbatch - 1)
        def _():
            y = ffn(0, nbatch * cap)
            for bb in range(nbatch):
                y_ref[bb, 0] = y[bb * cap:(bb + 1) * cap]
    else:
        y_ref[0, 0] = ffn(0, cap)


def _moe_ffn(offs, u, dest4, gate4, wg, wu, wd, cap):
    b, n, d = u.shape
    nblk = n // TOK_BLK
    win = min(TOK_BLK + BF16_ROWS, cap)
    ff = wg.shape[2]
    batch_ffn = cap < 256
    rows = b * cap if batch_ffn else cap
    if batch_ffn:
        y_spec = pl.BlockSpec((b, 1, cap, d), lambda e, i, o: (0, e, 0, 0))
    else:
        y_spec = pl.BlockSpec((1, 1, cap, d), lambda e, i, o: (i, e, 0, 0))
    return pl.pallas_call(
        functools.partial(_moe_ffn_kernel, cap=cap, nblk=nblk, win=win, nbatch=b, batch_ffn=batch_ffn),
        grid_spec=pltpu.PrefetchScalarGridSpec(
            num_scalar_prefetch=1,
            grid=(N_EXPERTS, b),
            in_specs=[pl.BlockSpec((1, n, d), lambda e, i, o: (i, 0, 0)),
                      pl.BlockSpec((1, 1, nblk, TOK_BLK), lambda e, i, o: (i, e, 0, 0)),
                      pl.BlockSpec((1, 1, nblk, TOK_BLK), lambda e, i, o: (i, e, 0, 0)),
                      pl.BlockSpec((1, d, ff), lambda e, i, o: (e, 0, 0)),
                      pl.BlockSpec((1, d, ff), lambda e, i, o: (e, 0, 0)),
                      pl.BlockSpec((1, ff, d), lambda e, i, o: (e, 0, 0))],
            out_specs=y_spec,
            scratch_shapes=[pltpu.VMEM((rows, d), F32), pltpu.VMEM((rows, LANES), F32)]),
        out_shape=jax.ShapeDtypeStruct((b, N_EXPERTS, cap, d), BF16),
        compiler_params=_cparams(("arbitrary", "arbitrary")),
        name="moe_ffn",
    )(offs, u, dest4, gate4, wg, wu, wd)


def _moe_combine_kernel(offs_ref, y_ref, dest_ref, h_ref, mod_ref, out_ref, *, cap, nblk, win, nsub):
    b = pl.program_id(0)
    jb = pl.program_id(1)
    e = pl.program_id(2)

    @pl.when(e == 0)
    def _():
        out_ref[...] = jnp.zeros_like(out_ref)

    for s in range(nsub):
        off_al = _window(offs_ref, b, jb * nsub + s, e, nblk, cap, win)
        g = _onehot(dest_ref[0, 0, s:s + 1, :], off_al, win).astype(BF16)
        yw = y_ref[0, 0, pl.ds(off_al, win), :]
        out_ref[0, s * TOK_BLK:(s + 1) * TOK_BLK, :] += _dot_tn(g, yw)

    @pl.when(e == N_EXPERTS - 1)
    def _():
        out_ref[0] = h_ref[0] + mod_ref[0][5:6] * out_ref[0]


def _moe_combine(offs, y, dest4, h, mod, cap):
    b, n, d = h.shape
    nblk = n // TOK_BLK
    win = min(TOK_BLK + BF16_ROWS, cap)
    nsub = min(8, nblk)
    tb = nsub * TOK_BLK
    nb_mod = mod.shape[0]
    midx = (lambda i, j, e, o: (i, 0, 0)) if nb_mod > 1 else (lambda i, j, e, o: (0, 0, 0))
    return pl.pallas_call(
        functools.partial(_moe_combine_kernel, cap=cap, nblk=nblk, win=win, nsub=nsub),
        grid_spec=pltpu.PrefetchScalarGridSpec(
            num_scalar_prefetch=1,
            grid=(b, n // tb, N_EXPERTS),
            in_specs=[pl.BlockSpec((1, 1, cap, d), lambda i, j, e, o: (i, e, 0, 0)),
                      pl.BlockSpec((1, 1, nsub, TOK_BLK), lambda i, j, e, o: (i, e, j, 0)),
                      pl.BlockSpec((1, tb, d), lambda i, j, e, o: (i, j, 0)),
                      pl.BlockSpec((1, 6, d), midx)],
            out_specs=pl.BlockSpec((1, tb, d), lambda i, j, e, o: (i, j, 0))),
        out_shape=jax.ShapeDtypeStruct((b, n, d), F32),
        compiler_params=_cparams(("parallel", "parallel", "arbitrary")),
        name="moe_combine",
    )(offs, y, dest4, h, mod)


def _moe(h, u2, lg, mod, wg, wu, wd):
    b, n, d = h.shape
    cap = EC_CAPACITY * n // N_EXPERTS
    nblk = n // TOK_BLK
    dest, gate, starts = _route(lg, cap)
    offs = starts[:, :, 0].reshape(-1)
    dest4 = dest.reshape(b, N_EXPERTS, nblk, TOK_BLK)
    gate4 = gate.reshape(b, N_EXPERTS, nblk, TOK_BLK)
    y = _moe_ffn(offs, u2, dest4, gate4, wg, wu, wd, cap)
    return _moe_combine(offs, y, dest4, h, mod, cap)


def _gla_proj_kernel(x_ref, mod_ref, g1_ref, w_ref, wz_ref, wgate_ref, bgate_ref,
                     q_ref, k_ref, v_ref, og_ref, gf_ref, gb_ref, *, dk, dv):
    m = mod_ref[0]
    u = _rms_mod(x_ref[0], g1_ref[...], m[0:1], m[1:2]).astype(BF16)
    acc = _dot(u, w_ref[...])
    z = _dot(u, wz_ref[...]).astype(BF16)
    gp = _dot(z, wgate_ref[...]) + bgate_ref[...]
    g = (jnp.minimum(gp, 0.0) - jnp.log1p(jnp.exp(-jnp.abs(gp)))) * (1.0 / GLA_GATE_NORM)
    q_ref[0] = acc[:, :dk]
    k_ref[0] = acc[:, dk:2 * dk]
    v_ref[0] = acc[:, 2 * dk:2 * dk + dv].astype(BF16)
    og_ref[0] = acc[:, 2 * dk + dv:]
    gf_ref[0] = g[:, :dk]
    gb_ref[0] = g[:, dk:]


def _gla_proj(h, mod, g1, w, wz, wgate, bgate, dk, dv):
    b, t, d = h.shape
    tm = min(512, t)
    nb_mod = mod.shape[0]
    midx = (lambda i, j: (i, 0, 0)) if nb_mod > 1 else (lambda i, j: (0, 0, 0))
    cols = 2 * dk + 2 * dv
    widths = [(dk, F32), (dk, F32), (dv, BF16), (dv, F32), (dk, F32), (dk, F32)]
    return pl.pallas_call(
        functools.partial(_gla_proj_kernel, dk=dk, dv=dv),
        grid=(b, t // tm),
        in_specs=[pl.BlockSpec((1, tm, d), lambda i, j: (i, j, 0)),
                  pl.BlockSpec((1, 6, d), midx),
                  pl.BlockSpec((1, d), lambda i, j: (0, 0)),
                  pl.BlockSpec((d, cols), lambda i, j: (0, 0)),
                  pl.BlockSpec((d, LANES), lambda i, j: (0, 0)),
                  pl.BlockSpec((LANES, 2 * dk), lambda i, j: (0, 0)),
                  pl.BlockSpec((1, 2 * dk), lambda i, j: (0, 0))],
        out_specs=[pl.BlockSpec((1, tm, wd), lambda i, j: (i, j, 0)) for wd, _ in widths],
        out_shape=[jax.ShapeDtypeStruct((b, t, wd), dt) for wd, dt in widths],
        compiler_params=_cparams(("parallel", "parallel")),
        name="gla_proj",
    )(h, mod, g1.reshape(1, d), w, wz, wgate, bgate)


def _gla_scan(q_ref, k_ref, v_ref, g_ref, acc_ref, st_ref, *, n_chunks, reverse, scale):
    c = GLA_CHUNK
    ii = lax.broadcasted_iota(I32, (c, c), 0)
    jj = lax.broadcasted_iota(I32, (c, c), 1)
    tri = (jj >= ii) if reverse else (jj <= ii)
    tri_b = jnp.where(tri, 1.0, 0.0).astype(BF16)

    def body(i, carry):
        idx = (n_chunks - 1 - i) if reverse else i
        r0 = pl.multiple_of(idx * c, c)
        q = q_ref[0, pl.ds(r0, c), :] * scale
        k = k_ref[0, pl.ds(r0, c), :]
        v = v_ref[0, pl.ds(r0, c), :]
        g_hi, g_lo = _split2(g_ref[0, pl.ds(r0, c), :])
        cum = _dot(tri_b, g_hi) + _dot(tri_b, g_lo)
        cl = cum[0:1, :] if reverse else cum[c - 1:c, :]
        q_dec = (q * jnp.exp(cum)).astype(BF16)
        k_inv = (k * jnp.exp(-cum)).astype(BF16)
        k_st = (k * jnp.exp(cl - cum)).astype(BF16)
        a = jnp.where(tri, _dot_nt(q_dec, k_inv), 0.0)
        st = st_ref[...]
        o = _dot(a.astype(BF16), v) + _dot_nt(q_dec, st.astype(BF16))
        st_ref[...] = st * jnp.exp(cl) + _dot_tn(v, k_st)
        if reverse:
            acc_ref[pl.ds(r0, c), :] += o
        else:
            acc_ref[pl.ds(r0, c), :] = o
        return carry

    lax.fori_loop(0, n_chunks, body, 0)


def _gla_merge(acc_ref, og_ref, gain, o_ref, t):
    tm = min(512, t)
    for r in range(t // tm):
        o = acc_ref[r * tm:(r + 1) * tm, :]
        ms = jnp.mean(o * o, axis=-1, keepdims=True)
        y = (o * lax.rsqrt(ms + EPS) * gain) * _silu(og_ref[0, r * tm:(r + 1) * tm, :])
        o_ref[0, r * tm:(r + 1) * tm, :] = y.astype(BF16)


def _gla_kernel(*refs, n, l, dk, ctx_out):
    q_ref, k_ref, v_ref, gf_ref, gb_ref, og_ref, cq_ref, ck_ref, cv_ref, cgf_ref, cgb_ref, cog_ref, on_ref = refs[:13]
    if ctx_out:
        o_ref, co_ref, acc_ref, cacc_ref, st_ref = refs[13:]
    else:
        o_ref, acc_ref, cacc_ref, st_ref = refs[13:]
    scale = dk ** -0.5
    for reverse, g_ref, cg_ref in ((False, gf_ref, cgf_ref), (True, gb_ref, cgb_ref)):
        st_ref[...] = jnp.zeros_like(st_ref)
        _gla_scan(cq_ref, ck_ref, cv_ref, cg_ref, cacc_ref, st_ref, n_chunks=l // GLA_CHUNK, reverse=reverse, scale=scale)
        _gla_scan(q_ref, k_ref, v_ref, g_ref, acc_ref, st_ref, n_chunks=n // GLA_CHUNK, reverse=reverse, scale=scale)
    gain = on_ref[...]
    _gla_merge(acc_ref, og_ref, gain, o_ref, n)
    if ctx_out:
        _gla_merge(cacc_ref, cog_ref, gain, co_ref, l)


def _gla(lat, ctx, onorm, ctx_out):
    q, k, v, og, gf, gb = lat
    cq, ck, cv, cog, cgf, cgb = ctx
    b, n, dkt = q.shape
    l = cq.shape[1]
    dvt = v.shape[2]
    dk = dkt // GLA_HEADS
    dv = dvt // GLA_HEADS

    def spec(t, w):
        return pl.BlockSpec((1, t, w), lambda i, h: (i, 0, h))

    in_specs = [spec(n, dk), spec(n, dk), spec(n, dv), spec(n, dk), spec(n, dk), spec(n, dv),
                spec(l, dk), spec(l, dk), spec(l, dv), spec(l, dk), spec(l, dk), spec(l, dv),
                pl.BlockSpec((1, dv), lambda i, h: (0, 0))]
    out_specs = [spec(n, dv)]
    out_shape = [jax.ShapeDtypeStruct((b, n, dvt), BF16)]
    if ctx_out:
        out_specs.append(spec(l, dv))
        out_shape.append(jax.ShapeDtypeStruct((b, l, dvt), BF16))
    res = pl.pallas_call(
        functools.partial(_gla_kernel, n=n, l=l, dk=dk, ctx_out=ctx_out),
        grid=(b, GLA_HEADS),
        in_specs=in_specs,
        out_specs=out_specs,
        out_shape=out_shape,
        scratch_shapes=[pltpu.VMEM((n, dv), F32), pltpu.VMEM((l, dv), F32), pltpu.VMEM((dv, dk), F32)],
        compiler_params=_cparams(("parallel", "parallel")),
        name="gla",
    )(q, k, v, gf, gb, og, cq, ck, cv, cgf, cgb, cog, onorm.reshape(1, dv))
    return (res[0], res[1]) if ctx_out else (res[0], None)


def _rope_tables(n):
    half = HEAD_DIM // 2
    t = jnp.arange(n, dtype=I32)
    inv = ROPE_THETA ** (-jnp.arange(0, half, 2, dtype=F32) / half)

    def tab(pos):
        ang = pos.astype(F32)[:, None] * inv[None, :]
        ang = jnp.concatenate([ang, ang], axis=-1)
        return jnp.cos(ang), jnp.sin(ang)

    cos_r, sin_r = tab(t // GRID_W)
    cos_c, sin_c = tab(t % GRID_W)
    cos = jnp.concatenate([cos_r, cos_c], axis=-1)
    sin = jnp.concatenate([sin_r, sin_c], axis=-1)
    sign = jnp.where((jnp.arange(HEAD_DIM) % half) < half // 2, -1.0, 1.0).astype(F32)
    reps = LANES // HEAD_DIM
    return jnp.tile(cos, (1, reps)), jnp.tile(sin * sign[None, :], (1, reps))


def _block_ones():
    i = np.arange(LANES)
    return jnp.asarray((i[:, None] // HEAD_DIM) == (i[None, :] // HEAD_DIM), BF16)


def _attn_gains(qa_g, ka_g, qb_g, kb_g):
    qs = HEAD_DIM ** -0.5
    one_a = jnp.ones((A_KV_HEADS * HEAD_DIM,), F32)
    one_b = jnp.ones((B_HEADS * HEAD_DIM,), F32)
    return jnp.concatenate([jnp.tile(qa_g, A_HEADS) * qs, jnp.tile(ka_g, A_KV_HEADS), one_a,
                            jnp.tile(qb_g, B_HEADS) * qs, jnp.tile(kb_g, B_HEADS), one_b]).reshape(1, -1)


def kernel(x, c, ctx, c_ctx, ada_w, ada_b, norm1_g, norm2_g, attn_w_in, attn_w_out, a_q_norm, a_k_norm, b_q_norm,
           b_k_norm, na_rpb, gla_w_in, gla_gk_w_f, gla_gk_b_f, gla_gk_w_b, gla_gk_b_b, gla_o_norm, gla_w_out,
           moe_router, moe_w_gate, moe_w_up, moe_w_down):
    b, n, d = x.shape
    depth = ada_w.shape[0]
    dk = gla_gk_w_f.shape[2]
    dv = gla_w_out.shape[1]
    rows = -(-(b + 1) // 8) * 8
    cvec = jnp.zeros((rows, d), F32).at[:b].set(c).at[b].set(c_ctx)
    mods = _mods(cvec, ada_w, ada_b)
    cos, sin = _rope_tables(n)
    bd = _block_ones()
    h, hc = x, ctx
    for layer in range(depth):
        last = layer == depth - 1
        i = layer // 2
        ml = mods[layer, :b].reshape(b, 6, d)
        mc = mods[layer, b:b + 1].reshape(1, 6, d)
        if layer % 2 == 0:
            w_in = attn_w_in[i].astype(BF16)
            gains = _attn_gains(a_q_norm[i], a_k_norm[i], b_q_norm[i], b_k_norm[i])
            qa, ka, va, qb, kb, vb = _attn_proj(h, ml, norm1_g[layer], w_in, gains, bd, cos, sin)
            cqa, cka, cva, cqb, ckb, cvb = _attn_proj(hc, mc, norm1_g[layer], w_in, gains, bd)
            oa = _gqa(qa, jnp.concatenate([ka, cka], axis=1), jnp.concatenate([va, cva], axis=1), A_HEADS, A_KV_HEADS)
            ob = _na(qb, kb, vb, ckb, cvb, _na_bias_table(na_rpb[i], n // GRID_W))
            o_lat = [oa, ob]
            o_ctx = None if last else [_gqa(cqa, cka, cva, A_HEADS, A_KV_HEADS), _gqa(cqb, ckb, cvb, B_HEADS, B_HEADS)]
            w_out = attn_w_out[i].astype(BF16)
        else:
            w_main = gla_w_in[i][:, :2 * dk + 2 * dv].astype(BF16)
            wz = jnp.pad(gla_w_in[i][:, 2 * dk + 2 * dv:], ((0, 0), (0, LANES - 2 * GLA_GATE_RANK))).astype(BF16)
            wgate = jnp.zeros((LANES, 2 * dk), F32)
            wgate = wgate.at[:GLA_GATE_RANK, :dk].set(gla_gk_w_f[i]).at[GLA_GATE_RANK:2 * GLA_GATE_RANK, dk:].set(gla_gk_w_b[i])
            bgate = jnp.concatenate([gla_gk_b_f[i], gla_gk_b_b[i]]).reshape(1, 2 * dk)
            lat = _gla_proj(h, ml, norm1_g[layer], w_main, wz, wgate.astype(BF16), bgate, dk, dv)
            cx = _gla_proj(hc, mc, norm1_g[layer], w_main, wz, wgate.astype(BF16), bgate, dk, dv)
            o, co = _gla(lat, cx, gla_o_norm[i], not last)
            o_lat = [o]
            o_ctx = None if last else [co]
            w_out = gla_w_out[i].astype(BF16)
        wr_hi, wr_lo = _split2(moe_router[layer].T)
        wg = moe_w_gate[layer].astype(BF16)
        wu = moe_w_up[layer].astype(BF16)
        wd = moe_w_down[layer].astype(BF16)
        h1, u2, lg = _outproj(o_lat, h, ml, norm2_g[layer], w_out, wr_hi, wr_lo)
        h = _moe(h1, u2, lg, ml, wg, wu, wd)
        if not last:
            hc1, uc2, lgc = _outproj(o_ctx, hc, mc, norm2_g[layer], w_out, wr_hi, wr_lo)
            hc = _moe(hc1, uc2, lgc, mc, wg, wu, wd)
    return h
```

```python
import functools

import numpy as np
import jax
import jax.numpy as jnp
from jax import lax
from jax.experimental import pallas as pl
from jax.experimental.pallas import tpu as pltpu

F32 = jnp.float32
BF16 = jnp.bfloat16
I32 = jnp.int32

GRID_W = 64
HEAD_DIM = 64
A_HEADS = 8
A_KV_HEADS = 2
B_HEADS = 8
NA_ROWS = 8
NA_COLS = 16
ROPE_THETA = 10000.0
GLA_HEADS = 4
GLA_GATE_RANK = 16
GLA_GATE_NORM = 16.0
GLA_CHUNK = 64
N_EXPERTS = 16
EC_CAPACITY = 2
EPS = 1e-6

LANES = 128
BF16_ROWS = 16
NA_QROWS = 4
TOK_BLK = LANES
LOOP_UNROLL = 4
MASK_VALUE = -1e30
VMEM_LIMIT = 56 * 1024 * 1024

_NT = (((1,), (1,)), ((), ()))
_TN = (((0,), (0,)), ((), ()))


def _cparams(sem):
    return pltpu.CompilerParams(dimension_semantics=sem, vmem_limit_bytes=VMEM_LIMIT)


def _dot(a, b):
    return jnp.dot(a, b, preferred_element_type=F32)


def _dot_nt(a, b):
    return lax.dot_general(a, b, _NT, preferred_element_type=F32)


def _dot_tn(a, b):
    return lax.dot_general(a, b, _TN, preferred_element_type=F32)


def _split2(a):
    hi = a.astype(BF16)
    lo = (a - hi.astype(F32)).astype(BF16)
    return hi, lo


def _silu(a):
    return a / (1.0 + jnp.exp(-a))


def _rms_mod(x, g, shift, scale):
    ms = jnp.mean(x * x, axis=-1, keepdims=True)
    return (x * lax.rsqrt(ms + EPS) * g) * (1.0 + scale) + shift


def _mods_kernel(c_ref, w_ref, b_ref, o_ref):
    c = c_ref[...]
    s_hi, s_lo = _split2(_silu(c))
    w_hi, w_lo = _split2(w_ref[0])
    o_ref[0] = _dot(s_hi, w_hi) + _dot(s_lo, w_hi) + _dot(s_hi, w_lo) + b_ref[0]


def _mods(cvec, ada_w, ada_b):
    depth, d, d6 = ada_w.shape
    rows = cvec.shape[0]
    tn = 1536 if d6 % 1536 == 0 else d6
    return pl.pallas_call(
        _mods_kernel,
        grid=(depth, d6 // tn),
        in_specs=[pl.BlockSpec((rows, d), lambda l, j: (0, 0)),
                  pl.BlockSpec((1, d, tn), lambda l, j: (l, 0, j)),
                  pl.BlockSpec((1, 1, tn), lambda l, j: (l, 0, j))],
        out_specs=pl.BlockSpec((1, rows, tn), lambda l, j: (l, 0, j)),
        out_shape=jax.ShapeDtypeStruct((depth, rows, d6), F32),
        compiler_params=_cparams(("arbitrary", "arbitrary")),
        name="mods",
    )(cvec, ada_w, ada_b.reshape(depth, 1, d6))


_ATTN_COLS = (A_HEADS + 2 * A_KV_HEADS + 3 * B_HEADS) * HEAD_DIM
_QA = (0, 4)
_KA = (4, 5)
_VA = (5, 6)
_QB = (6, 10)
_KB = (10, 14)
_VB = (14, 18)


def _attn_proj_kernel(*refs, rope):
    if rope:
        x_ref, mod_ref, g1_ref, w_ref, gain_ref, bd_ref, cos_ref, sin_ref = refs[:8]
        outs = refs[8:]
    else:
        x_ref, mod_ref, g1_ref, w_ref, gain_ref, bd_ref = refs[:6]
        outs = refs[6:]
    qa_ref, ka_ref, va_ref, qb_ref, kb_ref, vb_ref = outs
    m = mod_ref[0]
    u = _rms_mod(x_ref[0], g1_ref[...], m[0:1], m[1:2]).astype(BF16)
    acc = _dot(u, w_ref[...])
    bd = bd_ref[...]
    if rope:
        cos = cos_ref[...]
        sin = sin_ref[...]
        lane = lax.broadcasted_iota(I32, cos.shape, 1)
        first = (lane & 31) < 16

    def chunk(j, norm, rot):
        cch = acc[:, j * LANES:(j + 1) * LANES]
        if norm:
            sq_hi, sq_lo = _split2(cch * cch)
            ss = _dot(sq_hi, bd) + _dot(sq_lo, bd)
            cch = cch * lax.rsqrt(ss * (1.0 / HEAD_DIM) + EPS) * gain_ref[:, j * LANES:(j + 1) * LANES]
        if rot:
            partner = jnp.where(first, pltpu.roll(cch, LANES - 16, 1), pltpu.roll(cch, 16, 1))
            cch = cch * cos + partner * sin
        return cch.astype(BF16)

    for (lo, hi), ref, norm, rot in ((_QA, qa_ref, True, rope), (_KA, ka_ref, True, rope), (_VA, va_ref, False, False),
                                     (_QB, qb_ref, True, False), (_KB, kb_ref, True, False), (_VB, vb_ref, False, False)):
        for j in range(lo, hi):
            ref[0, :, (j - lo) * LANES:(j - lo + 1) * LANES] = chunk(j, norm, rot)


def _attn_proj(h, mod, g1, w, gains, bd, cos=None, sin=None):
    b, t, d = h.shape
    tm = min(512, t)
    rope = cos is not None
    nb_mod = mod.shape[0]
    midx = (lambda i, j: (i, 0, 0)) if nb_mod > 1 else (lambda i, j: (0, 0, 0))
    in_specs = [pl.BlockSpec((1, tm, d), lambda i, j: (i, j, 0)),
                pl.BlockSpec((1, 6, d), midx),
                pl.BlockSpec((1, d), lambda i, j: (0, 0)),
                pl.BlockSpec((d, _ATTN_COLS), lambda i, j: (0, 0)),
                pl.BlockSpec((1, _ATTN_COLS), lambda i, j: (0, 0)),
                pl.BlockSpec((LANES, LANES), lambda i, j: (0, 0))]
    args = [h, mod, g1.reshape(1, d), w, gains, bd]
    if rope:
        in_specs += [pl.BlockSpec((tm, LANES), lambda i, j: (j, 0))] * 2
        args += [cos, sin]
    widths = [(hi - lo) * LANES for lo, hi in (_QA, _KA, _VA, _QB, _KB, _VB)]
    return pl.pallas_call(
        functools.partial(_attn_proj_kernel, rope=rope),
        grid=(b, t // tm),
        in_specs=in_specs,
        out_specs=[pl.BlockSpec((1, tm, wd), lambda i, j: (i, j, 0)) for wd in widths],
        out_shape=[jax.ShapeDtypeStruct((b, t, wd), BF16) for wd in widths],
        compiler_params=_cparams(("parallel", "parallel")),
        name="attn_proj_rope" if rope else "attn_proj",
    )(*args)


def _gqa_kernel(q_ref, k_ref, v_ref, o_ref, *, hq, hk):
    grp = hq // hk
    for h in range(hq):
        kv = h // grp
        q = q_ref[0, :, h * HEAD_DIM:(h + 1) * HEAD_DIM]
        k = k_ref[0, :, kv * HEAD_DIM:(kv + 1) * HEAD_DIM]
        v = v_ref[0, :, kv * HEAD_DIM:(kv + 1) * HEAD_DIM]
        s = _dot_nt(q, k)
        p = jnp.exp(s - jnp.max(s, axis=-1, keepdims=True))
        l = jnp.sum(p, axis=-1, keepdims=True)
        o = _dot(p.astype(BF16), v) / l
        o_ref[0, :, h * HEAD_DIM:(h + 1) * HEAD_DIM] = o.astype(BF16)


def _gqa(q, k, v, hq, hk):
    b, t, _ = q.shape
    s = k.shape[1]
    tq = min(256, t)
    return pl.pallas_call(
        functools.partial(_gqa_kernel, hq=hq, hk=hk),
        grid=(b, t // tq),
        in_specs=[pl.BlockSpec((1, tq, hq * HEAD_DIM), lambda i, j: (i, j, 0)),
                  pl.BlockSpec((1, s, hk * HEAD_DIM), lambda i, j: (i, 0, 0)),
                  pl.BlockSpec((1, s, hk * HEAD_DIM), lambda i, j: (i, 0, 0))],
        out_specs=pl.BlockSpec((1, tq, hq * HEAD_DIM), lambda i, j: (i, j, 0)),
        out_shape=jax.ShapeDtypeStruct((b, t, hq * HEAD_DIM), BF16),
        compiler_params=_cparams(("parallel", "parallel")),
        name="gqa",
    )(q, k, v)


def _na_geometry(rows):
    kr = min(NA_QROWS + NA_ROWS - 1, rows)
    nblk = rows // NA_QROWS
    return kr, nblk


def _na_bias_table(rpb, rows):
    kr, nblk = _na_geometry(rows)
    wr = min(NA_ROWS, rows)
    heads = rpb.shape[0]
    qc = np.arange(GRID_W)[:, None]
    kc = np.arange(GRID_W)[None, :]
    cs = np.clip(qc - NA_COLS // 2, 0, GRID_W - NA_COLS)
    col_ok = (kc >= cs) & (kc < cs + NA_COLS)
    pick = (np.arange(2 * NA_COLS - 1)[:, None, None] == (kc - qc + NA_COLS - 1)[None]) & col_ok[None]
    toeplitz = jnp.einsum("hrd,dqk->hrqk", rpb.astype(F32), jnp.asarray(pick, F32), precision=lax.Precision.HIGHEST)
    toeplitz = jnp.where(col_ok[None, None], toeplitz, MASK_VALUE)
    tabs = []
    for rb in (0, min(1, nblk - 1), nblk - 1):
        qr0 = rb * NA_QROWS
        kr0 = int(np.clip(qr0 - NA_ROWS // 2, 0, rows - kr))
        qr = qr0 + np.arange(NA_QROWS)[:, None]
        krr = kr0 + np.arange(kr)[None, :]
        rs = np.clip(qr - wr // 2, 0, rows - wr)
        row_ok = (krr >= rs) & (krr < rs + wr)
        dr = krr - qr + (NA_ROWS - 1)
        masked = jnp.full((heads, GRID_W, GRID_W), MASK_VALUE, F32)
        blocks = [jnp.concatenate([toeplitz[:, dr[i, j]] if row_ok[i, j] else masked for j in range(kr)], axis=2)
                  for i in range(NA_QROWS)]
        tabs.append(jnp.concatenate(blocks, axis=1))
    return jnp.stack(tabs)


def _na_kernel(q_ref, k_ref, v_ref, kc_ref, vc_ref, bias_ref, o_ref, *, rows, heads):
    kr, _ = _na_geometry(rows)
    rb = pl.program_id(1)
    kr0 = jnp.clip(rb * NA_QROWS - NA_ROWS // 2, 0, rows - kr)
    start = pl.multiple_of(kr0 * GRID_W, GRID_W)
    kw = k_ref[0, pl.ds(start, kr * GRID_W), :]
    vw = v_ref[0, pl.ds(start, kr * GRID_W), :]
    for h in range(heads):
        sl = slice(h * HEAD_DIM, (h + 1) * HEAD_DIM)
        q = q_ref[0, :, sl]
        s_w = _dot_nt(q, kw[:, sl]) + bias_ref[0, h]
        s_c = _dot_nt(q, kc_ref[0, :, sl])
        m = jnp.maximum(jnp.max(s_w, axis=-1, keepdims=True), jnp.max(s_c, axis=-1, keepdims=True))
        p_w = jnp.exp(s_w - m)
        p_c = jnp.exp(s_c - m)
        l = jnp.sum(p_w, axis=-1, keepdims=True) + jnp.sum(p_c, axis=-1, keepdims=True)
        o = (_dot(p_w.astype(BF16), vw[:, sl]) + _dot(p_c.astype(BF16), vc_ref[0, :, sl])) / l
        o_ref[0, :, sl] = o.astype(BF16)


def _na(q, k, v, kc, vc, bias):
    b, n, c = q.shape
    l = kc.shape[1]
    rows = n // GRID_W
    kr, nblk = _na_geometry(rows)
    tq = NA_QROWS * GRID_W
    heads = c // HEAD_DIM

    def pattern(i, j):
        return (jnp.where(j == 0, 0, jnp.where(j == nblk - 1, 2, 1)), 0, 0, 0)

    return pl.pallas_call(
        functools.partial(_na_kernel, rows=rows, heads=heads),
        grid=(b, nblk),
        in_specs=[pl.BlockSpec((1, tq, c), lambda i, j: (i, j, 0)),
                  pl.BlockSpec((1, n, c), lambda i, j: (i, 0, 0)),
                  pl.BlockSpec((1, n, c), lambda i, j: (i, 0, 0)),
                  pl.BlockSpec((1, l, c), lambda i, j: (i, 0, 0)),
                  pl.BlockSpec((1, l, c), lambda i, j: (i, 0, 0)),
                  pl.BlockSpec((1, heads, tq, kr * GRID_W), pattern)],
        out_specs=pl.BlockSpec((1, tq, c), lambda i, j: (i, j, 0)),
        out_shape=jax.ShapeDtypeStruct((b, n, c), BF16),
        compiler_params=_cparams(("parallel", "arbitrary")),
        name="na_attn",
    )(q, k, v, kc, vc, bias)


def _outproj_kernel(*refs, n_o, route):
    o_refs = refs[:n_o]
    h_ref, mod_ref, g2_ref, w_ref = refs[n_o:n_o + 4]
    rest = refs[n_o + 4:]
    if route:
        wr_hi_ref, wr_lo_ref, hn_ref, u2_ref, lg_ref = rest
    else:
        hn_ref, = rest
    m = mod_ref[0]
    o = o_refs[0][0] if n_o == 1 else jnp.concatenate([r[0] for r in o_refs], axis=1)
    hn = h_ref[0] + m[2:3] * _dot(o, w_ref[...])
    hn_ref[0] = hn
    if route:
        u2 = _rms_mod(hn, g2_ref[...], m[3:4], m[4:5])
        u_hi, u_lo = _split2(u2)
        u2_ref[0] = u_hi
        wr_hi = wr_hi_ref[...]
        lg_ref[0] = _dot_nt(wr_hi, u_hi) + _dot_nt(wr_hi, u_lo) + _dot_nt(wr_lo_ref[...], u_hi)


def _outproj(o_list, h, mod, g2, w, wr_hi=None, wr_lo=None):
    b, t, d = h.shape
    tm = min(512, t)
    route = wr_hi is not None
    nb_mod = mod.shape[0]
    midx = (lambda i, j: (i, 0, 0)) if nb_mod > 1 else (lambda i, j: (0, 0, 0))
    in_specs = [pl.BlockSpec((1, tm, o.shape[2]), lambda i, j: (i, j, 0)) for o in o_list]
    in_specs += [pl.BlockSpec((1, tm, d), lambda i, j: (i, j, 0)),
                 pl.BlockSpec((1, 6, d), midx),
                 pl.BlockSpec((1, d), lambda i, j: (0, 0)),
                 pl.BlockSpec((d, d), lambda i, j: (0, 0))]
    args = list(o_list) + [h, mod, g2.reshape(1, d), w]
    out_specs = [pl.BlockSpec((1, tm, d), lambda i, j: (i, j, 0))]
    out_shape = [jax.ShapeDtypeStruct((b, t, d), F32)]
    if route:
        in_specs += [pl.BlockSpec((N_EXPERTS, d), lambda i, j: (0, 0))] * 2
        args += [wr_hi, wr_lo]
        out_specs += [pl.BlockSpec((1, tm, d), lambda i, j: (i, j, 0)),
                      pl.BlockSpec((1, N_EXPERTS, tm), lambda i, j: (i, 0, j))]
        out_shape += [jax.ShapeDtypeStruct((b, t, d), BF16), jax.ShapeDtypeStruct((b, N_EXPERTS, t), F32)]
    return pl.pallas_call(
        functools.partial(_outproj_kernel, n_o=len(o_list), route=route),
        grid=(b, t // tm),
        in_specs=in_specs,
        out_specs=out_specs,
        out_shape=out_shape,
        compiler_params=_cparams(("parallel", "parallel")),
        name="outproj",
    )(*args)


def _cumsum_excl(x):
    e, n = x.shape
    nb = n // LANES
    st = jnp.concatenate([x[:, j * LANES:(j + 1) * LANES] for j in range(nb)], axis=0)
    ii = lax.broadcasted_iota(I32, (LANES, LANES), 0)
    jj = lax.broadcasted_iota(I32, (LANES, LANES), 1)
    incl = _dot(st.astype(BF16), jnp.where(ii <= jj, 1.0, 0.0).astype(BF16))
    tot = jnp.broadcast_to(incl[:, LANES - 1:LANES], incl.shape)
    r = lax.broadcasted_iota(I32, (nb * e, nb * e), 0)
    c = lax.broadcasted_iota(I32, (nb * e, nb * e), 1)
    sh = e.bit_length() - 1
    same = (r & (e - 1)) == (c & (e - 1))
    prev = jnp.where(same & (lax.shift_right_logical(c, sh) < lax.shift_right_logical(r, sh)), 1.0, 0.0).astype(BF16)
    starts = _dot(prev, tot.astype(BF16))
    excl_st = incl - st + starts
    excl = jnp.concatenate([excl_st[j * e:(j + 1) * e, :] for j in range(nb)], axis=1)
    return excl, starts


def _route_kernel(lg_ref, dest_ref, gate_ref, offs_ref, *, cap):
    x = lg_ref[0]
    ex = jnp.exp(x - jnp.max(x, axis=0, keepdims=True))
    aff = ex / jnp.sum(ex, axis=0, keepdims=True)
    bits = lax.bitcast_convert_type(aff, I32)

    def search(i, cur):
        cand = cur | lax.shift_left(jnp.int32(1), 30 - i)
        cnt = jnp.sum(jnp.where(bits >= cand, 1.0, 0.0), axis=1, keepdims=True)
        return jnp.where(cnt >= cap, cand, cur)

    thr = lax.fori_loop(0, 31, search, jnp.zeros((x.shape[0], 1), I32))
    gt = bits > thr
    eq = bits == thr
    need = cap - jnp.sum(jnp.where(gt, 1.0, 0.0), axis=1, keepdims=True)
    rank_eq, _ = _cumsum_excl(jnp.where(eq, 1.0, 0.0))
    sel = gt | (eq & (rank_eq < need))
    pos, starts = _cumsum_excl(jnp.where(sel, 1.0, 0.0))
    dest_ref[0] = jnp.where(sel, pos.astype(I32), -1)
    gate_ref[0] = jnp.where(sel, aff, 0.0)
    offs_ref[0] = starts.astype(I32)


def _route(lg, cap):
    b, e, n = lg.shape
    nb = n // LANES
    return pl.pallas_call(
        functools.partial(_route_kernel, cap=cap),
        grid=(b,),
        in_specs=[pl.BlockSpec((1, e, n), lambda i: (i, 0, 0))],
        out_specs=[pl.BlockSpec((1, e, n), lambda i: (i, 0, 0)),
                   pl.BlockSpec((1, e, n), lambda i: (i, 0, 0)),
                   pl.BlockSpec((1, nb * e, LANES), lambda i: (i, 0, 0))],
        out_shape=[jax.ShapeDtypeStruct((b, e, n), I32), jax.ShapeDtypeStruct((b, e, n), F32),
                   jax.ShapeDtypeStruct((b, nb * e, LANES), I32)],
        compiler_params=_cparams(("parallel",)),
        name="route",
    )(lg)


def _window(offs_ref, b, j, e, nblk, cap, win):
    off = offs_ref[(b * nblk + j) * N_EXPERTS + e]
    off_al = jnp.minimum(lax.shift_right_logical(off, 4) * BF16_ROWS, cap - win)
    return pl.multiple_of(off_al, BF16_ROWS)


def _onehot(dest_row, off_al, win):
    rows = lax.broadcasted_iota(I32, (win, TOK_BLK), 0) + off_al
    return jnp.where(rows == dest_row, 1.0, 0.0)


def _moe_ffn_kernel(offs_ref, u_ref, dest_ref, gate_ref, wg_ref, wu_ref, wd_ref, y_ref, xg_ref, gb_ref,
                    *, cap, nblk, win, nbatch, batch_ffn):
    e = pl.program_id(0)
    b = pl.program_id(1)
    row0 = pl.multiple_of(b * cap, BF16_ROWS) if batch_ffn else 0
    xg_ref[pl.ds(row0, cap), :] = jnp.zeros((cap, xg_ref.shape[1]), F32)
    gb_ref[pl.ds(row0, cap), :] = jnp.zeros((cap, LANES), F32)

    def gather(j, carry):
        off_al = _window(offs_ref, b, j, e, nblk, cap, win)
        g = _onehot(dest_ref[0, 0, pl.ds(j, 1), :], off_al, win)
        uj = u_ref[0, pl.ds(pl.multiple_of(j * TOK_BLK, TOK_BLK), TOK_BLK), :]
        r = pl.multiple_of(row0 + off_al, BF16_ROWS)
        xg_ref[pl.ds(r, win), :] += _dot(g.astype(BF16), uj)
        gsum = jnp.sum(g * gate_ref[0, 0, pl.ds(j, 1), :], axis=1, keepdims=True)
        gb_ref[pl.ds(r, win), :] += jnp.broadcast_to(gsum, (win, LANES))
        return carry

    lax.fori_loop(0, nblk, gather, 0, unroll=min(LOOP_UNROLL, nblk))

    def ffn(r0, m):
        x = xg_ref[pl.ds(r0, m), :].astype(BF16)
        ff = wg_ref.shape[2]
        fc = min(512, ff)
        acc = jnp.zeros((m, wd_ref.shape[2]), F32)
        for c in range(ff // fc):
            a = _dot(x, wg_ref[0, :, c * fc:(c + 1) * fc])
            up = _dot(x, wu_ref[0, :, c * fc:(c + 1) * fc])
            acc = acc + _dot((_silu(a) * up).astype(BF16), wd_ref[0, c * fc:(c + 1) * fc, :])
        return (acc * gb_ref[pl.ds(r0, m), 0:1]).astype(BF16)

    if batch_ffn:
        @pl.when(b == nbatch - 1)
        def _():
            y = ffn(0, nbatch * cap)
            for bb in range(nbatch):
                y_ref[bb, 0] = y[bb * cap:(bb + 1) * cap]
    else:
        y_ref[0, 0] = ffn(0, cap)


def _moe_ffn(offs, u, dest4, gate4, wg, wu, wd, cap):
    b, n, d = u.shape
    nblk = n // TOK_BLK
    win = min(TOK_BLK + BF16_ROWS, cap)
    ff = wg.shape[2]
    batch_ffn = cap < 256
    rows = b * cap if batch_ffn else cap
    if batch_ffn:
        y_spec = pl.BlockSpec((b, 1, cap, d), lambda e, i, o: (0, e, 0, 0))
    else:
        y_spec = pl.BlockSpec((1, 1, cap, d), lambda e, i, o: (i, e, 0, 0))
    return pl.pallas_call(
        functools.partial(_moe_ffn_kernel, cap=cap, nblk=nblk, win=win, nbatch=b, batch_ffn=batch_ffn),
        grid_spec=pltpu.PrefetchScalarGridSpec(
            num_scalar_prefetch=1,
            grid=(N_EXPERTS, b),
            in_specs=[pl.BlockSpec((1, n, d), lambda e, i, o: (i, 0, 0)),
                      pl.BlockSpec((1, 1, nblk, TOK_BLK), lambda e, i, o: (i, e, 0, 0)),
                      pl.BlockSpec((1, 1, nblk, TOK_BLK), lambda e, i, o: (i, e, 0, 0)),
                      pl.BlockSpec((1, d, ff), lambda e, i, o: (e, 0, 0)),
                      pl.BlockSpec((1, d, ff), lambda e, i, o: (e, 0, 0)),
                      pl.BlockSpec((1, ff, d), lambda e, i, o: (e, 0, 0))],
            out_specs=y_spec,
            scratch_shapes=[pltpu.VMEM((rows, d), F32), pltpu.VMEM((rows, LANES), F32)]),
        out_shape=jax.ShapeDtypeStruct((b, N_EXPERTS, cap, d), BF16),
        compiler_params=_cparams(("arbitrary", "arbitrary")),
        name="moe_ffn",
    )(offs, u, dest4, gate4, wg, wu, wd)


def _moe_combine_kernel(offs_ref, y_ref, dest_ref, h_ref, mod_ref, out_ref, *, cap, nblk, win, nsub):
    b = pl.program_id(0)
    jb = pl.program_id(1)
    e = pl.program_id(2)

    @pl.when(e == 0)
    def _():
        out_ref[...] = jnp.zeros_like(out_ref)

    for s in range(nsub):
        off_al = _window(offs_ref, b, jb * nsub + s, e, nblk, cap, win)
        g = _onehot(dest_ref[0, 0, s:s + 1, :], off_al, win).astype(BF16)
        yw = y_ref[0, 0, pl.ds(off_al, win), :]
        out_ref[0, s * TOK_BLK:(s + 1) * TOK_BLK, :] += _dot_tn(g, yw)

    @pl.when(e == N_EXPERTS - 1)
    def _():
        out_ref[0] = h_ref[0] + mod_ref[0][5:6] * out_ref[0]


def _moe_combine(offs, y, dest4, h, mod, cap):
    b, n, d = h.shape
    nblk = n // TOK_BLK
    win = min(TOK_BLK + BF16_ROWS, cap)
    nsub = min(8, nblk)
    tb = nsub * TOK_BLK
    nb_mod = mod.shape[0]
    midx = (lambda i, j, e, o: (i, 0, 0)) if nb_mod > 1 else (lambda i, j, e, o: (0, 0, 0))
    return pl.pallas_call(
        functools.partial(_moe_combine_kernel, cap=cap, nblk=nblk, win=win, nsub=nsub),
        grid_spec=pltpu.PrefetchScalarGridSpec(
            num_scalar_prefetch=1,
            grid=(b, n // tb, N_EXPERTS),
            in_specs=[pl.BlockSpec((1, 1, cap, d), lambda i, j, e, o: (i, e, 0, 0)),
                      pl.BlockSpec((1, 1, nsub, TOK_BLK), lambda i, j, e, o: (i, e, j, 0)),
                      pl.BlockSpec((1, tb, d), lambda i, j, e, o: (i, j, 0)),
                      pl.BlockSpec((1, 6, d), midx)],
            out_specs=pl.BlockSpec((1, tb, d), lambda i, j, e, o: (i, j, 0))),
        out_shape=jax.ShapeDtypeStruct((b, n, d), F32),
        compiler_params=_cparams(("parallel", "parallel", "arbitrary")),
        name="moe_combine",
    )(offs, y, dest4, h, mod)


def _moe(h, u2, lg, mod, wg, wu, wd):
    b, n, d = h.shape
    cap = EC_CAPACITY * n // N_EXPERTS
    nblk = n // TOK_BLK
    dest, gate, starts = _route(lg, cap)
    offs = starts[:, :, 0].reshape(-1)
    dest4 = dest.reshape(b, N_EXPERTS, nblk, TOK_BLK)
    gate4 = gate.reshape(b, N_EXPERTS, nblk, TOK_BLK)
    y = _moe_ffn(offs, u2, dest4, gate4, wg, wu, wd, cap)
    return _moe_combine(offs, y, dest4, h, mod, cap)


def _gla_proj_kernel(x_ref, mod_ref, g1_ref, w_ref, wz_ref, wgate_ref, bgate_ref,
                     q_ref, k_ref, v_ref, og_ref, gf_ref, gb_ref, *, dk, dv):
    m = mod_ref[0]
    u = _rms_mod(x_ref[0], g1_ref[...], m[0:1], m[1:2]).astype(BF16)
    acc = _dot(u, w_ref[...])
    z = _dot(u, wz_ref[...]).astype(BF16)
    gp = _dot(z, wgate_ref[...]) + bgate_ref[...]
    g = (jnp.minimum(gp, 0.0) - jnp.log1p(jnp.exp(-jnp.abs(gp)))) * (1.0 / GLA_GATE_NORM)
    q_ref[0] = acc[:, :dk]
    k_ref[0] = acc[:, dk:2 * dk]
    v_ref[0] = acc[:, 2 * dk:2 * dk + dv].astype(BF16)
    og_ref[0] = acc[:, 2 * dk + dv:]
    gf_ref[0] = g[:, :dk]
    gb_ref[0] = g[:, dk:]


def _gla_proj(h, mod, g1, w, wz, wgate, bgate, dk, dv):
    b, t, d = h.shape
    tm = min(512, t)
    nb_mod = mod.shape[0]
    midx = (lambda i, j: (i, 0, 0)) if nb_mod > 1 else (lambda i, j: (0, 0, 0))
    cols = 2 * dk + 2 * dv
    widths = [(dk, F32), (dk, F32), (dv, BF16), (dv, F32), (dk, F32), (dk, F32)]
    return pl.pallas_call(
        functools.partial(_gla_proj_kernel, dk=dk, dv=dv),
        grid=(b, t // tm),
        in_specs=[pl.BlockSpec((1, tm, d), lambda i, j: (i, j, 0)),
                  pl.BlockSpec((1, 6, d), midx),
                  pl.BlockSpec((1, d), lambda i, j: (0, 0)),
                  pl.BlockSpec((d, cols), lambda i, j: (0, 0)),
                  pl.BlockSpec((d, LANES), lambda i, j: (0, 0)),
                  pl.BlockSpec((LANES, 2 * dk), lambda i, j: (0, 0)),
                  pl.BlockSpec((1, 2 * dk), lambda i, j: (0, 0))],
        out_specs=[pl.BlockSpec((1, tm, wd), lambda i, j: (i, j, 0)) for wd, _ in widths],
        out_shape=[jax.ShapeDtypeStruct((b, t, wd), dt) for wd, dt in widths],
        compiler_params=_cparams(("parallel", "parallel")),
        name="gla_proj",
    )(h, mod, g1.reshape(1, d), w, wz, wgate, bgate)


def _gla_scan(q_ref, k_ref, v_ref, g_ref, acc_ref, st_ref, *, n_chunks, reverse, scale):
    c = GLA_CHUNK
    ii = lax.broadcasted_iota(I32, (c, c), 0)
    jj = lax.broadcasted_iota(I32, (c, c), 1)
    tri = (jj >= ii) if reverse else (jj <= ii)
    tri_b = jnp.where(tri, 1.0, 0.0).astype(BF16)

    def body(i, carry):
        idx = (n_chunks - 1 - i) if reverse else i
        r0 = pl.multiple_of(idx * c, c)
        q = q_ref[0, pl.ds(r0, c), :] * scale
        k = k_ref[0, pl.ds(r0, c), :]
        v = v_ref[0, pl.ds(r0, c), :]
        g_hi, g_lo = _split2(g_ref[0, pl.ds(r0, c), :])
        cum = _dot(tri_b, g_hi) + _dot(tri_b, g_lo)
        cl = cum[0:1, :] if reverse else cum[c - 1:c, :]
        q_dec = (q * jnp.exp(cum)).astype(BF16)
        k_inv = (k * jnp.exp(-cum)).astype(BF16)
        k_st = (k * jnp.exp(cl - cum)).astype(BF16)
        a = jnp.where(tri, _dot_nt(q_dec, k_inv), 0.0)
        st = st_ref[...]
        o = _dot(a.astype(BF16), v) + _dot_nt(q_dec, st.astype(BF16))
        st_ref[...] = st * jnp.exp(cl) + _dot_tn(v, k_st)
        if reverse:
            acc_ref[pl.ds(r0, c), :] += o
        else:
            acc_ref[pl.ds(r0, c), :] = o
        return carry

    lax.fori_loop(0, n_chunks, body, 0, unroll=min(2 * LOOP_UNROLL, n_chunks))


def _gla_merge(acc_ref, og_ref, gain, o_ref, t):
    tm = min(512, t)
    for r in range(t // tm):
        o = acc_ref[r * tm:(r + 1) * tm, :]
        ms = jnp.mean(o * o, axis=-1, keepdims=True)
        y = (o * lax.rsqrt(ms + EPS) * gain) * _silu(og_ref[0, r * tm:(r + 1) * tm, :])
        o_ref[0, r * tm:(r + 1) * tm, :] = y.astype(BF16)


def _gla_kernel(*refs, n, l, dk, ctx_out):
    q_ref, k_ref, v_ref, gf_ref, gb_ref, og_ref, cq_ref, ck_ref, cv_ref, cgf_ref, cgb_ref, cog_ref, on_ref = refs[:13]
    if ctx_out:
        o_ref, co_ref, acc_ref, cacc_ref, st_ref = refs[13:]
    else:
        o_ref, acc_ref, cacc_ref, st_ref = refs[13:]
    scale = dk ** -0.5
    for reverse, g_ref, cg_ref in ((False, gf_ref, cgf_ref), (True, gb_ref, cgb_ref)):
        st_ref[...] = jnp.zeros_like(st_ref)
        _gla_scan(cq_ref, ck_ref, cv_ref, cg_ref, cacc_ref, st_ref, n_chunks=l // GLA_CHUNK, reverse=reverse, scale=scale)
        _gla_scan(q_ref, k_ref, v_ref, g_ref, acc_ref, st_ref, n_chunks=n // GLA_CHUNK, reverse=reverse, scale=scale)
    gain = on_ref[...]
    _gla_merge(acc_ref, og_ref, gain, o_ref, n)
    if ctx_out:
        _gla_merge(cacc_ref, cog_ref, gain, co_ref, l)


def _gla(lat, ctx, onorm, ctx_out):
    q, k, v, og, gf, gb = lat
    cq, ck, cv, cog, cgf, cgb = ctx
    b, n, dkt = q.shape
    l = cq.shape[1]
    dvt = v.shape[2]
    dk = dkt // GLA_HEADS
    dv = dvt // GLA_HEADS

    def spec(t, w):
        return pl.BlockSpec((1, t, w), lambda i, h: (i, 0, h))

    in_specs = [spec(n, dk), spec(n, dk), spec(n, dv), spec(n, dk), spec(n, dk), spec(n, dv),
                spec(l, dk), spec(l, dk), spec(l, dv), spec(l, dk), spec(l, dk), spec(l, dv),
                pl.BlockSpec((1, dv), lambda i, h: (0, 0))]
    out_specs = [spec(n, dv)]
    out_shape = [jax.ShapeDtypeStruct((b, n, dvt), BF16)]
    if ctx_out:
        out_specs.append(spec(l, dv))
        out_shape.append(jax.ShapeDtypeStruct((b, l, dvt), BF16))
    res = pl.pallas_call(
        functools.partial(_gla_kernel, n=n, l=l, dk=dk, ctx_out=ctx_out),
        grid=(b, GLA_HEADS),
        in_specs=in_specs,
        out_specs=out_specs,
        out_shape=out_shape,
        scratch_shapes=[pltpu.VMEM((n, dv), F32), pltpu.VMEM((l, dv), F32), pltpu.VMEM((dv, dk), F32)],
        compiler_params=_cparams(("parallel", "parallel")),
        name="gla",
    )(q, k, v, gf, gb, og, cq, ck, cv, cgf, cgb, cog, onorm.reshape(1, dv))
    return (res[0], res[1]) if ctx_out else (res[0], None)


def _rope_tables(n):
    half = HEAD_DIM // 2
    t = jnp.arange(n, dtype=I32)
    inv = ROPE_THETA ** (-jnp.arange(0, half, 2, dtype=F32) / half)

    def tab(pos):
        ang = pos.astype(F32)[:, None] * inv[None, :]
        ang = jnp.concatenate([ang, ang], axis=-1)
        return jnp.cos(ang), jnp.sin(ang)

    cos_r, sin_r = tab(t // GRID_W)
    cos_c, sin_c = tab(t % GRID_W)
    cos = jnp.concatenate([cos_r, cos_c], axis=-1)
    sin = jnp.concatenate([sin_r, sin_c], axis=-1)
    sign = jnp.where((jnp.arange(HEAD_DIM) % half) < half // 2, -1.0, 1.0).astype(F32)
    reps = LANES // HEAD_DIM
    return jnp.tile(cos, (1, reps)), jnp.tile(sin * sign[None, :], (1, reps))


def _block_ones():
    i = np.arange(LANES)
    return jnp.asarray((i[:, None] // HEAD_DIM) == (i[None, :] // HEAD_DIM), BF16)


def _attn_gains(qa_g, ka_g, qb_g, kb_g):
    qs = HEAD_DIM ** -0.5
    one_a = jnp.ones((A_KV_HEADS * HEAD_DIM,), F32)
    one_b = jnp.ones((B_HEADS * HEAD_DIM,), F32)
    return jnp.concatenate([jnp.tile(qa_g, A_HEADS) * qs, jnp.tile(ka_g, A_KV_HEADS), one_a,
                            jnp.tile(qb_g, B_HEADS) * qs, jnp.tile(kb_g, B_HEADS), one_b]).reshape(1, -1)


def kernel(x, c, ctx, c_ctx, ada_w, ada_b, norm1_g, norm2_g, attn_w_in, attn_w_out, a_q_norm, a_k_norm, b_q_norm,
           b_k_norm, na_rpb, gla_w_in, gla_gk_w_f, gla_gk_b_f, gla_gk_w_b, gla_gk_b_b, gla_o_norm, gla_w_out,
           moe_router, moe_w_gate, moe_w_up, moe_w_down):
    b, n, d = x.shape
    depth = ada_w.shape[0]
    dk = gla_gk_w_f.shape[2]
    dv = gla_w_out.shape[1]
    rows = -(-(b + 1) // 8) * 8
    cvec = jnp.zeros((rows, d), F32).at[:b].set(c).at[b].set(c_ctx)
    mods = _mods(cvec, ada_w, ada_b)
    cos, sin = _rope_tables(n)
    bd = _block_ones()
    h, hc = x, ctx
    for layer in range(depth):
        last = layer == depth - 1
        i = layer // 2
        ml = mods[layer, :b].reshape(b, 6, d)
        mc = mods[layer, b:b + 1].reshape(1, 6, d)
        if layer % 2 == 0:
            w_in = attn_w_in[i].astype(BF16)
            gains = _attn_gains(a_q_norm[i], a_k_norm[i], b_q_norm[i], b_k_norm[i])
            qa, ka, va, qb, kb, vb = _attn_proj(h, ml, norm1_g[layer], w_in, gains, bd, cos, sin)
            cqa, cka, cva, cqb, ckb, cvb = _attn_proj(hc, mc, norm1_g[layer], w_in, gains, bd)
            oa = _gqa(qa, jnp.concatenate([ka, cka], axis=1), jnp.concatenate([va, cva], axis=1), A_HEADS, A_KV_HEADS)
            ob = _na(qb, kb, vb, ckb, cvb, _na_bias_table(na_rpb[i], n // GRID_W))
            o_lat = [oa, ob]
            o_ctx = None if last else [_gqa(cqa, cka, cva, A_HEADS, A_KV_HEADS), _gqa(cqb, ckb, cvb, B_HEADS, B_HEADS)]
            w_out = attn_w_out[i].astype(BF16)
        else:
            w_main = gla_w_in[i][:, :2 * dk + 2 * dv].astype(BF16)
            wz = jnp.pad(gla_w_in[i][:, 2 * dk + 2 * dv:], ((0, 0), (0, LANES - 2 * GLA_GATE_RANK))).astype(BF16)
            wgate = jnp.zeros((LANES, 2 * dk), F32)
            wgate = wgate.at[:GLA_GATE_RANK, :dk].set(gla_gk_w_f[i]).at[GLA_GATE_RANK:2 * GLA_GATE_RANK, dk:].set(gla_gk_w_b[i])
            bgate = jnp.concatenate([gla_gk_b_f[i], gla_gk_b_b[i]]).reshape(1, 2 * dk)
            lat = _gla_proj(h, ml, norm1_g[layer], w_main, wz, wgate.astype(BF16), bgate, dk, dv)
            cx = _gla_proj(hc, mc, norm1_g[layer], w_main, wz, wgate.astype(BF16), bgate, dk, dv)
            o, co = _gla(lat, cx, gla_o_norm[i], not last)
            o_lat = [o]
            o_ctx = None if last else [co]
            w_out = gla_w_out[i].astype(BF16)
        wr_hi, wr_lo = _split2(moe_router[layer].T)
        wg = moe_w_gate[layer].astype(BF16)
        wu = moe_w_up[layer].astype(BF16)
        wd = moe_w_down[layer].astype(BF16)
        h1, u2, lg = _outproj(o_lat, h, ml, norm2_g[layer], w_out, wr_hi, wr_lo)
        h = _moe(h1, u2, lg, ml, wg, wu, wd)
        if not last:
            hc1, uc2, lgc = _outproj(o_ctx, hc, mc, norm2_g[layer], w_out, wr_hi, wr_lo)
            hc = _moe(hc1, uc2, lgc, mc, wg, wu, wd)
    return h
```

```python
import functools

import numpy as np
import jax
import jax.numpy as jnp
from jax import lax
from jax.experimental import pallas as pl
from jax.experimental.pallas import tpu as pltpu

F32 = jnp.float32
BF16 = jnp.bfloat16
I32 = jnp.int32

GRID_W = 64
HEAD_DIM = 64
A_HEADS = 8
A_KV_HEADS = 2
B_HEADS = 8
NA_ROWS = 8
NA_COLS = 16
ROPE_THETA = 10000.0
GLA_HEADS = 4
GLA_GATE_RANK = 16
GLA_GATE_NORM = 16.0
GLA_CHUNK = 64
GLA_GROUP = 4
N_EXPERTS = 16
EC_CAPACITY = 2
EPS = 1e-6

LANES = 128
BF16_ROWS = 16
NA_QROWS = 4
TOK_BLK = LANES
LOOP_UNROLL = 4
MASK_VALUE = -1e30
VMEM_LIMIT = 56 * 1024 * 1024

_NT = (((1,), (1,)), ((), ()))
_TN = (((0,), (0,)), ((), ()))


def _cparams(sem):
    return pltpu.CompilerParams(dimension_semantics=sem, vmem_limit_bytes=VMEM_LIMIT)


def _dot(a, b):
    return jnp.dot(a, b, preferred_element_type=F32)


def _dot_nt(a, b):
    return lax.dot_general(a, b, _NT, preferred_element_type=F32)


def _dot_tn(a, b):
    return lax.dot_general(a, b, _TN, preferred_element_type=F32)


def _split2(a):
    hi = a.astype(BF16)
    lo = (a - hi.astype(F32)).astype(BF16)
    return hi, lo


def _silu(a):
    return a / (1.0 + jnp.exp(-a))


def _rms_mod(x, g, shift, scale):
    ms = jnp.mean(x * x, axis=-1, keepdims=True)
    return (x * lax.rsqrt(ms + EPS) * g) * (1.0 + scale) + shift


def _mods_kernel(c_ref, w_ref, b_ref, o_ref):
    c = c_ref[...]
    s_hi, s_lo = _split2(_silu(c))
    w_hi, w_lo = _split2(w_ref[0])
    o_ref[0] = _dot(s_hi, w_hi) + _dot(s_lo, w_hi) + _dot(s_hi, w_lo) + b_ref[0]


def _mods(cvec, ada_w, ada_b):
    depth, d, d6 = ada_w.shape
    rows = cvec.shape[0]
    tn = 1536 if d6 % 1536 == 0 else d6
    return pl.pallas_call(
        _mods_kernel,
        grid=(depth, d6 // tn),
        in_specs=[pl.BlockSpec((rows, d), lambda l, j: (0, 0)),
                  pl.BlockSpec((1, d, tn), lambda l, j: (l, 0, j)),
                  pl.BlockSpec((1, 1, tn), lambda l, j: (l, 0, j))],
        out_specs=pl.BlockSpec((1, rows, tn), lambda l, j: (l, 0, j)),
        out_shape=jax.ShapeDtypeStruct((depth, rows, d6), F32),
        compiler_params=_cparams(("arbitrary", "arbitrary")),
        name="mods",
    )(cvec, ada_w, ada_b.reshape(depth, 1, d6))


_ATTN_COLS = (A_HEADS + 2 * A_KV_HEADS + 3 * B_HEADS) * HEAD_DIM
_QA = (0, 4)
_KA = (4, 5)
_VA = (5, 6)
_QB = (6, 10)
_KB = (10, 14)
_VB = (14, 18)


def _attn_proj_kernel(*refs, rope):
    if rope:
        x_ref, mod_ref, g1_ref, w_ref, gain_ref, bd_ref, cos_ref, sin_ref = refs[:8]
        outs = refs[8:]
    else:
        x_ref, mod_ref, g1_ref, w_ref, gain_ref, bd_ref = refs[:6]
        outs = refs[6:]
    qa_ref, ka_ref, va_ref, qb_ref, kb_ref, vb_ref = outs
    m = mod_ref[0]
    u = _rms_mod(x_ref[0], g1_ref[...], m[0:1], m[1:2]).astype(BF16)
    acc = _dot(u, w_ref[...])
    bd = bd_ref[...]
    if rope:
        cos = cos_ref[...]
        sin = sin_ref[...]
        lane = lax.broadcasted_iota(I32, cos.shape, 1)
        first = (lane & 31) < 16

    def chunk(j, norm, rot):
        cch = acc[:, j * LANES:(j + 1) * LANES]
        if norm:
            sq_hi, sq_lo = _split2(cch * cch)
            ss = _dot(sq_hi, bd) + _dot(sq_lo, bd)
            cch = cch * lax.rsqrt(ss * (1.0 / HEAD_DIM) + EPS) * gain_ref[:, j * LANES:(j + 1) * LANES]
        if rot:
            partner = jnp.where(first, pltpu.roll(cch, LANES - 16, 1), pltpu.roll(cch, 16, 1))
            cch = cch * cos + partner * sin
        return cch.astype(BF16)

    for (lo, hi), ref, norm, rot in ((_QA, qa_ref, True, rope), (_KA, ka_ref, True, rope), (_VA, va_ref, False, False),
                                     (_QB, qb_ref, True, False), (_KB, kb_ref, True, False), (_VB, vb_ref, False, False)):
        for j in range(lo, hi):
            ref[0, :, (j - lo) * LANES:(j - lo + 1) * LANES] = chunk(j, norm, rot)


def _attn_proj(h, mod, g1, w, gains, bd, cos=None, sin=None):
    b, t, d = h.shape
    tm = min(512, t)
    rope = cos is not None
    nb_mod = mod.shape[0]
    midx = (lambda i, j: (i, 0, 0)) if nb_mod > 1 else (lambda i, j: (0, 0, 0))
    in_specs = [pl.BlockSpec((1, tm, d), lambda i, j: (i, j, 0)),
                pl.BlockSpec((1, 6, d), midx),
                pl.BlockSpec((1, d), lambda i, j: (0, 0)),
                pl.BlockSpec((d, _ATTN_COLS), lambda i, j: (0, 0)),
                pl.BlockSpec((1, _ATTN_COLS), lambda i, j: (0, 0)),
                pl.BlockSpec((LANES, LANES), lambda i, j: (0, 0))]
    args = [h, mod, g1.reshape(1, d), w, gains, bd]
    if rope:
        in_specs += [pl.BlockSpec((tm, LANES), lambda i, j: (j, 0))] * 2
        args += [cos, sin]
    widths = [(hi - lo) * LANES for lo, hi in (_QA, _KA, _VA, _QB, _KB, _VB)]
    return pl.pallas_call(
        functools.partial(_attn_proj_kernel, rope=rope),
        grid=(b, t // tm),
        in_specs=in_specs,
        out_specs=[pl.BlockSpec((1, tm, wd), lambda i, j: (i, j, 0)) for wd in widths],
        out_shape=[jax.ShapeDtypeStruct((b, t, wd), BF16) for wd in widths],
        compiler_params=_cparams(("parallel", "parallel")),
        name="attn_proj_rope" if rope else "attn_proj",
    )(*args)


def _gqa_kernel(q_ref, k_ref, v_ref, o_ref, *, hq, hk):
    grp = hq // hk
    for h in range(hq):
        kv = h // grp
        q = q_ref[0, :, h * HEAD_DIM:(h + 1) * HEAD_DIM]
        k = k_ref[0, :, kv * HEAD_DIM:(kv + 1) * HEAD_DIM]
        v = v_ref[0, :, kv * HEAD_DIM:(kv + 1) * HEAD_DIM]
        s = _dot_nt(q, k)
        p = jnp.exp(s - jnp.max(s, axis=-1, keepdims=True))
        l = jnp.sum(p, axis=-1, keepdims=True)
        o = _dot(p.astype(BF16), v) / l
        o_ref[0, :, h * HEAD_DIM:(h + 1) * HEAD_DIM] = o.astype(BF16)


def _gqa(q, k, v, hq, hk):
    b, t, _ = q.shape
    s = k.shape[1]
    tq = min(256, t)
    return pl.pallas_call(
        functools.partial(_gqa_kernel, hq=hq, hk=hk),
        grid=(b, t // tq),
        in_specs=[pl.BlockSpec((1, tq, hq * HEAD_DIM), lambda i, j: (i, j, 0)),
                  pl.BlockSpec((1, s, hk * HEAD_DIM), lambda i, j: (i, 0, 0)),
                  pl.BlockSpec((1, s, hk * HEAD_DIM), lambda i, j: (i, 0, 0))],
        out_specs=pl.BlockSpec((1, tq, hq * HEAD_DIM), lambda i, j: (i, j, 0)),
        out_shape=jax.ShapeDtypeStruct((b, t, hq * HEAD_DIM), BF16),
        compiler_params=_cparams(("parallel", "parallel")),
        name="gqa",
    )(q, k, v)


def _na_geometry(rows):
    kr = min(NA_QROWS + NA_ROWS - 1, rows)
    nblk = rows // NA_QROWS
    return kr, nblk


def _na_bias_table(rpb, rows):
    kr, nblk = _na_geometry(rows)
    wr = min(NA_ROWS, rows)
    heads = rpb.shape[0]
    qc = np.arange(GRID_W)[:, None]
    kc = np.arange(GRID_W)[None, :]
    cs = np.clip(qc - NA_COLS // 2, 0, GRID_W - NA_COLS)
    col_ok = (kc >= cs) & (kc < cs + NA_COLS)
    pick = (np.arange(2 * NA_COLS - 1)[:, None, None] == (kc - qc + NA_COLS - 1)[None]) & col_ok[None]
    toeplitz = jnp.einsum("hrd,dqk->hrqk", rpb.astype(F32), jnp.asarray(pick, F32), precision=lax.Precision.HIGHEST)
    toeplitz = jnp.where(col_ok[None, None], toeplitz, MASK_VALUE)
    tabs = []
    for rb in (0, min(1, nblk - 1), nblk - 1):
        qr0 = rb * NA_QROWS
        kr0 = int(np.clip(qr0 - NA_ROWS // 2, 0, rows - kr))
        qr = qr0 + np.arange(NA_QROWS)[:, None]
        krr = kr0 + np.arange(kr)[None, :]
        rs = np.clip(qr - wr // 2, 0, rows - wr)
        row_ok = (krr >= rs) & (krr < rs + wr)
        dr = krr - qr + (NA_ROWS - 1)
        masked = jnp.full((heads, GRID_W, GRID_W), MASK_VALUE, F32)
        blocks = [jnp.concatenate([toeplitz[:, dr[i, j]] if row_ok[i, j] else masked for j in range(kr)], axis=2)
                  for i in range(NA_QROWS)]
        tabs.append(jnp.concatenate(blocks, axis=1))
    return jnp.stack(tabs)


def _na_kernel(q_ref, k_ref, v_ref, kc_ref, vc_ref, bias_ref, o_ref, *, rows, heads):
    kr, _ = _na_geometry(rows)
    rb = pl.program_id(1)
    kr0 = jnp.clip(rb * NA_QROWS - NA_ROWS // 2, 0, rows - kr)
    start = pl.multiple_of(kr0 * GRID_W, GRID_W)
    kw = k_ref[0, pl.ds(start, kr * GRID_W), :]
    vw = v_ref[0, pl.ds(start, kr * GRID_W), :]
    for h in range(heads):
        sl = slice(h * HEAD_DIM, (h + 1) * HEAD_DIM)
        q = q_ref[0, :, sl]
        s_w = _dot_nt(q, kw[:, sl]) + bias_ref[0, h]
        s_c = _dot_nt(q, kc_ref[0, :, sl])
        m = jnp.maximum(jnp.max(s_w, axis=-1, keepdims=True), jnp.max(s_c, axis=-1, keepdims=True))
        p_w = jnp.exp(s_w - m)
        p_c = jnp.exp(s_c - m)
        l = jnp.sum(p_w, axis=-1, keepdims=True) + jnp.sum(p_c, axis=-1, keepdims=True)
        o = (_dot(p_w.astype(BF16), vw[:, sl]) + _dot(p_c.astype(BF16), vc_ref[0, :, sl])) / l
        o_ref[0, :, sl] = o.astype(BF16)


def _na(q, k, v, kc, vc, bias):
    b, n, c = q.shape
    l = kc.shape[1]
    rows = n // GRID_W
    kr, nblk = _na_geometry(rows)
    tq = NA_QROWS * GRID_W
    heads = c // HEAD_DIM

    def pattern(i, j):
        return (jnp.where(j == 0, 0, jnp.where(j == nblk - 1, 2, 1)), 0, 0, 0)

    return pl.pallas_call(
        functools.partial(_na_kernel, rows=rows, heads=heads),
        grid=(b, nblk),
        in_specs=[pl.BlockSpec((1, tq, c), lambda i, j: (i, j, 0)),
                  pl.BlockSpec((1, n, c), lambda i, j: (i, 0, 0)),
                  pl.BlockSpec((1, n, c), lambda i, j: (i, 0, 0)),
                  pl.BlockSpec((1, l, c), lambda i, j: (i, 0, 0)),
                  pl.BlockSpec((1, l, c), lambda i, j: (i, 0, 0)),
                  pl.BlockSpec((1, heads, tq, kr * GRID_W), pattern)],
        out_specs=pl.BlockSpec((1, tq, c), lambda i, j: (i, j, 0)),
        out_shape=jax.ShapeDtypeStruct((b, n, c), BF16),
        compiler_params=_cparams(("parallel", "arbitrary")),
        name="na_attn",
    )(q, k, v, kc, vc, bias)


def _outproj_kernel(*refs, n_o, route):
    o_refs = refs[:n_o]
    h_ref, mod_ref, g2_ref, w_ref = refs[n_o:n_o + 4]
    rest = refs[n_o + 4:]
    if route:
        wr_hi_ref, wr_lo_ref, hn_ref, u2_ref, lg_ref = rest
    else:
        hn_ref, = rest
    m = mod_ref[0]
    o = o_refs[0][0] if n_o == 1 else jnp.concatenate([r[0] for r in o_refs], axis=1)
    hn = h_ref[0] + m[2:3] * _dot(o, w_ref[...])
    hn_ref[0] = hn
    if route:
        u2 = _rms_mod(hn, g2_ref[...], m[3:4], m[4:5])
        u_hi, u_lo = _split2(u2)
        u2_ref[0] = u_hi
        wr_hi = wr_hi_ref[...]
        lg_ref[0] = _dot_nt(wr_hi, u_hi) + _dot_nt(wr_hi, u_lo) + _dot_nt(wr_lo_ref[...], u_hi)


def _outproj(o_list, h, mod, g2, w, wr_hi=None, wr_lo=None):
    b, t, d = h.shape
    tm = min(512, t)
    route = wr_hi is not None
    nb_mod = mod.shape[0]
    midx = (lambda i, j: (i, 0, 0)) if nb_mod > 1 else (lambda i, j: (0, 0, 0))
    in_specs = [pl.BlockSpec((1, tm, o.shape[2]), lambda i, j: (i, j, 0)) for o in o_list]
    in_specs += [pl.BlockSpec((1, tm, d), lambda i, j: (i, j, 0)),
                 pl.BlockSpec((1, 6, d), midx),
                 pl.BlockSpec((1, d), lambda i, j: (0, 0)),
                 pl.BlockSpec((d, d), lambda i, j: (0, 0))]
    args = list(o_list) + [h, mod, g2.reshape(1, d), w]
    out_specs = [pl.BlockSpec((1, tm, d), lambda i, j: (i, j, 0))]
    out_shape = [jax.ShapeDtypeStruct((b, t, d), F32)]
    if route:
        in_specs += [pl.BlockSpec((N_EXPERTS, d), lambda i, j: (0, 0))] * 2
        args += [wr_hi, wr_lo]
        out_specs += [pl.BlockSpec((1, tm, d), lambda i, j: (i, j, 0)),
                      pl.BlockSpec((1, N_EXPERTS, tm), lambda i, j: (i, 0, j))]
        out_shape += [jax.ShapeDtypeStruct((b, t, d), BF16), jax.ShapeDtypeStruct((b, N_EXPERTS, t), F32)]
    return pl.pallas_call(
        functools.partial(_outproj_kernel, n_o=len(o_list), route=route),
        grid=(b, t // tm),
        in_specs=in_specs,
        out_specs=out_specs,
        out_shape=out_shape,
        compiler_params=_cparams(("parallel", "parallel")),
        name="outproj",
    )(*args)


def _cumsum_excl(x):
    e, n = x.shape
    nb = n // LANES
    st = jnp.concatenate([x[:, j * LANES:(j + 1) * LANES] for j in range(nb)], axis=0)
    ii = lax.broadcasted_iota(I32, (LANES, LANES), 0)
    jj = lax.broadcasted_iota(I32, (LANES, LANES), 1)
    incl = _dot(st.astype(BF16), jnp.where(ii <= jj, 1.0, 0.0).astype(BF16))
    tot = jnp.broadcast_to(incl[:, LANES - 1:LANES], incl.shape)
    r = lax.broadcasted_iota(I32, (nb * e, nb * e), 0)
    c = lax.broadcasted_iota(I32, (nb * e, nb * e), 1)
    sh = e.bit_length() - 1
    same = (r & (e - 1)) == (c & (e - 1))
    prev = jnp.where(same & (lax.shift_right_logical(c, sh) < lax.shift_right_logical(r, sh)), 1.0, 0.0).astype(BF16)
    starts = _dot(prev, tot.astype(BF16))
    excl_st = incl - st + starts
    excl = jnp.concatenate([excl_st[j * e:(j + 1) * e, :] for j in range(nb)], axis=1)
    return excl, starts


def _route_kernel(lg_ref, dest_ref, gate_ref, offs_ref, *, cap):
    x = lg_ref[0]
    ex = jnp.exp(x - jnp.max(x, axis=0, keepdims=True))
    aff = ex / jnp.sum(ex, axis=0, keepdims=True)
    bits = lax.bitcast_convert_type(aff, I32)

    def search(i, cur):
        cand = cur | lax.shift_left(jnp.int32(1), 30 - i)
        cnt = jnp.sum(jnp.where(bits >= cand, 1.0, 0.0), axis=1, keepdims=True)
        return jnp.where(cnt >= cap, cand, cur)

    thr = lax.fori_loop(0, 31, search, jnp.zeros((x.shape[0], 1), I32))
    gt = bits > thr
    eq = bits == thr
    need = cap - jnp.sum(jnp.where(gt, 1.0, 0.0), axis=1, keepdims=True)
    rank_eq, _ = _cumsum_excl(jnp.where(eq, 1.0, 0.0))
    sel = gt | (eq & (rank_eq < need))
    pos, starts = _cumsum_excl(jnp.where(sel, 1.0, 0.0))
    dest_ref[0] = jnp.where(sel, pos.astype(I32), -1)
    gate_ref[0] = jnp.where(sel, aff, 0.0)
    offs_ref[0] = starts.astype(I32)


def _route(lg, cap):
    b, e, n = lg.shape
    nb = n // LANES
    return pl.pallas_call(
        functools.partial(_route_kernel, cap=cap),
        grid=(b,),
        in_specs=[pl.BlockSpec((1, e, n), lambda i: (i, 0, 0))],
        out_specs=[pl.BlockSpec((1, e, n), lambda i: (i, 0, 0)),
                   pl.BlockSpec((1, e, n), lambda i: (i, 0, 0)),
                   pl.BlockSpec((1, nb * e, LANES), lambda i: (i, 0, 0))],
        out_shape=[jax.ShapeDtypeStruct((b, e, n), I32), jax.ShapeDtypeStruct((b, e, n), F32),
                   jax.ShapeDtypeStruct((b, nb * e, LANES), I32)],
        compiler_params=_cparams(("parallel",)),
        name="route",
    )(lg)


def _window(offs_ref, b, j, e, nblk, cap, win):
    off = offs_ref[(b * nblk + j) * N_EXPERTS + e]
    off_al = jnp.minimum(lax.shift_right_logical(off, 4) * BF16_ROWS, cap - win)
    return pl.multiple_of(off_al, BF16_ROWS)


def _onehot(dest_row, off_al, win):
    rows = lax.broadcasted_iota(I32, (win, TOK_BLK), 0) + off_al
    return jnp.where(rows == dest_row, 1.0, 0.0)


def _moe_ffn_kernel(offs_ref, u_ref, dest_ref, gate_ref, wg_ref, wu_ref, wd_ref, y_ref,
                    xg_ref, gb_ref, wgb_ref, wub_ref, wdb_ref, *, cap, nblk, win, nbatch, batch_ffn):
    e = pl.program_id(0)
    b = pl.program_id(1)

    @pl.when(b == 0)
    def _():
        wgb_ref[...] = wg_ref[0, 0].astype(BF16)
        wub_ref[...] = wu_ref[0, 0].astype(BF16)
        wdb_ref[...] = wd_ref[0, 0].astype(BF16)

    row0 = pl.multiple_of(b * cap, BF16_ROWS) if batch_ffn else 0
    xg_ref[pl.ds(row0, cap), :] = jnp.zeros((cap, xg_ref.shape[1]), F32)
    gb_ref[pl.ds(row0, cap), :] = jnp.zeros((cap, LANES), F32)

    def gather(j, carry):
        off_al = _window(offs_ref, b, j, e, nblk, cap, win)
        g = _onehot(dest_ref[0, 0, pl.ds(j, 1), :], off_al, win)
        uj = u_ref[0, pl.ds(pl.multiple_of(j * TOK_BLK, TOK_BLK), TOK_BLK), :]
        r = pl.multiple_of(row0 + off_al, BF16_ROWS)
        xg_ref[pl.ds(r, win), :] += _dot(g.astype(BF16), uj)
        gsum = jnp.sum(g * gate_ref[0, 0, pl.ds(j, 1), :], axis=1, keepdims=True)
        gb_ref[pl.ds(r, win), :] += jnp.broadcast_to(gsum, (win, LANES))
        return carry

    lax.fori_loop(0, nblk, gather, 0, unroll=min(LOOP_UNROLL, nblk))

    def ffn(r0, m):
        x = xg_ref[pl.ds(r0, m), :].astype(BF16)
        ff = wgb_ref.shape[1]
        fc = min(512, ff)
        acc = jnp.zeros((m, wdb_ref.shape[1]), F32)
        for c in range(ff // fc):
            a = _dot(x, wgb_ref[:, c * fc:(c + 1) * fc])
            up = _dot(x, wub_ref[:, c * fc:(c + 1) * fc])
            acc = acc + _dot((_silu(a) * up).astype(BF16), wdb_ref[c * fc:(c + 1) * fc, :])
        return (acc * gb_ref[pl.ds(r0, m), 0:1]).astype(BF16)

    if batch_ffn:
        @pl.when(b == nbatch - 1)
        def _():
            y = ffn(0, nbatch * cap)
            for bb in range(nbatch):
                y_ref[bb, 0] = y[bb * cap:(bb + 1) * cap]
    else:
        y_ref[0, 0] = ffn(0, cap)


def _moe_ffn(offs, u, dest4, gate4, wg, wu, wd, layer, cap):
    b, n, d = u.shape
    nblk = n // TOK_BLK
    win = min(TOK_BLK + BF16_ROWS, cap)
    ff = wg.shape[3]
    batch_ffn = cap < 256
    rows = b * cap if batch_ffn else cap
    if batch_ffn:
        y_spec = pl.BlockSpec((b, 1, cap, d), lambda e, i, o: (0, e, 0, 0))
    else:
        y_spec = pl.BlockSpec((1, 1, cap, d), lambda e, i, o: (i, e, 0, 0))
    return pl.pallas_call(
        functools.partial(_moe_ffn_kernel, cap=cap, nblk=nblk, win=win, nbatch=b, batch_ffn=batch_ffn),
        grid_spec=pltpu.PrefetchScalarGridSpec(
            num_scalar_prefetch=1,
            grid=(N_EXPERTS, b),
            in_specs=[pl.BlockSpec((1, n, d), lambda e, i, o: (i, 0, 0)),
                      pl.BlockSpec((1, 1, nblk, TOK_BLK), lambda e, i, o: (i, e, 0, 0)),
                      pl.BlockSpec((1, 1, nblk, TOK_BLK), lambda e, i, o: (i, e, 0, 0)),
                      pl.BlockSpec((1, 1, d, ff), lambda e, i, o: (layer, e, 0, 0)),
                      pl.BlockSpec((1, 1, d, ff), lambda e, i, o: (layer, e, 0, 0)),
                      pl.BlockSpec((1, 1, ff, d), lambda e, i, o: (layer, e, 0, 0))],
            out_specs=y_spec,
            scratch_shapes=[pltpu.VMEM((rows, d), F32), pltpu.VMEM((rows, LANES), F32),
                            pltpu.VMEM((d, ff), BF16), pltpu.VMEM((d, ff), BF16), pltpu.VMEM((ff, d), BF16)]),
        out_shape=jax.ShapeDtypeStruct((b, N_EXPERTS, cap, d), BF16),
        compiler_params=_cparams(("arbitrary", "arbitrary")),
        name="moe_ffn",
    )(offs, u, dest4, gate4, wg, wu, wd)


def _moe_combine_kernel(offs_ref, y_ref, dest_ref, h_ref, mod_ref, out_ref, *, cap, nblk, win, nsub):
    b = pl.program_id(0)
    jb = pl.program_id(1)
    e = pl.program_id(2)

    @pl.when(e == 0)
    def _():
        out_ref[...] = jnp.zeros_like(out_ref)

    for s in range(nsub):
        off_al = _window(offs_ref, b, jb * nsub + s, e, nblk, cap, win)
        g = _onehot(dest_ref[0, 0, s:s + 1, :], off_al, win).astype(BF16)
        yw = y_ref[0, 0, pl.ds(off_al, win), :]
        out_ref[0, s * TOK_BLK:(s + 1) * TOK_BLK, :] += _dot_tn(g, yw)

    @pl.when(e == N_EXPERTS - 1)
    def _():
        out_ref[0] = h_ref[0] + mod_ref[0][5:6] * out_ref[0]


def _moe_combine(offs, y, dest4, h, mod, cap):
    b, n, d = h.shape
    nblk = n // TOK_BLK
    win = min(TOK_BLK + BF16_ROWS, cap)
    nsub = min(8, nblk)
    tb = nsub * TOK_BLK
    nb_mod = mod.shape[0]
    midx = (lambda i, j, e, o: (i, 0, 0)) if nb_mod > 1 else (lambda i, j, e, o: (0, 0, 0))
    return pl.pallas_call(
        functools.partial(_moe_combine_kernel, cap=cap, nblk=nblk, win=win, nsub=nsub),
        grid_spec=pltpu.PrefetchScalarGridSpec(
            num_scalar_prefetch=1,
            grid=(b, n // tb, N_EXPERTS),
            in_specs=[pl.BlockSpec((1, 1, cap, d), lambda i, j, e, o: (i, e, 0, 0)),
                      pl.BlockSpec((1, 1, nsub, TOK_BLK), lambda i, j, e, o: (i, e, j, 0)),
                      pl.BlockSpec((1, tb, d), lambda i, j, e, o: (i, j, 0)),
                      pl.BlockSpec((1, 6, d), midx)],
            out_specs=pl.BlockSpec((1, tb, d), lambda i, j, e, o: (i, j, 0))),
        out_shape=jax.ShapeDtypeStruct((b, n, d), F32),
        compiler_params=_cparams(("parallel", "parallel", "arbitrary")),
        name="moe_combine",
    )(offs, y, dest4, h, mod)


def _moe(h, u2, lg, mod, wg, wu, wd, layer):
    b, n, d = h.shape
    cap = EC_CAPACITY * n // N_EXPERTS
    nblk = n // TOK_BLK
    dest, gate, starts = _route(lg, cap)
    offs = starts[:, :, 0].reshape(-1)
    dest4 = dest.reshape(b, N_EXPERTS, nblk, TOK_BLK)
    gate4 = gate.reshape(b, N_EXPERTS, nblk, TOK_BLK)
    y = _moe_ffn(offs, u2, dest4, gate4, wg, wu, wd, layer, cap)
    return _moe_combine(offs, y, dest4, h, mod, cap)


def _gla_proj_kernel(x_ref, mod_ref, g1_ref, w_ref, wz_ref, wgate_ref, bgate_ref,
                     q_ref, k_ref, v_ref, og_ref, gf_ref, gb_ref, *, dk, dv):
    m = mod_ref[0]
    u = _rms_mod(x_ref[0], g1_ref[...], m[0:1], m[1:2]).astype(BF16)
    acc = _dot(u, w_ref[...])
    z = _dot(u, wz_ref[...]).astype(BF16)
    gp = _dot(z, wgate_ref[...]) + bgate_ref[...]
    g = (jnp.minimum(gp, 0.0) - jnp.log1p(jnp.exp(-jnp.abs(gp)))) * (1.0 / GLA_GATE_NORM)
    q_ref[0] = acc[:, :dk]
    k_ref[0] = acc[:, dk:2 * dk]
    v_ref[0] = acc[:, 2 * dk:2 * dk + dv].astype(BF16)
    og_ref[0] = acc[:, 2 * dk + dv:]
    gf_ref[0] = g[:, :dk]
    gb_ref[0] = g[:, dk:]


def _gla_proj(h, mod, g1, w, wz, wgate, bgate, dk, dv):
    b, t, d = h.shape
    tm = min(512, t)
    nb_mod = mod.shape[0]
    midx = (lambda i, j: (i, 0, 0)) if nb_mod > 1 else (lambda i, j: (0, 0, 0))
    cols = 2 * dk + 2 * dv
    widths = [(dk, F32), (dk, F32), (dv, BF16), (dv, F32), (dk, F32), (dk, F32)]
    return pl.pallas_call(
        functools.partial(_gla_proj_kernel, dk=dk, dv=dv),
        grid=(b, t // tm),
        in_specs=[pl.BlockSpec((1, tm, d), lambda i, j: (i, j, 0)),
                  pl.BlockSpec((1, 6, d), midx),
                  pl.BlockSpec((1, d), lambda i, j: (0, 0)),
                  pl.BlockSpec((d, cols), lambda i, j: (0, 0)),
                  pl.BlockSpec((d, LANES), lambda i, j: (0, 0)),
                  pl.BlockSpec((LANES, 2 * dk), lambda i, j: (0, 0)),
                  pl.BlockSpec((1, 2 * dk), lambda i, j: (0, 0))],
        out_specs=[pl.BlockSpec((1, tm, wd), lambda i, j: (i, j, 0)) for wd, _ in widths],
        out_shape=[jax.ShapeDtypeStruct((b, t, wd), dt) for wd, dt in widths],
        compiler_params=_cparams(("parallel", "parallel")),
        name="gla_proj",
    )(h, mod, g1.reshape(1, d), w, wz, wgate, bgate)


def _gla_scan(q_ref, k_ref, v_ref, g_ref, acc_ref, st_ref, *, n_rows, reverse, scale):
    c = GLA_CHUNK
    gr = min(GLA_GROUP * c, n_rows)
    cpg = gr // c
    gpi = min(2, n_rows // gr)
    rows_it = gpi * gr
    n_it = n_rows // rows_it
    ii = lax.broadcasted_iota(I32, (gr, gr), 0)
    jj = lax.broadcasted_iota(I32, (gr, gr), 1)
    sh = c.bit_length() - 1
    same = (ii >> sh) == (jj >> sh)
    tri = same & ((jj >= ii) if reverse else (jj <= ii))
    tri_b = jnp.where(tri, 1.0, 0.0).astype(BF16)
    g_order = range(gpi - 1, -1, -1) if reverse else range(gpi)
    c_order = range(cpg - 1, -1, -1) if reverse else range(cpg)

    def body(i, carry):
        r0 = 0 if n_it == 1 else pl.multiple_of(((n_it - 1 - i) if reverse else i) * rows_it, rows_it)
        q_all = q_ref[0, pl.ds(r0, rows_it), :]
        k_all = k_ref[0, pl.ds(r0, rows_it), :]
        v_all = v_ref[0, pl.ds(r0, rows_it), :]
        g_all = g_ref[0, pl.ds(r0, rows_it), :]
        prev = acc_ref[pl.ds(r0, rows_it), :] if reverse else None
        st = st_ref[...]
        outs = [None] * gpi
        for gi in g_order:
            rs = slice(gi * gr, (gi + 1) * gr)
            k = k_all[rs]
            v = v_all[rs]
            g_hi, g_lo = _split2(g_all[rs])
            cum = _dot(tri_b, g_hi) + _dot(tri_b, g_lo)
            cl = [cum[ci * c:ci * c + 1] if reverse else cum[(ci + 1) * c - 1:(ci + 1) * c] for ci in range(cpg)]
            cl_rows = jnp.concatenate([jnp.broadcast_to(x, (c, x.shape[1])) for x in cl], axis=0)
            q_dec = (q_all[rs] * scale * jnp.exp(cum)).astype(BF16)
            k_inv = (k * jnp.exp(-cum)).astype(BF16)
            k_st = (k * jnp.exp(cl_rows - cum)).astype(BF16)
            a = jnp.where(tri, _dot_nt(q_dec, k_inv), 0.0)
            o_intra = _dot(a.astype(BF16), v)
            parts = [None] * cpg
            for ci in c_order:
                cs = slice(ci * c, (ci + 1) * c)
                parts[ci] = o_intra[cs] + _dot_nt(q_dec[cs], st.astype(BF16))
                st = st * jnp.exp(cl[ci]) + _dot_tn(v[cs], k_st[cs])
            outs[gi] = jnp.concatenate(parts, axis=0)
        st_ref[...] = st
        o_all = outs[0] if gpi == 1 else jnp.concatenate(outs, axis=0)
        acc_ref[pl.ds(r0, rows_it), :] = (prev + o_all) if reverse else o_all
        return carry

    if n_it == 1:
        body(0, 0)
    else:
        lax.fori_loop(0, n_it, body, 0)


def _gla_merge(acc_ref, og_ref, gain, o_ref, t):
    tm = min(512, t)
    for r in range(t // tm):
        o = acc_ref[r * tm:(r + 1) * tm, :]
        ms = jnp.mean(o * o, axis=-1, keepdims=True)
        y = (o * lax.rsqrt(ms + EPS) * gain) * _silu(og_ref[0, r * tm:(r + 1) * tm, :])
        o_ref[0, r * tm:(r + 1) * tm, :] = y.astype(BF16)


def _gla_kernel(*refs, n, l, dk, ctx_out):
    q_ref, k_ref, v_ref, gf_ref, gb_ref, og_ref, cq_ref, ck_ref, cv_ref, cgf_ref, cgb_ref, cog_ref, on_ref = refs[:13]
    if ctx_out:
        o_ref, co_ref, acc_ref, cacc_ref, st_ref = refs[13:]
    else:
        o_ref, acc_ref, cacc_ref, st_ref = refs[13:]
    scale = dk ** -0.5
    for reverse, g_ref, cg_ref in ((False, gf_ref, cgf_ref), (True, gb_ref, cgb_ref)):
        st_ref[...] = jnp.zeros_like(st_ref)
        _gla_scan(cq_ref, ck_ref, cv_ref, cg_ref, cacc_ref, st_ref, n_rows=l, reverse=reverse, scale=scale)
        _gla_scan(q_ref, k_ref, v_ref, g_ref, acc_ref, st_ref, n_rows=n, reverse=reverse, scale=scale)
    gain = on_ref[...]
    _gla_merge(acc_ref, og_ref, gain, o_ref, n)
    if ctx_out:
        _gla_merge(cacc_ref, cog_ref, gain, co_ref, l)


def _gla(lat, ctx, onorm, ctx_out):
    q, k, v, og, gf, gb = lat
    cq, ck, cv, cog, cgf, cgb = ctx
    b, n, dkt = q.shape
    l = cq.shape[1]
    dvt = v.shape[2]
    dk = dkt // GLA_HEADS
    dv = dvt // GLA_HEADS

    def spec(t, w):
        return pl.BlockSpec((1, t, w), lambda i, h: (i, 0, h))

    in_specs = [spec(n, dk), spec(n, dk), spec(n, dv), spec(n, dk), spec(n, dk), spec(n, dv),
                spec(l, dk), spec(l, dk), spec(l, dv), spec(l, dk), spec(l, dk), spec(l, dv),
                pl.BlockSpec((1, dv), lambda i, h: (0, 0))]
    out_specs = [spec(n, dv)]
    out_shape = [jax.ShapeDtypeStruct((b, n, dvt), BF16)]
    if ctx_out:
        out_specs.append(spec(l, dv))
        out_shape.append(jax.ShapeDtypeStruct((b, l, dvt), BF16))
    res = pl.pallas_call(
        functools.partial(_gla_kernel, n=n, l=l, dk=dk, ctx_out=ctx_out),
        grid=(b, GLA_HEADS),
        in_specs=in_specs,
        out_specs=out_specs,
        out_shape=out_shape,
        scratch_shapes=[pltpu.VMEM((n, dv), F32), pltpu.VMEM((l, dv), F32), pltpu.VMEM((dv, dk), F32)],
        compiler_params=_cparams(("parallel", "parallel")),
        name="gla",
    )(q, k, v, gf, gb, og, cq, ck, cv, cgf, cgb, cog, onorm.reshape(1, dv))
    return (res[0], res[1]) if ctx_out else (res[0], None)


def _rope_tables(n):
    half = HEAD_DIM // 2
    t = jnp.arange(n, dtype=I32)
    inv = ROPE_THETA ** (-jnp.arange(0, half, 2, dtype=F32) / half)

    def tab(pos):
        ang = pos.astype(F32)[:, None] * inv[None, :]
        ang = jnp.concatenate([ang, ang], axis=-1)
        return jnp.cos(ang), jnp.sin(ang)

    cos_r, sin_r = tab(t // GRID_W)
    cos_c, sin_c = tab(t % GRID_W)
    cos = jnp.concatenate([cos_r, cos_c], axis=-1)
    sin = jnp.concatenate([sin_r, sin_c], axis=-1)
    sign = jnp.where((jnp.arange(HEAD_DIM) % half) < half // 2, -1.0, 1.0).astype(F32)
    reps = LANES // HEAD_DIM
    return jnp.tile(cos, (1, reps)), jnp.tile(sin * sign[None, :], (1, reps))


def _block_ones():
    i = np.arange(LANES)
    return jnp.asarray((i[:, None] // HEAD_DIM) == (i[None, :] // HEAD_DIM), BF16)


def _attn_gains(qa_g, ka_g, qb_g, kb_g):
    qs = HEAD_DIM ** -0.5
    one_a = jnp.ones((A_KV_HEADS * HEAD_DIM,), F32)
    one_b = jnp.ones((B_HEADS * HEAD_DIM,), F32)
    return jnp.concatenate([jnp.tile(qa_g, A_HEADS) * qs, jnp.tile(ka_g, A_KV_HEADS), one_a,
                            jnp.tile(qb_g, B_HEADS) * qs, jnp.tile(kb_g, B_HEADS), one_b]).reshape(1, -1)


def kernel(x, c, ctx, c_ctx, ada_w, ada_b, norm1_g, norm2_g, attn_w_in, attn_w_out, a_q_norm, a_k_norm, b_q_norm,
           b_k_norm, na_rpb, gla_w_in, gla_gk_w_f, gla_gk_b_f, gla_gk_w_b, gla_gk_b_b, gla_o_norm, gla_w_out,
           moe_router, moe_w_gate, moe_w_up, moe_w_down):
    b, n, d = x.shape
    depth = ada_w.shape[0]
    dk = gla_gk_w_f.shape[2]
    dv = gla_w_out.shape[1]
    rows = -(-(b + 1) // 8) * 8
    cvec = jnp.zeros((rows, d), F32).at[:b].set(c).at[b].set(c_ctx)
    mods = _mods(cvec, ada_w, ada_b)
    cos, sin = _rope_tables(n)
    bd = _block_ones()
    h, hc = x, ctx
    for layer in range(depth):
        last = layer == depth - 1
        i = layer // 2
        ml = mods[layer, :b].reshape(b, 6, d)
        mc = mods[layer, b:b + 1].reshape(1, 6, d)
        if layer % 2 == 0:
            w_in = attn_w_in[i].astype(BF16)
            gains = _attn_gains(a_q_norm[i], a_k_norm[i], b_q_norm[i], b_k_norm[i])
            qa, ka, va, qb, kb, vb = _attn_proj(h, ml, norm1_g[layer], w_in, gains, bd, cos, sin)
            cqa, cka, cva, cqb, ckb, cvb = _attn_proj(hc, mc, norm1_g[layer], w_in, gains, bd)
            oa = _gqa(qa, jnp.concatenate([ka, cka], axis=1), jnp.concatenate([va, cva], axis=1), A_HEADS, A_KV_HEADS)
            ob = _na(qb, kb, vb, ckb, cvb, _na_bias_table(na_rpb[i], n // GRID_W))
            o_lat = [oa, ob]
            o_ctx = None if last else [_gqa(cqa, cka, cva, A_HEADS, A_KV_HEADS), _gqa(cqb, ckb, cvb, B_HEADS, B_HEADS)]
            w_out = attn_w_out[i].astype(BF16)
        else:
            w_main = gla_w_in[i][:, :2 * dk + 2 * dv].astype(BF16)
            wz = jnp.pad(gla_w_in[i][:, 2 * dk + 2 * dv:], ((0, 0), (0, LANES - 2 * GLA_GATE_RANK))).astype(BF16)
            wgate = jnp.zeros((LANES, 2 * dk), F32)
            wgate = wgate.at[:GLA_GATE_RANK, :dk].set(gla_gk_w_f[i]).at[GLA_GATE_RANK:2 * GLA_GATE_RANK, dk:].set(gla_gk_w_b[i])
            bgate = jnp.concatenate([gla_gk_b_f[i], gla_gk_b_b[i]]).reshape(1, 2 * dk)
            lat = _gla_proj(h, ml, norm1_g[layer], w_main, wz, wgate.astype(BF16), bgate, dk, dv)
            cx = _gla_proj(hc, mc, norm1_g[layer], w_main, wz, wgate.astype(BF16), bgate, dk, dv)
            o, co = _gla(lat, cx, gla_o_norm[i], not last)
            o_lat = [o]
            o_ctx = None if last else [co]
            w_out = gla_w_out[i].astype(BF16)
        wr_hi, wr_lo = _split2(moe_router[layer].T)
        h1, u2, lg = _outproj(o_lat, h, ml, norm2_g[layer], w_out, wr_hi, wr_lo)
        h = _moe(h1, u2, lg, ml, moe_w_gate, moe_w_up, moe_w_down, layer)
        if not last:
            hc1, uc2, lgc = _outproj(o_ctx, hc, mc, norm2_g[layer], w_out, wr_hi, wr_lo)
            hc = _moe(hc1, uc2, lgc, mc, moe_w_gate, moe_w_up, moe_w_down, layer)
    return h
```

```python
import functools

import numpy as np
import jax
import jax.numpy as jnp
from jax import lax
from jax.experimental import pallas as pl
from jax.experimental.pallas import tpu as pltpu

F32 = jnp.float32
BF16 = jnp.bfloat16
I32 = jnp.int32

GRID_W = 64
HEAD_DIM = 64
A_HEADS = 8
A_KV_HEADS = 2
B_HEADS = 8
NA_ROWS = 8
NA_COLS = 16
ROPE_THETA = 10000.0
GLA_HEADS = 4
GLA_GATE_RANK = 16
GLA_GATE_NORM = 16.0
GLA_CHUNK = 64
GLA_GROUP = 4
N_EXPERTS = 16
EC_CAPACITY = 2
EPS = 1e-6

LANES = 128
BF16_ROWS = 16
NA_QROWS = 4
TOK_BLK = LANES
SMALL_WIN = 48
GATHER_TRIP = 8
MASK_VALUE = -1e30
VMEM_LIMIT = 56 * 1024 * 1024

_NT = (((1,), (1,)), ((), ()))
_TN = (((0,), (0,)), ((), ()))


def _cparams(sem):
    return pltpu.CompilerParams(dimension_semantics=sem, vmem_limit_bytes=VMEM_LIMIT)


def _dot(a, b):
    return jnp.dot(a, b, preferred_element_type=F32)


def _dot_nt(a, b):
    return lax.dot_general(a, b, _NT, preferred_element_type=F32)


def _dot_tn(a, b):
    return lax.dot_general(a, b, _TN, preferred_element_type=F32)


def _split2(a):
    hi = a.astype(BF16)
    lo = (a - hi.astype(F32)).astype(BF16)
    return hi, lo


def _silu(a):
    return a / (1.0 + jnp.exp(-a))


def _rms_mod(x, g, shift, scale):
    ms = jnp.mean(x * x, axis=-1, keepdims=True)
    return (x * lax.rsqrt(ms + EPS) * g) * (1.0 + scale) + shift


def _mods_kernel(c_ref, w_ref, b_ref, o_ref):
    c = c_ref[...]
    s_hi, s_lo = _split2(_silu(c))
    w_hi, w_lo = _split2(w_ref[0])
    o_ref[0] = _dot(s_hi, w_hi) + _dot(s_lo, w_hi) + _dot(s_hi, w_lo) + b_ref[0]


def _mods(cvec, ada_w, ada_b):
    depth, d, d6 = ada_w.shape
    rows = cvec.shape[0]
    tn = 1536 if d6 % 1536 == 0 else d6
    return pl.pallas_call(
        _mods_kernel,
        grid=(depth, d6 // tn),
        in_specs=[pl.BlockSpec((rows, d), lambda l, j: (0, 0)),
                  pl.BlockSpec((1, d, tn), lambda l, j: (l, 0, j)),
                  pl.BlockSpec((1, 1, tn), lambda l, j: (l, 0, j))],
        out_specs=pl.BlockSpec((1, rows, tn), lambda l, j: (l, 0, j)),
        out_shape=jax.ShapeDtypeStruct((depth, rows, d6), F32),
        compiler_params=_cparams(("arbitrary", "arbitrary")),
        name="mods",
    )(cvec, ada_w, ada_b.reshape(depth, 1, d6))


_ATTN_COLS = (A_HEADS + 2 * A_KV_HEADS + 3 * B_HEADS) * HEAD_DIM
_QA = (0, 4)
_KA = (4, 5)
_VA = (5, 6)
_QB = (6, 10)
_KB = (10, 14)
_VB = (14, 18)


def _attn_proj_kernel(*refs, rope):
    if rope:
        x_ref, mod_ref, g1_ref, w_ref, gain_ref, bd_ref, cos_ref, sin_ref = refs[:8]
        outs = refs[8:]
    else:
        x_ref, mod_ref, g1_ref, w_ref, gain_ref, bd_ref = refs[:6]
        outs = refs[6:]
    qa_ref, ka_ref, va_ref, qb_ref, kb_ref, vb_ref = outs
    m = mod_ref[0]
    u = _rms_mod(x_ref[0], g1_ref[...], m[0:1], m[1:2]).astype(BF16)
    acc = _dot(u, w_ref[...])
    bd = bd_ref[...]
    if rope:
        cos = cos_ref[...]
        sin = sin_ref[...]
        lane = lax.broadcasted_iota(I32, cos.shape, 1)
        first = (lane & 31) < 16

    def chunk(j, norm, rot):
        cch = acc[:, j * LANES:(j + 1) * LANES]
        if norm:
            sq_hi, sq_lo = _split2(cch * cch)
            ss = _dot(sq_hi, bd) + _dot(sq_lo, bd)
            cch = cch * lax.rsqrt(ss * (1.0 / HEAD_DIM) + EPS) * gain_ref[:, j * LANES:(j + 1) * LANES]
        if rot:
            partner = jnp.where(first, pltpu.roll(cch, LANES - 16, 1), pltpu.roll(cch, 16, 1))
            cch = cch * cos + partner * sin
        return cch.astype(BF16)

    for (lo, hi), ref, norm, rot in ((_QA, qa_ref, True, rope), (_KA, ka_ref, True, rope), (_VA, va_ref, False, False),
                                     (_QB, qb_ref, True, False), (_KB, kb_ref, True, False), (_VB, vb_ref, False, False)):
        for j in range(lo, hi):
            ref[0, :, (j - lo) * LANES:(j - lo + 1) * LANES] = chunk(j, norm, rot)


def _attn_proj(h, mod, g1, w, gains, bd, cos=None, sin=None):
    b, t, d = h.shape
    tm = min(512, t)
    rope = cos is not None
    nb_mod = mod.shape[0]
    midx = (lambda i, j: (i, 0, 0)) if nb_mod > 1 else (lambda i, j: (0, 0, 0))
    in_specs = [pl.BlockSpec((1, tm, d), lambda i, j: (i, j, 0)),
                pl.BlockSpec((1, 6, d), midx),
                pl.BlockSpec((1, d), lambda i, j: (0, 0)),
                pl.BlockSpec((d, _ATTN_COLS), lambda i, j: (0, 0)),
                pl.BlockSpec((1, _ATTN_COLS), lambda i, j: (0, 0)),
                pl.BlockSpec((LANES, LANES), lambda i, j: (0, 0))]
    args = [h, mod, g1.reshape(1, d), w, gains, bd]
    if rope:
        in_specs += [pl.BlockSpec((tm, LANES), lambda i, j: (j, 0))] * 2
        args += [cos, sin]
    widths = [(hi - lo) * LANES for lo, hi in (_QA, _KA, _VA, _QB, _KB, _VB)]
    return pl.pallas_call(
        functools.partial(_attn_proj_kernel, rope=rope),
        grid=(b, t // tm),
        in_specs=in_specs,
        out_specs=[pl.BlockSpec((1, tm, wd), lambda i, j: (i, j, 0)) for wd in widths],
        out_shape=[jax.ShapeDtypeStruct((b, t, wd), BF16) for wd in widths],
        compiler_params=_cparams(("parallel", "parallel")),
        name="attn_proj_rope" if rope else "attn_proj",
    )(*args)


def _gqa_kernel(q_ref, k_ref, v_ref, o_ref, *, hq, hk):
    grp = hq // hk
    for h in range(hq):
        kv = h // grp
        q = q_ref[0, :, h * HEAD_DIM:(h + 1) * HEAD_DIM]
        k = k_ref[0, :, kv * HEAD_DIM:(kv + 1) * HEAD_DIM]
        v = v_ref[0, :, kv * HEAD_DIM:(kv + 1) * HEAD_DIM]
        s = _dot_nt(q, k)
        p = jnp.exp(s - jnp.max(s, axis=-1, keepdims=True))
        l = jnp.sum(p, axis=-1, keepdims=True)
        o = _dot(p.astype(BF16), v) / l
        o_ref[0, :, h * HEAD_DIM:(h + 1) * HEAD_DIM] = o.astype(BF16)


def _gqa(q, k, v, hq, hk):
    b, t, _ = q.shape
    s = k.shape[1]
    tq = min(256, t)
    return pl.pallas_call(
        functools.partial(_gqa_kernel, hq=hq, hk=hk),
        grid=(b, t // tq),
        in_specs=[pl.BlockSpec((1, tq, hq * HEAD_DIM), lambda i, j: (i, j, 0)),
                  pl.BlockSpec((1, s, hk * HEAD_DIM), lambda i, j: (i, 0, 0)),
                  pl.BlockSpec((1, s, hk * HEAD_DIM), lambda i, j: (i, 0, 0))],
        out_specs=pl.BlockSpec((1, tq, hq * HEAD_DIM), lambda i, j: (i, j, 0)),
        out_shape=jax.ShapeDtypeStruct((b, t, hq * HEAD_DIM), BF16),
        compiler_params=_cparams(("parallel", "parallel")),
        name="gqa",
    )(q, k, v)


def _na_geometry(rows):
    kr = min(NA_QROWS + NA_ROWS - 1, rows)
    nblk = rows // NA_QROWS
    return kr, nblk


def _na_bias_table(rpb, rows):
    kr, nblk = _na_geometry(rows)
    wr = min(NA_ROWS, rows)
    heads = rpb.shape[0]
    qc = np.arange(GRID_W)[:, None]
    kc = np.arange(GRID_W)[None, :]
    cs = np.clip(qc - NA_COLS // 2, 0, GRID_W - NA_COLS)
    col_ok = (kc >= cs) & (kc < cs + NA_COLS)
    pick = (np.arange(2 * NA_COLS - 1)[:, None, None] == (kc - qc + NA_COLS - 1)[None]) & col_ok[None]
    toeplitz = jnp.einsum("hrd,dqk->hrqk", rpb.astype(F32), jnp.asarray(pick, F32), precision=lax.Precision.HIGHEST)
    toeplitz = jnp.where(col_ok[None, None], toeplitz, MASK_VALUE)
    tabs = []
    for rb in (0, min(1, nblk - 1), nblk - 1):
        qr0 = rb * NA_QROWS
        kr0 = int(np.clip(qr0 - NA_ROWS // 2, 0, rows - kr))
        qr = qr0 + np.arange(NA_QROWS)[:, None]
        krr = kr0 + np.arange(kr)[None, :]
        rs = np.clip(qr - wr // 2, 0, rows - wr)
        row_ok = (krr >= rs) & (krr < rs + wr)
        dr = krr - qr + (NA_ROWS - 1)
        masked = jnp.full((heads, GRID_W, GRID_W), MASK_VALUE, F32)
        blocks = [jnp.concatenate([toeplitz[:, dr[i, j]] if row_ok[i, j] else masked for j in range(kr)], axis=2)
                  for i in range(NA_QROWS)]
        tabs.append(jnp.concatenate(blocks, axis=1))
    return jnp.stack(tabs)


def _na_kernel(q_ref, k_ref, v_ref, kc_ref, vc_ref, bias_ref, o_ref, *, rows, heads):
    kr, _ = _na_geometry(rows)
    rb = pl.program_id(1)
    kr0 = jnp.clip(rb * NA_QROWS - NA_ROWS // 2, 0, rows - kr)
    start = pl.multiple_of(kr0 * GRID_W, GRID_W)
    kw = k_ref[0, pl.ds(start, kr * GRID_W), :]
    vw = v_ref[0, pl.ds(start, kr * GRID_W), :]
    for h in range(heads):
        sl = slice(h * HEAD_DIM, (h + 1) * HEAD_DIM)
        q = q_ref[0, :, sl]
        s_w = _dot_nt(q, kw[:, sl]) + bias_ref[0, h]
        s_c = _dot_nt(q, kc_ref[0, :, sl])
        m = jnp.maximum(jnp.max(s_w, axis=-1, keepdims=True), jnp.max(s_c, axis=-1, keepdims=True))
        p_w = jnp.exp(s_w - m)
        p_c = jnp.exp(s_c - m)
        l = jnp.sum(p_w, axis=-1, keepdims=True) + jnp.sum(p_c, axis=-1, keepdims=True)
        o = (_dot(p_w.astype(BF16), vw[:, sl]) + _dot(p_c.astype(BF16), vc_ref[0, :, sl])) / l
        o_ref[0, :, sl] = o.astype(BF16)


def _na(q, k, v, kc, vc, bias):
    b, n, c = q.shape
    l = kc.shape[1]
    rows = n // GRID_W
    kr, nblk = _na_geometry(rows)
    tq = NA_QROWS * GRID_W
    heads = c // HEAD_DIM

    def pattern(i, j):
        return (jnp.where(j == 0, 0, jnp.where(j == nblk - 1, 2, 1)), 0, 0, 0)

    return pl.pallas_call(
        functools.partial(_na_kernel, rows=rows, heads=heads),
        grid=(b, nblk),
        in_specs=[pl.BlockSpec((1, tq, c), lambda i, j: (i, j, 0)),
                  pl.BlockSpec((1, n, c), lambda i, j: (i, 0, 0)),
                  pl.BlockSpec((1, n, c), lambda i, j: (i, 0, 0)),
                  pl.BlockSpec((1, l, c), lambda i, j: (i, 0, 0)),
                  pl.BlockSpec((1, l, c), lambda i, j: (i, 0, 0)),
                  pl.BlockSpec((1, heads, tq, kr * GRID_W), pattern)],
        out_specs=pl.BlockSpec((1, tq, c), lambda i, j: (i, j, 0)),
        out_shape=jax.ShapeDtypeStruct((b, n, c), BF16),
        compiler_params=_cparams(("parallel", "arbitrary")),
        name="na_attn",
    )(q, k, v, kc, vc, bias)


def _outproj_kernel(*refs, n_o, route):
    o_refs = refs[:n_o]
    h_ref, mod_ref, g2_ref, w_ref = refs[n_o:n_o + 4]
    rest = refs[n_o + 4:]
    if route:
        wr_hi_ref, wr_lo_ref, hn_ref, u2_ref, lg_ref = rest
    else:
        hn_ref, = rest
    m = mod_ref[0]
    o = o_refs[0][0] if n_o == 1 else jnp.concatenate([r[0] for r in o_refs], axis=1)
    hn = h_ref[0] + m[2:3] * _dot(o, w_ref[...])
    hn_ref[0] = hn
    if route:
        u2 = _rms_mod(hn, g2_ref[...], m[3:4], m[4:5])
        u_hi, u_lo = _split2(u2)
        u2_ref[0] = u_hi
        wr_hi = wr_hi_ref[...]
        lg_ref[0] = _dot_nt(wr_hi, u_hi) + _dot_nt(wr_hi, u_lo) + _dot_nt(wr_lo_ref[...], u_hi)


def _outproj(o_list, h, mod, g2, w, wr_hi=None, wr_lo=None):
    b, t, d = h.shape
    tm = min(512, t)
    route = wr_hi is not None
    nb_mod = mod.shape[0]
    midx = (lambda i, j: (i, 0, 0)) if nb_mod > 1 else (lambda i, j: (0, 0, 0))
    in_specs = [pl.BlockSpec((1, tm, o.shape[2]), lambda i, j: (i, j, 0)) for o in o_list]
    in_specs += [pl.BlockSpec((1, tm, d), lambda i, j: (i, j, 0)),
                 pl.BlockSpec((1, 6, d), midx),
                 pl.BlockSpec((1, d), lambda i, j: (0, 0)),
                 pl.BlockSpec((d, d), lambda i, j: (0, 0))]
    args = list(o_list) + [h, mod, g2.reshape(1, d), w]
    out_specs = [pl.BlockSpec((1, tm, d), lambda i, j: (i, j, 0))]
    out_shape = [jax.ShapeDtypeStruct((b, t, d), F32)]
    if route:
        in_specs += [pl.BlockSpec((N_EXPERTS, d), lambda i, j: (0, 0))] * 2
        args += [wr_hi, wr_lo]
        out_specs += [pl.BlockSpec((1, tm, d), lambda i, j: (i, j, 0)),
                      pl.BlockSpec((1, N_EXPERTS, tm), lambda i, j: (i, 0, j))]
        out_shape += [jax.ShapeDtypeStruct((b, t, d), BF16), jax.ShapeDtypeStruct((b, N_EXPERTS, t), F32)]
    return pl.pallas_call(
        functools.partial(_outproj_kernel, n_o=len(o_list), route=route),
        grid=(b, t // tm),
        in_specs=in_specs,
        out_specs=out_specs,
        out_shape=out_shape,
        compiler_params=_cparams(("parallel", "parallel")),
        name="outproj",
    )(*args)


def _cumsum_excl(x):
    e, n = x.shape
    nb = n // LANES
    st = jnp.concatenate([x[:, j * LANES:(j + 1) * LANES] for j in range(nb)], axis=0)
    ii = lax.broadcasted_iota(I32, (LANES, LANES), 0)
    jj = lax.broadcasted_iota(I32, (LANES, LANES), 1)
    incl = _dot(st.astype(BF16), jnp.where(ii <= jj, 1.0, 0.0).astype(BF16))
    tot = jnp.broadcast_to(incl[:, LANES - 1:LANES], incl.shape)
    r = lax.broadcasted_iota(I32, (nb * e, nb * e), 0)
    c = lax.broadcasted_iota(I32, (nb * e, nb * e), 1)
    sh = e.bit_length() - 1
    same = (r & (e - 1)) == (c & (e - 1))
    prev = jnp.where(same & (lax.shift_right_logical(c, sh) < lax.shift_right_logical(r, sh)), 1.0, 0.0).astype(BF16)
    starts = _dot(prev, tot.astype(BF16))
    excl_st = incl - st + starts
    excl = jnp.concatenate([excl_st[j * e:(j + 1) * e, :] for j in range(nb)], axis=1)
    return excl, starts


def _route_kernel(lg_ref, dest_ref, gate_ref, offs_ref, *, cap):
    x = lg_ref[0]
    ex = jnp.exp(x - jnp.max(x, axis=0, keepdims=True))
    aff = ex / jnp.sum(ex, axis=0, keepdims=True)
    bits = lax.bitcast_convert_type(aff, I32)

    def search(i, cur):
        cand = cur | lax.shift_left(jnp.int32(1), 30 - i)
        cnt = jnp.sum(jnp.where(bits >= cand, 1.0, 0.0), axis=1, keepdims=True)
        return jnp.where(cnt >= cap, cand, cur)

    thr = lax.fori_loop(0, 31, search, jnp.zeros((x.shape[0], 1), I32))
    gt = bits > thr
    eq = bits == thr
    need = cap - jnp.sum(jnp.where(gt, 1.0, 0.0), axis=1, keepdims=True)
    rank_eq, _ = _cumsum_excl(jnp.where(eq, 1.0, 0.0))
    sel = gt | (eq & (rank_eq < need))
    pos, starts = _cumsum_excl(jnp.where(sel, 1.0, 0.0))
    dest_ref[0] = jnp.where(sel, pos.astype(I32), -1)
    gate_ref[0] = jnp.where(sel, aff, 0.0)
    offs_ref[0] = starts.astype(I32)


def _route(lg, cap):
    b, e, n = lg.shape
    nb = n // LANES
    return pl.pallas_call(
        functools.partial(_route_kernel, cap=cap),
        grid=(b,),
        in_specs=[pl.BlockSpec((1, e, n), lambda i: (i, 0, 0))],
        out_specs=[pl.BlockSpec((1, e, n), lambda i: (i, 0, 0)),
                   pl.BlockSpec((1, e, n), lambda i: (i, 0, 0)),
                   pl.BlockSpec((1, nb * e, LANES), lambda i: (i, 0, 0))],
        out_shape=[jax.ShapeDtypeStruct((b, e, n), I32), jax.ShapeDtypeStruct((b, e, n), F32),
                   jax.ShapeDtypeStruct((b, nb * e, LANES), I32)],
        compiler_params=_cparams(("parallel",)),
        name="route",
    )(lg)


def _blk_range(offs_ref, b, j, e, nblk, cap):
    base = (b * nblk + j) * N_EXPERTS + e
    last = (b * nblk + nblk - 1) * N_EXPERTS + e
    return offs_ref[base], jnp.where(j + 1 < nblk, offs_ref[jnp.minimum(base + N_EXPERTS, last)], cap)


def _win_start(off, cap, win):
    return pl.multiple_of(jnp.minimum(lax.shift_right_logical(off, 4) * BF16_ROWS, cap - win), BF16_ROWS)


def _onehot(dest_row, off_al, win):
    rows = lax.broadcasted_iota(I32, (win, TOK_BLK), 0) + off_al
    return jnp.where(rows == dest_row, 1.0, 0.0)


def _moe_ffn_kernel(offs_ref, u_ref, dest_ref, gate_ref, wg_ref, wu_ref, wd_ref, y_ref,
                    xg_ref, gb_ref, wgb_ref, wub_ref, wdb_ref, *, cap, nblk, win, small, nbatch, batch_ffn):
    e = pl.program_id(0)
    b = pl.program_id(1)

    @pl.when(b == 0)
    def _():
        wgb_ref[...] = wg_ref[0, 0].astype(BF16)
        wub_ref[...] = wu_ref[0, 0].astype(BF16)
        wdb_ref[...] = wd_ref[0, 0].astype(BF16)

    row0 = pl.multiple_of(b * cap, BF16_ROWS) if batch_ffn else 0
    xg_ref[pl.ds(row0, cap), :] = jnp.zeros((cap, xg_ref.shape[1]), F32)
    gb_ref[pl.ds(row0, cap), :] = jnp.zeros((cap, LANES), F32)

    trip = min(GATHER_TRIP, nblk)

    def gather(t, carry):
        js = [t * trip + i for i in range(trip)]
        ranges = [_blk_range(offs_ref, b, j, e, nblk, cap) for j in js]

        def run(w):
            for j, (off, _) in zip(js, ranges):
                st = _win_start(off, cap, w)
                g = _onehot(dest_ref[0, 0, pl.ds(j, 1), :], st, w)
                uj = u_ref[0, pl.ds(pl.multiple_of(j * TOK_BLK, TOK_BLK), TOK_BLK), :]
                r = pl.multiple_of(row0 + st, BF16_ROWS)
                xg_ref[pl.ds(r, w), :] += _dot(g.astype(BF16), uj)
                gsum = jnp.sum(g * gate_ref[0, 0, pl.ds(j, 1), :], axis=1, keepdims=True)
                gb_ref[pl.ds(r, w), :] += jnp.broadcast_to(gsum, (w, LANES))

        if small == win:
            run(win)
        else:
            fits = ranges[0][1] <= _win_start(ranges[0][0], cap, small) + small
            for off, end in ranges[1:]:
                fits = fits & (end <= _win_start(off, cap, small) + small)
            pl.when(fits)(functools.partial(run, small))
            pl.when(jnp.logical_not(fits))(functools.partial(run, win))
        return carry

    lax.fori_loop(0, nblk // trip, gather, 0)

    def ffn(r0, m):
        x = xg_ref[pl.ds(r0, m), :].astype(BF16)
        ff = wgb_ref.shape[1]
        fc = min(512, ff)
        acc = jnp.zeros((m, wdb_ref.shape[1]), F32)
        for c in range(ff // fc):
            a = _dot(x, wgb_ref[:, c * fc:(c + 1) * fc])
            up = _dot(x, wub_ref[:, c * fc:(c + 1) * fc])
            acc = acc + _dot((_silu(a) * up).astype(BF16), wdb_ref[c * fc:(c + 1) * fc, :])
        return (acc * gb_ref[pl.ds(r0, m), 0:1]).astype(BF16)

    if batch_ffn:
        @pl.when(b == nbatch - 1)
        def _():
            y = ffn(0, nbatch * cap)
            for bb in range(nbatch):
                y_ref[bb, 0] = y[bb * cap:(bb + 1) * cap]
    else:
        y_ref[0, 0] = ffn(0, cap)


def _moe_ffn(offs, u, dest4, gate4, wg, wu, wd, layer, cap):
    b, n, d = u.shape
    nblk = n // TOK_BLK
    win = min(TOK_BLK + BF16_ROWS, cap)
    ff = wg.shape[3]
    batch_ffn = cap < 256
    rows = b * cap if batch_ffn else cap
    if batch_ffn:
        y_spec = pl.BlockSpec((b, 1, cap, d), lambda e, i, o: (0, e, 0, 0))
    else:
        y_spec = pl.BlockSpec((1, 1, cap, d), lambda e, i, o: (i, e, 0, 0))
    return pl.pallas_call(
        functools.partial(_moe_ffn_kernel, cap=cap, nblk=nblk, win=win, small=min(SMALL_WIN, cap), nbatch=b,
                          batch_ffn=batch_ffn),
        grid_spec=pltpu.PrefetchScalarGridSpec(
            num_scalar_prefetch=1,
            grid=(N_EXPERTS, b),
            in_specs=[pl.BlockSpec((1, n, d), lambda e, i, o: (i, 0, 0)),
                      pl.BlockSpec((1, 1, nblk, TOK_BLK), lambda e, i, o: (i, e, 0, 0)),
                      pl.BlockSpec((1, 1, nblk, TOK_BLK), lambda e, i, o: (i, e, 0, 0)),
                      pl.BlockSpec((1, 1, d, ff), lambda e, i, o: (layer, e, 0, 0)),
                      pl.BlockSpec((1, 1, d, ff), lambda e, i, o: (layer, e, 0, 0)),
                      pl.BlockSpec((1, 1, ff, d), lambda e, i, o: (layer, e, 0, 0))],
            out_specs=y_spec,
            scratch_shapes=[pltpu.VMEM((rows, d), F32), pltpu.VMEM((rows, LANES), F32),
                            pltpu.VMEM((d, ff), BF16), pltpu.VMEM((d, ff), BF16), pltpu.VMEM((ff, d), BF16)]),
        out_shape=jax.ShapeDtypeStruct((b, N_EXPERTS, cap, d), BF16),
        compiler_params=_cparams(("arbitrary", "arbitrary")),
        name="moe_ffn",
    )(offs, u, dest4, gate4, wg, wu, wd)


def _moe_combine_kernel(offs_ref, y_ref, dest_ref, h_ref, mod_ref, out_ref, *, cap, nblk, win, small, nsub):
    b = pl.program_id(0)
    jb = pl.program_id(1)
    gate = mod_ref[0][5:6]
    for s in range(nsub):
        rows = slice(s * TOK_BLK, (s + 1) * TOK_BLK)
        ranges = [_blk_range(offs_ref, b, jb * nsub + s, e, nblk, cap) for e in range(N_EXPERTS)]

        def stacked(s=s, rows=rows, ranges=ranges):
            starts = [_win_start(off, cap, small) for off, _ in ranges]
            ycat = jnp.concatenate([y_ref[0, e, pl.ds(starts[e], small), :] for e in range(N_EXPERTS)], axis=0)
            gcat = jnp.concatenate([_onehot(dest_ref[0, e, s:s + 1, :], starts[e], small) for e in range(N_EXPERTS)], axis=0)
            out_ref[0, rows, :] = h_ref[0, rows, :] + gate * _dot_tn(gcat.astype(BF16), ycat)

        def per_expert(s=s, rows=rows, ranges=ranges):
            acc = jnp.zeros((TOK_BLK, out_ref.shape[2]), F32)
            for e in range(N_EXPERTS):
                st = _win_start(ranges[e][0], cap, win)
                g = _onehot(dest_ref[0, e, s:s + 1, :], st, win).astype(BF16)
                acc = acc + _dot_tn(g, y_ref[0, e, pl.ds(st, win), :])
            out_ref[0, rows, :] = h_ref[0, rows, :] + gate * acc

        if small == win:
            stacked()
        else:
            fits = ranges[0][1] <= _win_start(ranges[0][0], cap, small) + small
            for off, end in ranges[1:]:
                fits = fits & (end <= _win_start(off, cap, small) + small)
            pl.when(fits)(stacked)
            pl.when(jnp.logical_not(fits))(per_expert)


def _moe_combine(offs, y, dest4, h, mod, cap):
    b, n, d = h.shape
    nblk = n // TOK_BLK
    win = min(TOK_BLK + BF16_ROWS, cap)
    small = min(SMALL_WIN, cap)
    nsub = min(8, nblk)
    tb = nsub * TOK_BLK
    nb_mod = mod.shape[0]
    midx = (lambda i, j, o: (i, 0, 0)) if nb_mod > 1 else (lambda i, j, o: (0, 0, 0))
    return pl.pallas_call(
        functools.partial(_moe_combine_kernel, cap=cap, nblk=nblk, win=win, small=small, nsub=nsub),
        grid_spec=pltpu.PrefetchScalarGridSpec(
            num_scalar_prefetch=1,
            grid=(b, n // tb),
            in_specs=[pl.BlockSpec((1, N_EXPERTS, cap, d), lambda i, j, o: (i, 0, 0, 0), pipeline_mode=pl.Buffered(1)),
                      pl.BlockSpec((1, N_EXPERTS, nsub, TOK_BLK), lambda i, j, o: (i, 0, j, 0)),
                      pl.BlockSpec((1, tb, d), lambda i, j, o: (i, j, 0)),
                      pl.BlockSpec((1, 6, d), midx)],
            out_specs=pl.BlockSpec((1, tb, d), lambda i, j, o: (i, j, 0))),
        out_shape=jax.ShapeDtypeStruct((b, n, d), F32),
        compiler_params=_cparams(("arbitrary", "arbitrary")),
        name="moe_combine",
    )(offs, y, dest4, h, mod)


def _moe(h, u2, lg, mod, wg, wu, wd, layer):
    b, n, d = h.shape
    cap = EC_CAPACITY * n // N_EXPERTS
    nblk = n // TOK_BLK
    dest, gate, starts = _route(lg, cap)
    offs = starts[:, :, 0].reshape(-1)
    dest4 = dest.reshape(b, N_EXPERTS, nblk, TOK_BLK)
    gate4 = gate.reshape(b, N_EXPERTS, nblk, TOK_BLK)
    y = _moe_ffn(offs, u2, dest4, gate4, wg, wu, wd, layer, cap)
    return _moe_combine(offs, y, dest4, h, mod, cap)


def _gla_proj_kernel(x_ref, mod_ref, g1_ref, w_ref, wz_ref, wgate_ref, bgate_ref,
                     q_ref, k_ref, v_ref, og_ref, gf_ref, gb_ref, *, dk, dv):
    m = mod_ref[0]
    u = _rms_mod(x_ref[0], g1_ref[...], m[0:1], m[1:2]).astype(BF16)
    acc = _dot(u, w_ref[...])
    z = _dot(u, wz_ref[...]).astype(BF16)
    gp = _dot(z, wgate_ref[...]) + bgate_ref[...]
    g = (jnp.minimum(gp, 0.0) - jnp.log1p(jnp.exp(-jnp.abs(gp)))) * (1.0 / GLA_GATE_NORM)
    q_ref[0] = acc[:, :dk]
    k_ref[0] = acc[:, dk:2 * dk]
    v_ref[0] = acc[:, 2 * dk:2 * dk + dv].astype(BF16)
    og_ref[0] = acc[:, 2 * dk + dv:]
    gf_ref[0] = g[:, :dk]
    gb_ref[0] = g[:, dk:]


def _gla_proj(h, mod, g1, w, wz, wgate, bgate, dk, dv):
    b, t, d = h.shape
    tm = min(512, t)
    nb_mod = mod.shape[0]
    midx = (lambda i, j: (i, 0, 0)) if nb_mod > 1 else (lambda i, j: (0, 0, 0))
    cols = 2 * dk + 2 * dv
    widths = [(dk, F32), (dk, F32), (dv, BF16), (dv, F32), (dk, F32), (dk, F32)]
    return pl.pallas_call(
        functools.partial(_gla_proj_kernel, dk=dk, dv=dv),
        grid=(b, t // tm),
        in_specs=[pl.BlockSpec((1, tm, d), lambda i, j: (i, j, 0)),
                  pl.BlockSpec((1, 6, d), midx),
                  pl.BlockSpec((1, d), lambda i, j: (0, 0)),
                  pl.BlockSpec((d, cols), lambda i, j: (0, 0)),
                  pl.BlockSpec((d, LANES), lambda i, j: (0, 0)),
                  pl.BlockSpec((LANES, 2 * dk), lambda i, j: (0, 0)),
                  pl.BlockSpec((1, 2 * dk), lambda i, j: (0, 0))],
        out_specs=[pl.BlockSpec((1, tm, wd), lambda i, j: (i, j, 0)) for wd, _ in widths],
        out_shape=[jax.ShapeDtypeStruct((b, t, wd), dt) for wd, dt in widths],
        compiler_params=_cparams(("parallel", "parallel")),
        name="gla_proj",
    )(h, mod, g1.reshape(1, d), w, wz, wgate, bgate)


def _gla_scan(q_ref, k_ref, v_ref, g_ref, acc_ref, st_ref, *, n_rows, reverse, scale):
    c = GLA_CHUNK
    gr = min(GLA_GROUP * c, n_rows)
    cpg = gr // c
    gpi = min(2, n_rows // gr)
    rows_it = gpi * gr
    n_it = n_rows // rows_it
    ii = lax.broadcasted_iota(I32, (gr, gr), 0)
    jj = lax.broadcasted_iota(I32, (gr, gr), 1)
    sh = c.bit_length() - 1
    same = (ii >> sh) == (jj >> sh)
    tri = same & ((jj >= ii) if reverse else (jj <= ii))
    tri_b = jnp.where(tri, 1.0, 0.0).astype(BF16)
    g_order = range(gpi - 1, -1, -1) if reverse else range(gpi)
    c_order = range(cpg - 1, -1, -1) if reverse else range(cpg)

    def body(i, carry):
        r0 = 0 if n_it == 1 else pl.multiple_of(((n_it - 1 - i) if reverse else i) * rows_it, rows_it)
        q_all = q_ref[0, pl.ds(r0, rows_it), :]
        k_all = k_ref[0, pl.ds(r0, rows_it), :]
        v_all = v_ref[0, pl.ds(r0, rows_it), :]
        g_all = g_ref[0, pl.ds(r0, rows_it), :]
        prev = acc_ref[pl.ds(r0, rows_it), :] if reverse else None
        st = st_ref[...]
        outs = [None] * gpi
        for gi in g_order:
            rs = slice(gi * gr, (gi + 1) * gr)
            k = k_all[rs]
            v = v_all[rs]
            g_hi, g_lo = _split2(g_all[rs])
            cum = _dot(tri_b, g_hi) + _dot(tri_b, g_lo)
            cl = [cum[ci * c:ci * c + 1] if reverse else cum[(ci + 1) * c - 1:(ci + 1) * c] for ci in range(cpg)]
            cl_rows = jnp.concatenate([jnp.broadcast_to(x, (c, x.shape[1])) for x in cl], axis=0)
            q_dec = (q_all[rs] * scale * jnp.exp(cum)).astype(BF16)
            k_inv = (k * jnp.exp(-cum)).astype(BF16)
            k_st = (k * jnp.exp(cl_rows - cum)).astype(BF16)
            a = jnp.where(tri, _dot_nt(q_dec, k_inv), 0.0)
            o_intra = _dot(a.astype(BF16), v)
            parts = [None] * cpg
            for ci in c_order:
                cs = slice(ci * c, (ci + 1) * c)
                parts[ci] = o_intra[cs] + _dot_nt(q_dec[cs], st.astype(BF16))
                st = st * jnp.exp(cl[ci]) + _dot_tn(v[cs], k_st[cs])
            outs[gi] = jnp.concatenate(parts, axis=0)
        st_ref[...] = st
        o_all = outs[0] if gpi == 1 else jnp.concatenate(outs, axis=0)
        acc_ref[pl.ds(r0, rows_it), :] = (prev + o_all) if reverse else o_all
        return carry

    if n_it == 1:
        body(0, 0)
    else:
        lax.fori_loop(0, n_it, body, 0)


def _gla_merge(acc_ref, og_ref, gain, o_ref, t):
    tm = min(512, t)
    for r in range(t // tm):
        o = acc_ref[r * tm:(r + 1) * tm, :]
        ms = jnp.mean(o * o, axis=-1, keepdims=True)
        y = (o * lax.rsqrt(ms + EPS) * gain) * _silu(og_ref[0, r * tm:(r + 1) * tm, :])
        o_ref[0, r * tm:(r + 1) * tm, :] = y.astype(BF16)


def _gla_kernel(*refs, n, l, dk, ctx_out):
    q_ref, k_ref, v_ref, gf_ref, gb_ref, og_ref, cq_ref, ck_ref, cv_ref, cgf_ref, cgb_ref, cog_ref, on_ref = refs[:13]
    if ctx_out:
        o_ref, co_ref, acc_ref, cacc_ref, st_ref = refs[13:]
    else:
        o_ref, acc_ref, cacc_ref, st_ref = refs[13:]
    scale = dk ** -0.5
    for reverse, g_ref, cg_ref in ((False, gf_ref, cgf_ref), (True, gb_ref, cgb_ref)):
        st_ref[...] = jnp.zeros_like(st_ref)
        _gla_scan(cq_ref, ck_ref, cv_ref, cg_ref, cacc_ref, st_ref, n_rows=l, reverse=reverse, scale=scale)
        _gla_scan(q_ref, k_ref, v_ref, g_ref, acc_ref, st_ref, n_rows=n, reverse=reverse, scale=scale)
    gain = on_ref[...]
    _gla_merge(acc_ref, og_ref, gain, o_ref, n)
    if ctx_out:
        _gla_merge(cacc_ref, cog_ref, gain, co_ref, l)


def _gla(lat, ctx, onorm, ctx_out):
    q, k, v, og, gf, gb = lat
    cq, ck, cv, cog, cgf, cgb = ctx
    b, n, dkt = q.shape
    l = cq.shape[1]
    dvt = v.shape[2]
    dk = dkt // GLA_HEADS
    dv = dvt // GLA_HEADS

    def spec(t, w):
        return pl.BlockSpec((1, t, w), lambda i, h: (i, 0, h))

    in_specs = [spec(n, dk), spec(n, dk), spec(n, dv), spec(n, dk), spec(n, dk), spec(n, dv),
                spec(l, dk), spec(l, dk), spec(l, dv), spec(l, dk), spec(l, dk), spec(l, dv),
                pl.BlockSpec((1, dv), lambda i, h: (0, 0))]
    out_specs = [spec(n, dv)]
    out_shape = [jax.ShapeDtypeStruct((b, n, dvt), BF16)]
    if ctx_out:
        out_specs.append(spec(l, dv))
        out_shape.append(jax.ShapeDtypeStruct((b, l, dvt), BF16))
    res = pl.pallas_call(
        functools.partial(_gla_kernel, n=n, l=l, dk=dk, ctx_out=ctx_out),
        grid=(b, GLA_HEADS),
        in_specs=in_specs,
        out_specs=out_specs,
        out_shape=out_shape,
        scratch_shapes=[pltpu.VMEM((n, dv), F32), pltpu.VMEM((l, dv), F32), pltpu.VMEM((dv, dk), F32)],
        compiler_params=_cparams(("parallel", "parallel")),
        name="gla",
    )(q, k, v, gf, gb, og, cq, ck, cv, cgf, cgb, cog, onorm.reshape(1, dv))
    return (res[0], res[1]) if ctx_out else (res[0], None)


def _rope_tables(n):
    half = HEAD_DIM // 2
    t = jnp.arange(n, dtype=I32)
    inv = ROPE_THETA ** (-jnp.arange(0, half, 2, dtype=F32) / half)

    def tab(pos):
        ang = pos.astype(F32)[:, None] * inv[None, :]
        ang = jnp.concatenate([ang, ang], axis=-1)
        return jnp.cos(ang), jnp.sin(ang)

    cos_r, sin_r = tab(t // GRID_W)
    cos_c, sin_c = tab(t % GRID_W)
    cos = jnp.concatenate([cos_r, cos_c], axis=-1)
    sin = jnp.concatenate([sin_r, sin_c], axis=-1)
    sign = jnp.where((jnp.arange(HEAD_DIM) % half) < half // 2, -1.0, 1.0).astype(F32)
    reps = LANES // HEAD_DIM
    return jnp.tile(cos, (1, reps)), jnp.tile(sin * sign[None, :], (1, reps))


def _block_ones():
    i = np.arange(LANES)
    return jnp.asarray((i[:, None] // HEAD_DIM) == (i[None, :] // HEAD_DIM), BF16)


def _attn_gains(qa_g, ka_g, qb_g, kb_g):
    qs = HEAD_DIM ** -0.5
    one_a = jnp.ones((A_KV_HEADS * HEAD_DIM,), F32)
    one_b = jnp.ones((B_HEADS * HEAD_DIM,), F32)
    return jnp.concatenate([jnp.tile(qa_g, A_HEADS) * qs, jnp.tile(ka_g, A_KV_HEADS), one_a,
                            jnp.tile(qb_g, B_HEADS) * qs, jnp.tile(kb_g, B_HEADS), one_b]).reshape(1, -1)


def kernel(x, c, ctx, c_ctx, ada_w, ada_b, norm1_g, norm2_g, attn_w_in, attn_w_out, a_q_norm, a_k_norm, b_q_norm,
           b_k_norm, na_rpb, gla_w_in, gla_gk_w_f, gla_gk_b_f, gla_gk_w_b, gla_gk_b_b, gla_o_norm, gla_w_out,
           moe_router, moe_w_gate, moe_w_up, moe_w_down):
    b, n, d = x.shape
    depth = ada_w.shape[0]
    dk = gla_gk_w_f.shape[2]
    dv = gla_w_out.shape[1]
    rows = -(-(b + 1) // 8) * 8
    cvec = jnp.zeros((rows, d), F32).at[:b].set(c).at[b].set(c_ctx)
    mods = _mods(cvec, ada_w, ada_b)
    cos, sin = _rope_tables(n)
    bd = _block_ones()
    h, hc = x, ctx
    for layer in range(depth):
        last = layer == depth - 1
        i = layer // 2
        ml = mods[layer, :b].reshape(b, 6, d)
        mc = mods[layer, b:b + 1].reshape(1, 6, d)
        if layer % 2 == 0:
            w_in = attn_w_in[i].astype(BF16)
            gains = _attn_gains(a_q_norm[i], a_k_norm[i], b_q_norm[i], b_k_norm[i])
            qa, ka, va, qb, kb, vb = _attn_proj(h, ml, norm1_g[layer], w_in, gains, bd, cos, sin)
            cqa, cka, cva, cqb, ckb, cvb = _attn_proj(hc, mc, norm1_g[layer], w_in, gains, bd)
            oa = _gqa(qa, jnp.concatenate([ka, cka], axis=1), jnp.concatenate([va, cva], axis=1), A_HEADS, A_KV_HEADS)
            ob = _na(qb, kb, vb, ckb, cvb, _na_bias_table(na_rpb[i], n // GRID_W))
            o_lat = [oa, ob]
            o_ctx = None if last else [_gqa(cqa, cka, cva, A_HEADS, A_KV_HEADS), _gqa(cqb, ckb, cvb, B_HEADS, B_HEADS)]
            w_out = attn_w_out[i].astype(BF16)
        else:
            w_main = gla_w_in[i][:, :2 * dk + 2 * dv].astype(BF16)
            wz = jnp.pad(gla_w_in[i][:, 2 * dk + 2 * dv:], ((0, 0), (0, LANES - 2 * GLA_GATE_RANK))).astype(BF16)
            wgate = jnp.zeros((LANES, 2 * dk), F32)
            wgate = wgate.at[:GLA_GATE_RANK, :dk].set(gla_gk_w_f[i]).at[GLA_GATE_RANK:2 * GLA_GATE_RANK, dk:].set(gla_gk_w_b[i])
            bgate = jnp.concatenate([gla_gk_b_f[i], gla_gk_b_b[i]]).reshape(1, 2 * dk)
            lat = _gla_proj(h, ml, norm1_g[layer], w_main, wz, wgate.astype(BF16), bgate, dk, dv)
            cx = _gla_proj(hc, mc, norm1_g[layer], w_main, wz, wgate.astype(BF16), bgate, dk, dv)
            o, co = _gla(lat, cx, gla_o_norm[i], not last)
            o_lat = [o]
            o_ctx = None if last else [co]
            w_out = gla_w_out[i].astype(BF16)
        wr_hi, wr_lo = _split2(moe_router[layer].T)
        h1, u2, lg = _outproj(o_lat, h, ml, norm2_g[layer], w_out, wr_hi, wr_lo)
        h = _moe(h1, u2, lg, ml, moe_w_gate, moe_w_up, moe_w_down, layer)
        if not last:
            hc1, uc2, lgc = _outproj(o_ctx, hc, mc, norm2_g[layer], w_out, wr_hi, wr_lo)
            hc = _moe(hc1, uc2, lgc, mc, moe_w_gate, moe_w_up, moe_w_down, layer)
    return h
```

```python
import functools

import numpy as np
import jax
import jax.numpy as jnp
from jax import lax
from jax.experimental import pallas as pl
from jax.experimental.pallas import tpu as pltpu

F32 = jnp.float32
BF16 = jnp.bfloat16
I32 = jnp.int32

GRID_W = 64
HEAD_DIM = 64
A_HEADS = 8
A_KV_HEADS = 2
B_HEADS = 8
NA_ROWS = 8
NA_COLS = 16
ROPE_THETA = 10000.0
GLA_HEADS = 4
GLA_GATE_RANK = 16
GLA_GATE_NORM = 16.0
GLA_CHUNK = 64
GLA_GROUP = 4
GLA_GROUPS_PER_TRIP = 4
N_EXPERTS = 16
EC_CAPACITY = 2
EPS = 1e-6

LANES = 128
BF16_ROWS = 16
NA_QROWS = 4
TOK_BLK = LANES
SMALL_WIN = 48
GATHER_TRIP = 8
MASK_VALUE = -1e30
VMEM_LIMIT = 56 * 1024 * 1024

_NT = (((1,), (1,)), ((), ()))
_TN = (((0,), (0,)), ((), ()))


def _cparams(sem):
    return pltpu.CompilerParams(dimension_semantics=sem, vmem_limit_bytes=VMEM_LIMIT)


def _dot(a, b):
    return jnp.dot(a, b, preferred_element_type=F32)


def _dot_nt(a, b):
    return lax.dot_general(a, b, _NT, preferred_element_type=F32)


def _dot_tn(a, b):
    return lax.dot_general(a, b, _TN, preferred_element_type=F32)


def _split2(a):
    hi = a.astype(BF16)
    lo = (a - hi.astype(F32)).astype(BF16)
    return hi, lo


def _silu(a):
    return a / (1.0 + jnp.exp(-a))


def _rms_mod(x, g, shift, scale):
    ms = jnp.mean(x * x, axis=-1, keepdims=True)
    return (x * lax.rsqrt(ms + EPS) * g) * (1.0 + scale) + shift


def _mods_kernel(c_ref, w_ref, b_ref, o_ref):
    c = c_ref[...]
    s_hi, s_lo = _split2(_silu(c))
    w_hi, w_lo = _split2(w_ref[0])
    o_ref[0] = _dot(s_hi, w_hi) + _dot(s_lo, w_hi) + _dot(s_hi, w_lo) + b_ref[0]


def _mods(cvec, ada_w, ada_b):
    depth, d, d6 = ada_w.shape
    rows = cvec.shape[0]
    tn = 1536 if d6 % 1536 == 0 else d6
    return pl.pallas_call(
        _mods_kernel,
        grid=(depth, d6 // tn),
        in_specs=[pl.BlockSpec((rows, d), lambda l, j: (0, 0)),
                  pl.BlockSpec((1, d, tn), lambda l, j: (l, 0, j)),
                  pl.BlockSpec((1, 1, tn), lambda l, j: (l, 0, j))],
        out_specs=pl.BlockSpec((1, rows, tn), lambda l, j: (l, 0, j)),
        out_shape=jax.ShapeDtypeStruct((depth, rows, d6), F32),
        compiler_params=_cparams(("arbitrary", "arbitrary")),
        name="mods",
    )(cvec, ada_w, ada_b.reshape(depth, 1, d6))


_ATTN_COLS = (A_HEADS + 2 * A_KV_HEADS + 3 * B_HEADS) * HEAD_DIM
_QA = (0, 4)
_KA = (4, 5)
_VA = (5, 6)
_QB = (6, 10)
_KB = (10, 14)
_VB = (14, 18)


def _attn_proj_kernel(*refs, rope):
    if rope:
        x_ref, mod_ref, g1_ref, w_ref, gain_ref, bd_ref, cos_ref, sin_ref = refs[:8]
        outs = refs[8:]
    else:
        x_ref, mod_ref, g1_ref, w_ref, gain_ref, bd_ref = refs[:6]
        outs = refs[6:]
    qa_ref, ka_ref, va_ref, qb_ref, kb_ref, vb_ref = outs
    m = mod_ref[0]
    u = _rms_mod(x_ref[0], g1_ref[...], m[0:1], m[1:2]).astype(BF16)
    acc = _dot(u, w_ref[...])
    bd = bd_ref[...]
    if rope:
        cos = cos_ref[...]
        sin = sin_ref[...]
        lane = lax.broadcasted_iota(I32, cos.shape, 1)
        first = (lane & 31) < 16

    def chunk(j, norm, rot):
        cch = acc[:, j * LANES:(j + 1) * LANES]
        if norm:
            sq_hi, sq_lo = _split2(cch * cch)
            ss = _dot(sq_hi, bd) + _dot(sq_lo, bd)
            cch = cch * lax.rsqrt(ss * (1.0 / HEAD_DIM) + EPS) * gain_ref[:, j * LANES:(j + 1) * LANES]
        if rot:
            partner = jnp.where(first, pltpu.roll(cch, LANES - 16, 1), pltpu.roll(cch, 16, 1))
            cch = cch * cos + partner * sin
        return cch.astype(BF16)

    for (lo, hi), ref, norm, rot in ((_QA, qa_ref, True, rope), (_KA, ka_ref, True, rope), (_VA, va_ref, False, False),
                                     (_QB, qb_ref, True, False), (_KB, kb_ref, True, False), (_VB, vb_ref, False, False)):
        for j in range(lo, hi):
            ref[0, :, (j - lo) * LANES:(j - lo + 1) * LANES] = chunk(j, norm, rot)


def _attn_proj(h, mod, g1, w, gains, bd, cos=None, sin=None):
    b, t, d = h.shape
    tm = min(512, t)
    rope = cos is not None
    nb_mod = mod.shape[0]
    midx = (lambda i, j: (i, 0, 0)) if nb_mod > 1 else (lambda i, j: (0, 0, 0))
    in_specs = [pl.BlockSpec((1, tm, d), lambda i, j: (i, j, 0)),
                pl.BlockSpec((1, 6, d), midx),
                pl.BlockSpec((1, d), lambda i, j: (0, 0)),
                pl.BlockSpec((d, _ATTN_COLS), lambda i, j: (0, 0)),
                pl.BlockSpec((1, _ATTN_COLS), lambda i, j: (0, 0)),
                pl.BlockSpec((LANES, LANES), lambda i, j: (0, 0))]
    args = [h, mod, g1.reshape(1, d), w, gains, bd]
    if rope:
        in_specs += [pl.BlockSpec((tm, LANES), lambda i, j: (j, 0))] * 2
        args += [cos, sin]
    widths = [(hi - lo) * LANES for lo, hi in (_QA, _KA, _VA, _QB, _KB, _VB)]
    return pl.pallas_call(
        functools.partial(_attn_proj_kernel, rope=rope),
        grid=(b, t // tm),
        in_specs=in_specs,
        out_specs=[pl.BlockSpec((1, tm, wd), lambda i, j: (i, j, 0)) for wd in widths],
        out_shape=[jax.ShapeDtypeStruct((b, t, wd), BF16) for wd in widths],
        compiler_params=_cparams(("parallel", "parallel")),
        name="attn_proj_rope" if rope else "attn_proj",
    )(*args)


def _gqa_kernel(q_ref, k_ref, v_ref, o_ref, *, hq, hk):
    grp = hq // hk
    for h in range(hq):
        kv = h // grp
        q = q_ref[0, :, h * HEAD_DIM:(h + 1) * HEAD_DIM]
        k = k_ref[0, :, kv * HEAD_DIM:(kv + 1) * HEAD_DIM]
        v = v_ref[0, :, kv * HEAD_DIM:(kv + 1) * HEAD_DIM]
        s = _dot_nt(q, k)
        p = jnp.exp(s - jnp.max(s, axis=-1, keepdims=True))
        l = jnp.sum(p, axis=-1, keepdims=True)
        o = _dot(p.astype(BF16), v) / l
        o_ref[0, :, h * HEAD_DIM:(h + 1) * HEAD_DIM] = o.astype(BF16)


def _gqa(q, k, v, hq, hk):
    b, t, _ = q.shape
    s = k.shape[1]
    tq = min(512, t)
    return pl.pallas_call(
        functools.partial(_gqa_kernel, hq=hq, hk=hk),
        grid=(b, t // tq),
        in_specs=[pl.BlockSpec((1, tq, hq * HEAD_DIM), lambda i, j: (i, j, 0)),
                  pl.BlockSpec((1, s, hk * HEAD_DIM), lambda i, j: (i, 0, 0)),
                  pl.BlockSpec((1, s, hk * HEAD_DIM), lambda i, j: (i, 0, 0))],
        out_specs=pl.BlockSpec((1, tq, hq * HEAD_DIM), lambda i, j: (i, j, 0)),
        out_shape=jax.ShapeDtypeStruct((b, t, hq * HEAD_DIM), BF16),
        compiler_params=_cparams(("parallel", "parallel")),
        name="gqa",
    )(q, k, v)


def _na_geometry(rows):
    kr = min(NA_QROWS + NA_ROWS - 1, rows)
    nblk = rows // NA_QROWS
    return kr, nblk


def _na_bias_table(rpb, rows):
    kr, nblk = _na_geometry(rows)
    wr = min(NA_ROWS, rows)
    heads = rpb.shape[0]
    qc = np.arange(GRID_W)[:, None]
    kc = np.arange(GRID_W)[None, :]
    cs = np.clip(qc - NA_COLS // 2, 0, GRID_W - NA_COLS)
    col_ok = (kc >= cs) & (kc < cs + NA_COLS)
    pick = (np.arange(2 * NA_COLS - 1)[:, None, None] == (kc - qc + NA_COLS - 1)[None]) & col_ok[None]
    toeplitz = jnp.einsum("hrd,dqk->hrqk", rpb.astype(F32), jnp.asarray(pick, F32), precision=lax.Precision.HIGHEST)
    toeplitz = jnp.where(col_ok[None, None], toeplitz, MASK_VALUE)
    tabs = []
    for rb in (0, min(1, nblk - 1), nblk - 1):
        qr0 = rb * NA_QROWS
        kr0 = int(np.clip(qr0 - NA_ROWS // 2, 0, rows - kr))
        qr = qr0 + np.arange(NA_QROWS)[:, None]
        krr = kr0 + np.arange(kr)[None, :]
        rs = np.clip(qr - wr // 2, 0, rows - wr)
        row_ok = (krr >= rs) & (krr < rs + wr)
        dr = krr - qr + (NA_ROWS - 1)
        masked = jnp.full((heads, GRID_W, GRID_W), MASK_VALUE, F32)
        blocks = [jnp.concatenate([toeplitz[:, dr[i, j]] if row_ok[i, j] else masked for j in range(kr)], axis=2)
                  for i in range(NA_QROWS)]
        tabs.append(jnp.concatenate(blocks, axis=1))
    return jnp.stack(tabs)


def _na_kernel(q_ref, k_ref, v_ref, kc_ref, vc_ref, bias_ref, o_ref, *, rows, heads):
    kr, _ = _na_geometry(rows)
    rb = pl.program_id(1)
    kr0 = jnp.clip(rb * NA_QROWS - NA_ROWS // 2, 0, rows - kr)
    start = pl.multiple_of(kr0 * GRID_W, GRID_W)
    kw = k_ref[0, pl.ds(start, kr * GRID_W), :]
    vw = v_ref[0, pl.ds(start, kr * GRID_W), :]
    for h in range(heads):
        sl = slice(h * HEAD_DIM, (h + 1) * HEAD_DIM)
        q = q_ref[0, :, sl]
        s_w = _dot_nt(q, kw[:, sl]) + bias_ref[0, h]
        s_c = _dot_nt(q, kc_ref[0, :, sl])
        m = jnp.maximum(jnp.max(s_w, axis=-1, keepdims=True), jnp.max(s_c, axis=-1, keepdims=True))
        p_w = jnp.exp(s_w - m)
        p_c = jnp.exp(s_c - m)
        l = jnp.sum(p_w, axis=-1, keepdims=True) + jnp.sum(p_c, axis=-1, keepdims=True)
        o = (_dot(p_w.astype(BF16), vw[:, sl]) + _dot(p_c.astype(BF16), vc_ref[0, :, sl])) / l
        o_ref[0, :, sl] = o.astype(BF16)


def _na(q, k, v, kc, vc, bias):
    b, n, c = q.shape
    l = kc.shape[1]
    rows = n // GRID_W
    kr, nblk = _na_geometry(rows)
    tq = NA_QROWS * GRID_W
    heads = c // HEAD_DIM

    def pattern(i, j):
        return (jnp.where(j == 0, 0, jnp.where(j == nblk - 1, 2, 1)), 0, 0, 0)

    return pl.pallas_call(
        functools.partial(_na_kernel, rows=rows, heads=heads),
        grid=(b, nblk),
        in_specs=[pl.BlockSpec((1, tq, c), lambda i, j: (i, j, 0)),
                  pl.BlockSpec((1, n, c), lambda i, j: (i, 0, 0)),
                  pl.BlockSpec((1, n, c), lambda i, j: (i, 0, 0)),
                  pl.BlockSpec((1, l, c), lambda i, j: (i, 0, 0)),
                  pl.BlockSpec((1, l, c), lambda i, j: (i, 0, 0)),
                  pl.BlockSpec((1, heads, tq, kr * GRID_W), pattern)],
        out_specs=pl.BlockSpec((1, tq, c), lambda i, j: (i, j, 0)),
        out_shape=jax.ShapeDtypeStruct((b, n, c), BF16),
        compiler_params=_cparams(("parallel", "arbitrary")),
        name="na_attn",
    )(q, k, v, kc, vc, bias)


def _outproj_kernel(*refs, n_o, route):
    o_refs = refs[:n_o]
    h_ref, mod_ref, g2_ref, w_ref = refs[n_o:n_o + 4]
    rest = refs[n_o + 4:]
    if route:
        wr_hi_ref, wr_lo_ref, hn_ref, u2_ref, lg_ref = rest
    else:
        hn_ref, = rest
    m = mod_ref[0]
    o = o_refs[0][0] if n_o == 1 else jnp.concatenate([r[0] for r in o_refs], axis=1)
    hn = h_ref[0] + m[2:3] * _dot(o, w_ref[...])
    hn_ref[0] = hn
    if route:
        u2 = _rms_mod(hn, g2_ref[...], m[3:4], m[4:5])
        u_hi, u_lo = _split2(u2)
        u2_ref[0] = u_hi
        wr_hi = wr_hi_ref[...]
        lg_ref[0] = _dot_nt(wr_hi, u_hi) + _dot_nt(wr_hi, u_lo) + _dot_nt(wr_lo_ref[...], u_hi)


def _outproj(o_list, h, mod, g2, w, wr_hi=None, wr_lo=None):
    b, t, d = h.shape
    tm = min(512, t)
    route = wr_hi is not None
    nb_mod = mod.shape[0]
    midx = (lambda i, j: (i, 0, 0)) if nb_mod > 1 else (lambda i, j: (0, 0, 0))
    in_specs = [pl.BlockSpec((1, tm, o.shape[2]), lambda i, j: (i, j, 0)) for o in o_list]
    in_specs += [pl.BlockSpec((1, tm, d), lambda i, j: (i, j, 0)),
                 pl.BlockSpec((1, 6, d), midx),
                 pl.BlockSpec((1, d), lambda i, j: (0, 0)),
                 pl.BlockSpec((d, d), lambda i, j: (0, 0))]
    args = list(o_list) + [h, mod, g2.reshape(1, d), w]
    out_specs = [pl.BlockSpec((1, tm, d), lambda i, j: (i, j, 0))]
    out_shape = [jax.ShapeDtypeStruct((b, t, d), F32)]
    if route:
        in_specs += [pl.BlockSpec((N_EXPERTS, d), lambda i, j: (0, 0))] * 2
        args += [wr_hi, wr_lo]
        out_specs += [pl.BlockSpec((1, tm, d), lambda i, j: (i, j, 0)),
                      pl.BlockSpec((1, N_EXPERTS, tm), lambda i, j: (i, 0, j))]
        out_shape += [jax.ShapeDtypeStruct((b, t, d), BF16), jax.ShapeDtypeStruct((b, N_EXPERTS, t), F32)]
    return pl.pallas_call(
        functools.partial(_outproj_kernel, n_o=len(o_list), route=route),
        grid=(b, t // tm),
        in_specs=in_specs,
        out_specs=out_specs,
        out_shape=out_shape,
        compiler_params=_cparams(("parallel", "parallel")),
        name="outproj",
    )(*args)


def _cumsum_excl(x):
    e, n = x.shape
    nb = n // LANES
    st = jnp.concatenate([x[:, j * LANES:(j + 1) * LANES] for j in range(nb)], axis=0)
    ii = lax.broadcasted_iota(I32, (LANES, LANES), 0)
    jj = lax.broadcasted_iota(I32, (LANES, LANES), 1)
    incl = _dot(st.astype(BF16), jnp.where(ii <= jj, 1.0, 0.0).astype(BF16))
    tot = jnp.broadcast_to(incl[:, LANES - 1:LANES], incl.shape)
    r = lax.broadcasted_iota(I32, (nb * e, nb * e), 0)
    c = lax.broadcasted_iota(I32, (nb * e, nb * e), 1)
    sh = e.bit_length() - 1
    same = (r & (e - 1)) == (c & (e - 1))
    prev = jnp.where(same & (lax.shift_right_logical(c, sh) < lax.shift_right_logical(r, sh)), 1.0, 0.0).astype(BF16)
    starts = _dot(prev, tot.astype(BF16))
    excl_st = incl - st + starts
    excl = jnp.concatenate([excl_st[j * e:(j + 1) * e, :] for j in range(nb)], axis=1)
    return excl, starts


def _route_kernel(lg_ref, dest_ref, gate_ref, offs_ref, *, cap):
    x = lg_ref[0]
    ex = jnp.exp(x - jnp.max(x, axis=0, keepdims=True))
    aff = ex / jnp.sum(ex, axis=0, keepdims=True)
    bits = lax.bitcast_convert_type(aff, I32)

    def search(i, cur):
        cand = cur | lax.shift_left(jnp.int32(1), 30 - i)
        cnt = jnp.sum(jnp.where(bits >= cand, 1.0, 0.0), axis=1, keepdims=True)
        return jnp.where(cnt >= cap, cand, cur)

    thr = lax.fori_loop(0, 31, search, jnp.zeros((x.shape[0], 1), I32))
    gt = bits > thr
    eq = bits == thr
    need = cap - jnp.sum(jnp.where(gt, 1.0, 0.0), axis=1, keepdims=True)
    rank_eq, _ = _cumsum_excl(jnp.where(eq, 1.0, 0.0))
    sel = gt | (eq & (rank_eq < need))
    pos, starts = _cumsum_excl(jnp.where(sel, 1.0, 0.0))
    dest_ref[0] = jnp.where(sel, pos.astype(I32), -1)
    gate_ref[0] = jnp.where(sel, aff, 0.0)
    offs_ref[0] = starts.astype(I32)


def _route(lg, cap):
    b, e, n = lg.shape
    nb = n // LANES
    return pl.pallas_call(
        functools.partial(_route_kernel, cap=cap),
        grid=(b,),
        in_specs=[pl.BlockSpec((1, e, n), lambda i: (i, 0, 0))],
        out_specs=[pl.BlockSpec((1, e, n), lambda i: (i, 0, 0)),
                   pl.BlockSpec((1, e, n), lambda i: (i, 0, 0)),
                   pl.BlockSpec((1, nb * e, LANES), lambda i: (i, 0, 0))],
        out_shape=[jax.ShapeDtypeStruct((b, e, n), I32), jax.ShapeDtypeStruct((b, e, n), F32),
                   jax.ShapeDtypeStruct((b, nb * e, LANES), I32)],
        compiler_params=_cparams(("parallel",)),
        name="route",
    )(lg)


def _blk_range(offs_ref, b, j, e, nblk, cap):
    base = (b * nblk + j) * N_EXPERTS + e
    last = (b * nblk + nblk - 1) * N_EXPERTS + e
    return offs_ref[base], jnp.where(j + 1 < nblk, offs_ref[jnp.minimum(base + N_EXPERTS, last)], cap)


def _win_start(off, cap, win):
    return pl.multiple_of(jnp.minimum(lax.shift_right_logical(off, 4) * BF16_ROWS, cap - win), BF16_ROWS)


def _onehot(dest_row, off_al, win):
    rows = lax.broadcasted_iota(I32, (win, TOK_BLK), 0) + off_al
    return jnp.where(rows == dest_row, 1.0, 0.0)


def _moe_ffn_kernel(offs_ref, u_ref, dest_ref, gate_ref, wg_ref, wu_ref, wd_ref, y_ref,
                    xg_ref, gb_ref, wgb_ref, wub_ref, wdb_ref, *, cap, nblk, win, small, nbatch, batch_ffn):
    e = pl.program_id(0)

    def load_weights():
        wgb_ref[...] = wg_ref[0, 0].astype(BF16)
        wub_ref[...] = wu_ref[0, 0].astype(BF16)
        wdb_ref[...] = wd_ref[0, 0].astype(BF16)

    trip = min(GATHER_TRIP, nblk)

    def gather_sample(b, blk, row0):
        xg_ref[pl.ds(row0, cap), :] = jnp.zeros((cap, xg_ref.shape[1]), F32)
        gb_ref[pl.ds(row0, cap), :] = jnp.zeros((cap, LANES), F32)

        def gather(t, carry):
            js = [t * trip + i for i in range(trip)]
            ranges = [_blk_range(offs_ref, b, j, e, nblk, cap) for j in js]

            def run(w):
                for j, (off, _) in zip(js, ranges):
                    st = _win_start(off, cap, w)
                    g = _onehot(dest_ref[blk, 0, pl.ds(j, 1), :], st, w)
                    tok0 = j * TOK_BLK if isinstance(j, int) else pl.multiple_of(j * TOK_BLK, TOK_BLK)
                    uj = u_ref[blk, pl.ds(tok0, TOK_BLK), :]
                    r = pl.multiple_of(row0 + st, BF16_ROWS)
                    xg_ref[pl.ds(r, w), :] += _dot(g.astype(BF16), uj)
                    gsum = jnp.sum(g * gate_ref[blk, 0, pl.ds(j, 1), :], axis=1, keepdims=True)
                    gb_ref[pl.ds(r, w), :] += jnp.broadcast_to(gsum, (w, LANES))

            if small == win:
                run(win)
            else:
                fits = ranges[0][1] <= _win_start(ranges[0][0], cap, small) + small
                for off, end in ranges[1:]:
                    fits = fits & (end <= _win_start(off, cap, small) + small)
                pl.when(fits)(functools.partial(run, small))
                pl.when(jnp.logical_not(fits))(functools.partial(run, win))
            return carry

        if nblk == trip:
            gather(0, 0)
        else:
            lax.fori_loop(0, nblk // trip, gather, 0)

    def ffn(m):
        x = xg_ref[0:m, :].astype(BF16)
        ff = wgb_ref.shape[1]
        fc = min(512, ff)
        acc = jnp.zeros((m, wdb_ref.shape[1]), F32)
        for c in range(ff // fc):
            a = _dot(x, wgb_ref[:, c * fc:(c + 1) * fc])
            up = _dot(x, wub_ref[:, c * fc:(c + 1) * fc])
            acc = acc + _dot((_silu(a) * up).astype(BF16), wdb_ref[c * fc:(c + 1) * fc, :])
        return (acc * gb_ref[0:m, 0:1]).astype(BF16)

    if batch_ffn:
        load_weights()
        for bb in range(nbatch):
            gather_sample(bb, bb, bb * cap)
        y = ffn(nbatch * cap)
        for bb in range(nbatch):
            y_ref[bb, 0] = y[bb * cap:(bb + 1) * cap]
    else:
        b = pl.program_id(1)
        pl.when(b == 0)(load_weights)
        gather_sample(b, 0, 0)
        y_ref[0, 0] = ffn(cap)


def _moe_ffn(offs, u, dest4, gate4, wg, wu, wd, layer, cap):
    b, n, d = u.shape
    nblk = n // TOK_BLK
    win = min(TOK_BLK + BF16_ROWS, cap)
    ff = wg.shape[3]
    batch_ffn = cap < 256
    if batch_ffn:
        grid = (N_EXPERTS,)
        rows = b * cap
        tok_spec = pl.BlockSpec((b, n, d), lambda e, o: (0, 0, 0))
        sel_spec = pl.BlockSpec((b, 1, nblk, TOK_BLK), lambda e, o: (0, e, 0, 0))
        y_spec = pl.BlockSpec((b, 1, cap, d), lambda e, o: (0, e, 0, 0))

        def w_spec(r, c):
            return pl.BlockSpec((1, 1, r, c), lambda e, o: (layer, e, 0, 0))
    else:
        grid = (N_EXPERTS, b)
        rows = cap
        tok_spec = pl.BlockSpec((1, n, d), lambda e, i, o: (i, 0, 0))
        sel_spec = pl.BlockSpec((1, 1, nblk, TOK_BLK), lambda e, i, o: (i, e, 0, 0))
        y_spec = pl.BlockSpec((1, 1, cap, d), lambda e, i, o: (i, e, 0, 0))

        def w_spec(r, c):
            return pl.BlockSpec((1, 1, r, c), lambda e, i, o: (layer, e, 0, 0))
    return pl.pallas_call(
        functools.partial(_moe_ffn_kernel, cap=cap, nblk=nblk, win=win, small=min(SMALL_WIN, cap), nbatch=b,
                          batch_ffn=batch_ffn),
        grid_spec=pltpu.PrefetchScalarGridSpec(
            num_scalar_prefetch=1,
            grid=grid,
            in_specs=[tok_spec, sel_spec, sel_spec, w_spec(d, ff), w_spec(d, ff), w_spec(ff, d)],
            out_specs=y_spec,
            scratch_shapes=[pltpu.VMEM((rows, d), F32), pltpu.VMEM((rows, LANES), F32),
                            pltpu.VMEM((d, ff), BF16), pltpu.VMEM((d, ff), BF16), pltpu.VMEM((ff, d), BF16)]),
        out_shape=jax.ShapeDtypeStruct((b, N_EXPERTS, cap, d), BF16),
        compiler_params=_cparams(("arbitrary",) * len(grid)),
        name="moe_ffn",
    )(offs, u, dest4, gate4, wg, wu, wd)


def _moe_combine_kernel(offs_ref, y_ref, dest_ref, h_ref, mod_ref, out_ref, *, cap, nblk, win, small, nsub):
    b = pl.program_id(0)
    jb = pl.program_id(1)
    gate = mod_ref[0][5:6]
    for s in range(nsub):
        rows = slice(s * TOK_BLK, (s + 1) * TOK_BLK)
        ranges = [_blk_range(offs_ref, b, jb * nsub + s, e, nblk, cap) for e in range(N_EXPERTS)]

        def stacked(s=s, rows=rows, ranges=ranges):
            starts = [_win_start(off, cap, small) for off, _ in ranges]
            ycat = jnp.concatenate([y_ref[0, e, pl.ds(starts[e], small), :] for e in range(N_EXPERTS)], axis=0)
            gcat = jnp.concatenate([_onehot(dest_ref[0, e, s:s + 1, :], starts[e], small) for e in range(N_EXPERTS)], axis=0)
            out_ref[0, rows, :] = h_ref[0, rows, :] + gate * _dot_tn(gcat.astype(BF16), ycat)

        def per_expert(s=s, rows=rows, ranges=ranges):
            acc = jnp.zeros((TOK_BLK, out_ref.shape[2]), F32)
            for e in range(N_EXPERTS):
                st = _win_start(ranges[e][0], cap, win)
                g = _onehot(dest_ref[0, e, s:s + 1, :], st, win).astype(BF16)
                acc = acc + _dot_tn(g, y_ref[0, e, pl.ds(st, win), :])
            out_ref[0, rows, :] = h_ref[0, rows, :] + gate * acc

        if small == win:
            stacked()
        else:
            fits = ranges[0][1] <= _win_start(ranges[0][0], cap, small) + small
            for off, end in ranges[1:]:
                fits = fits & (end <= _win_start(off, cap, small) + small)
            pl.when(fits)(stacked)
            pl.when(jnp.logical_not(fits))(per_expert)


def _moe_combine(offs, y, dest4, h, mod, cap):
    b, n, d = h.shape
    nblk = n // TOK_BLK
    win = min(TOK_BLK + BF16_ROWS, cap)
    small = min(SMALL_WIN, cap)
    nsub = min(8, nblk)
    tb = nsub * TOK_BLK
    nb_mod = mod.shape[0]
    midx = (lambda i, j, o: (i, 0, 0)) if nb_mod > 1 else (lambda i, j, o: (0, 0, 0))
    return pl.pallas_call(
        functools.partial(_moe_combine_kernel, cap=cap, nblk=nblk, win=win, small=small, nsub=nsub),
        grid_spec=pltpu.PrefetchScalarGridSpec(
            num_scalar_prefetch=1,
            grid=(b, n // tb),
            in_specs=[pl.BlockSpec((1, N_EXPERTS, cap, d), lambda i, j, o: (i, 0, 0, 0), pipeline_mode=pl.Buffered(1)),
                      pl.BlockSpec((1, N_EXPERTS, nsub, TOK_BLK), lambda i, j, o: (i, 0, j, 0)),
                      pl.BlockSpec((1, tb, d), lambda i, j, o: (i, j, 0)),
                      pl.BlockSpec((1, 6, d), midx)],
            out_specs=pl.BlockSpec((1, tb, d), lambda i, j, o: (i, j, 0))),
        out_shape=jax.ShapeDtypeStruct((b, n, d), F32),
        compiler_params=_cparams(("arbitrary", "arbitrary")),
        name="moe_combine",
    )(offs, y, dest4, h, mod)


def _moe(h, u2, lg, mod, wg, wu, wd, layer):
    b, n, d = h.shape
    cap = EC_CAPACITY * n // N_EXPERTS
    nblk = n // TOK_BLK
    dest, gate, starts = _route(lg, cap)
    offs = starts[:, :, 0].reshape(-1)
    dest4 = dest.reshape(b, N_EXPERTS, nblk, TOK_BLK)
    gate4 = gate.reshape(b, N_EXPERTS, nblk, TOK_BLK)
    y = _moe_ffn(offs, u2, dest4, gate4, wg, wu, wd, layer, cap)
    return _moe_combine(offs, y, dest4, h, mod, cap)


def _gla_proj_kernel(x_ref, mod_ref, g1_ref, w_ref, wz_ref, wgate_ref, bgate_ref,
                     q_ref, k_ref, v_ref, og_ref, gf_ref, gb_ref, *, dk, dv):
    m = mod_ref[0]
    u = _rms_mod(x_ref[0], g1_ref[...], m[0:1], m[1:2]).astype(BF16)
    acc = _dot(u, w_ref[...])
    z = _dot(u, wz_ref[...]).astype(BF16)
    gp = _dot(z, wgate_ref[...]) + bgate_ref[...]
    g = (jnp.minimum(gp, 0.0) - jnp.log1p(jnp.exp(-jnp.abs(gp)))) * (1.0 / GLA_GATE_NORM)
    q_ref[0] = acc[:, :dk]
    k_ref[0] = acc[:, dk:2 * dk]
    v_ref[0] = acc[:, 2 * dk:2 * dk + dv].astype(BF16)
    og_ref[0] = acc[:, 2 * dk + dv:]
    gf_ref[0] = g[:, :dk]
    gb_ref[0] = g[:, dk:]


def _gla_proj(h, mod, g1, w, wz, wgate, bgate, dk, dv):
    b, t, d = h.shape
    tm = min(512, t)
    nb_mod = mod.shape[0]
    midx = (lambda i, j: (i, 0, 0)) if nb_mod > 1 else (lambda i, j: (0, 0, 0))
    cols = 2 * dk + 2 * dv
    widths = [(dk, F32), (dk, F32), (dv, BF16), (dv, F32), (dk, F32), (dk, F32)]
    return pl.pallas_call(
        functools.partial(_gla_proj_kernel, dk=dk, dv=dv),
        grid=(b, t // tm),
        in_specs=[pl.BlockSpec((1, tm, d), lambda i, j: (i, j, 0)),
                  pl.BlockSpec((1, 6, d), midx),
                  pl.BlockSpec((1, d), lambda i, j: (0, 0)),
                  pl.BlockSpec((d, cols), lambda i, j: (0, 0)),
                  pl.BlockSpec((d, LANES), lambda i, j: (0, 0)),
                  pl.BlockSpec((LANES, 2 * dk), lambda i, j: (0, 0)),
                  pl.BlockSpec((1, 2 * dk), lambda i, j: (0, 0))],
        out_specs=[pl.BlockSpec((1, tm, wd), lambda i, j: (i, j, 0)) for wd, _ in widths],
        out_shape=[jax.ShapeDtypeStruct((b, t, wd), dt) for wd, dt in widths],
        compiler_params=_cparams(("parallel", "parallel")),
        name="gla_proj",
    )(h, mod, g1.reshape(1, d), w, wz, wgate, bgate)


def _gla_scan(q_ref, k_ref, v_ref, g_ref, acc_ref, st_ref, *, n_rows, reverse, scale):
    c = GLA_CHUNK
    gr = min(GLA_GROUP * c, n_rows)
    cpg = gr // c
    gpi = min(GLA_GROUPS_PER_TRIP, n_rows // gr)
    rows_it = gpi * gr
    n_it = n_rows // rows_it
    ii = lax.broadcasted_iota(I32, (gr, gr), 0)
    jj = lax.broadcasted_iota(I32, (gr, gr), 1)
    sh = c.bit_length() - 1
    same = (ii >> sh) == (jj >> sh)
    tri = same & ((jj >= ii) if reverse else (jj <= ii))
    tri_b = jnp.where(tri, 1.0, 0.0).astype(BF16)
    g_order = range(gpi - 1, -1, -1) if reverse else range(gpi)
    c_order = range(cpg - 1, -1, -1) if reverse else range(cpg)

    def body(i, carry):
        r0 = 0 if n_it == 1 else pl.multiple_of(((n_it - 1 - i) if reverse else i) * rows_it, rows_it)
        q_all = q_ref[0, pl.ds(r0, rows_it), :]
        k_all = k_ref[0, pl.ds(r0, rows_it), :]
        v_all = v_ref[0, pl.ds(r0, rows_it), :]
        g_all = g_ref[0, pl.ds(r0, rows_it), :]
        prev = acc_ref[pl.ds(r0, rows_it), :] if reverse else None
        st = st_ref[...]
        outs = [None] * gpi
        for gi in g_order:
            rs = slice(gi * gr, (gi + 1) * gr)
            k = k_all[rs]
            v = v_all[rs]
            g_hi, g_lo = _split2(g_all[rs])
            cum = _dot(tri_b, g_hi) + _dot(tri_b, g_lo)
            cl = [cum[ci * c:ci * c + 1] if reverse else cum[(ci + 1) * c - 1:(ci + 1) * c] for ci in range(cpg)]
            cl_rows = jnp.concatenate([jnp.broadcast_to(x, (c, x.shape[1])) for x in cl], axis=0)
            q_dec = (q_all[rs] * scale * jnp.exp(cum)).astype(BF16)
            k_inv = (k * jnp.exp(-cum)).astype(BF16)
            k_st = (k * jnp.exp(cl_rows - cum)).astype(BF16)
            a = jnp.where(tri, _dot_nt(q_dec, k_inv), 0.0)
            o_intra = _dot(a.astype(BF16), v)
            parts = [None] * cpg
            for ci in c_order:
                cs = slice(ci * c, (ci + 1) * c)
                parts[ci] = o_intra[cs] + _dot_nt(q_dec[cs], st.astype(BF16))
                st = st * jnp.exp(cl[ci]) + _dot_tn(v[cs], k_st[cs])
            outs[gi] = jnp.concatenate(parts, axis=0)
        st_ref[...] = st
        o_all = outs[0] if gpi == 1 else jnp.concatenate(outs, axis=0)
        acc_ref[pl.ds(r0, rows_it), :] = (prev + o_all) if reverse else o_all
        return carry

    if n_it == 1:
        body(0, 0)
    else:
        lax.fori_loop(0, n_it, body, 0)


def _gla_merge(acc_ref, og_ref, gain, o_ref, t):
    tm = min(512, t)
    for r in range(t // tm):
        o = acc_ref[r * tm:(r + 1) * tm, :]
        ms = jnp.mean(o * o, axis=-1, keepdims=True)
        y = (o * lax.rsqrt(ms + EPS) * gain) * _silu(og_ref[0, r * tm:(r + 1) * tm, :])
        o_ref[0, r * tm:(r + 1) * tm, :] = y.astype(BF16)


def _gla_kernel(*refs, n, l, dk, ctx_out):
    q_ref, k_ref, v_ref, gf_ref, gb_ref, og_ref, cq_ref, ck_ref, cv_ref, cgf_ref, cgb_ref, cog_ref, on_ref = refs[:13]
    if ctx_out:
        o_ref, co_ref, acc_ref, cacc_ref, st_ref = refs[13:]
    else:
        o_ref, acc_ref, cacc_ref, st_ref = refs[13:]
    scale = dk ** -0.5
    for reverse, g_ref, cg_ref in ((False, gf_ref, cgf_ref), (True, gb_ref, cgb_ref)):
        st_ref[...] = jnp.zeros_like(st_ref)
        _gla_scan(cq_ref, ck_ref, cv_ref, cg_ref, cacc_ref, st_ref, n_rows=l, reverse=reverse, scale=scale)
        _gla_scan(q_ref, k_ref, v_ref, g_ref, acc_ref, st_ref, n_rows=n, reverse=reverse, scale=scale)
    gain = on_ref[...]
    _gla_merge(acc_ref, og_ref, gain, o_ref, n)
    if ctx_out:
        _gla_merge(cacc_ref, cog_ref, gain, co_ref, l)


def _gla(lat, ctx, onorm, ctx_out):
    q, k, v, og, gf, gb = lat
    cq, ck, cv, cog, cgf, cgb = ctx
    b, n, dkt = q.shape
    l = cq.shape[1]
    dvt = v.shape[2]
    dk = dkt // GLA_HEADS
    dv = dvt // GLA_HEADS

    def spec(t, w):
        return pl.BlockSpec((1, t, w), lambda i, h: (i, 0, h))

    in_specs = [spec(n, dk), spec(n, dk), spec(n, dv), spec(n, dk), spec(n, dk), spec(n, dv),
                spec(l, dk), spec(l, dk), spec(l, dv), spec(l, dk), spec(l, dk), spec(l, dv),
                pl.BlockSpec((1, dv), lambda i, h: (0, 0))]
    out_specs = [spec(n, dv)]
    out_shape = [jax.ShapeDtypeStruct((b, n, dvt), BF16)]
    if ctx_out:
        out_specs.append(spec(l, dv))
        out_shape.append(jax.ShapeDtypeStruct((b, l, dvt), BF16))
    res = pl.pallas_call(
        functools.partial(_gla_kernel, n=n, l=l, dk=dk, ctx_out=ctx_out),
        grid=(b, GLA_HEADS),
        in_specs=in_specs,
        out_specs=out_specs,
        out_shape=out_shape,
        scratch_shapes=[pltpu.VMEM((n, dv), F32), pltpu.VMEM((l, dv), F32), pltpu.VMEM((dv, dk), F32)],
        compiler_params=_cparams(("parallel", "parallel")),
        name="gla",
    )(q, k, v, gf, gb, og, cq, ck, cv, cgf, cgb, cog, onorm.reshape(1, dv))
    return (res[0], res[1]) if ctx_out else (res[0], None)


def _rope_tables(n):
    half = HEAD_DIM // 2
    t = jnp.arange(n, dtype=I32)
    inv = ROPE_THETA ** (-jnp.arange(0, half, 2, dtype=F32) / half)

    def tab(pos):
        ang = pos.astype(F32)[:, None] * inv[None, :]
        ang = jnp.concatenate([ang, ang], axis=-1)
        return jnp.cos(ang), jnp.sin(ang)

    cos_r, sin_r = tab(t // GRID_W)
    cos_c, sin_c = tab(t % GRID_W)
    cos = jnp.concatenate([cos_r, cos_c], axis=-1)
    sin = jnp.concatenate([sin_r, sin_c], axis=-1)
    sign = jnp.where((jnp.arange(HEAD_DIM) % half) < half // 2, -1.0, 1.0).astype(F32)
    reps = LANES // HEAD_DIM
    return jnp.tile(cos, (1, reps)), jnp.tile(sin * sign[None, :], (1, reps))


def _block_ones():
    i = np.arange(LANES)
    return jnp.asarray((i[:, None] // HEAD_DIM) == (i[None, :] // HEAD_DIM), BF16)


def _attn_gains(qa_g, ka_g, qb_g, kb_g):
    qs = HEAD_DIM ** -0.5
    one_a = jnp.ones((A_KV_HEADS * HEAD_DIM,), F32)
    one_b = jnp.ones((B_HEADS * HEAD_DIM,), F32)
    return jnp.concatenate([jnp.tile(qa_g, A_HEADS) * qs, jnp.tile(ka_g, A_KV_HEADS), one_a,
                            jnp.tile(qb_g, B_HEADS) * qs, jnp.tile(kb_g, B_HEADS), one_b]).reshape(1, -1)


def kernel(x, c, ctx, c_ctx, ada_w, ada_b, norm1_g, norm2_g, attn_w_in, attn_w_out, a_q_norm, a_k_norm, b_q_norm,
           b_k_norm, na_rpb, gla_w_in, gla_gk_w_f, gla_gk_b_f, gla_gk_w_b, gla_gk_b_b, gla_o_norm, gla_w_out,
           moe_router, moe_w_gate, moe_w_up, moe_w_down):
    b, n, d = x.shape
    depth = ada_w.shape[0]
    dk = gla_gk_w_f.shape[2]
    dv = gla_w_out.shape[1]
    rows = -(-(b + 1) // 8) * 8
    cvec = jnp.zeros((rows, d), F32).at[:b].set(c).at[b].set(c_ctx)
    mods = _mods(cvec, ada_w, ada_b)
    cos, sin = _rope_tables(n)
    bd = _block_ones()
    h, hc = x, ctx
    for layer in range(depth):
        last = layer == depth - 1
        i = layer // 2
        ml = mods[layer, :b].reshape(b, 6, d)
        mc = mods[layer, b:b + 1].reshape(1, 6, d)
        if layer % 2 == 0:
            w_in = attn_w_in[i].astype(BF16)
            gains = _attn_gains(a_q_norm[i], a_k_norm[i], b_q_norm[i], b_k_norm[i])
            qa, ka, va, qb, kb, vb = _attn_proj(h, ml, norm1_g[layer], w_in, gains, bd, cos, sin)
            cqa, cka, cva, cqb, ckb, cvb = _attn_proj(hc, mc, norm1_g[layer], w_in, gains, bd)
            oa = _gqa(qa, jnp.concatenate([ka, cka], axis=1), jnp.concatenate([va, cva], axis=1), A_HEADS, A_KV_HEADS)
            ob = _na(qb, kb, vb, ckb, cvb, _na_bias_table(na_rpb[i], n // GRID_W))
            o_lat = [oa, ob]
            o_ctx = None if last else [_gqa(cqa, cka, cva, A_HEADS, A_KV_HEADS), _gqa(cqb, ckb, cvb, B_HEADS, B_HEADS)]
            w_out = attn_w_out[i].astype(BF16)
        else:
            w_main = gla_w_in[i][:, :2 * dk + 2 * dv].astype(BF16)
            wz = jnp.pad(gla_w_in[i][:, 2 * dk + 2 * dv:], ((0, 0), (0, LANES - 2 * GLA_GATE_RANK))).astype(BF16)
            wgate = jnp.zeros((LANES, 2 * dk), F32)
            wgate = wgate.at[:GLA_GATE_RANK, :dk].set(gla_gk_w_f[i]).at[GLA_GATE_RANK:2 * GLA_GATE_RANK, dk:].set(gla_gk_w_b[i])
            bgate = jnp.concatenate([gla_gk_b_f[i], gla_gk_b_b[i]]).reshape(1, 2 * dk)
            lat = _gla_proj(h, ml, norm1_g[layer], w_main, wz, wgate.astype(BF16), bgate, dk, dv)
            cx = _gla_proj(hc, mc, norm1_g[layer], w_main, wz, wgate.astype(BF16), bgate, dk, dv)
            o, co = _gla(lat, cx, gla_o_norm[i], not last)
            o_lat = [o]
            o_ctx = None if last else [co]
            w_out = gla_w_out[i].astype(BF16)
        wr_hi, wr_lo = _split2(moe_router[layer].T)
        h1, u2, lg = _outproj(o_lat, h, ml, norm2_g[layer], w_out, wr_hi, wr_lo)
        h = _moe(h1, u2, lg, ml, moe_w_gate, moe_w_up, moe_w_down, layer)
        if not last:
            hc1, uc2, lgc = _outproj(o_ctx, hc, mc, norm2_g[layer], w_out, wr_hi, wr_lo)
            hc = _moe(hc1, uc2, lgc, mc, moe_w_gate, moe_w_up, moe_w_down, layer)
    return h
```

```python
import functools

import numpy as np
import jax
import jax.numpy as jnp
from jax import lax
from jax.experimental import pallas as pl
from jax.experimental.pallas import tpu as pltpu

F32 = jnp.float32
BF16 = jnp.bfloat16
I32 = jnp.int32

GRID_W = 64
HEAD_DIM = 64
A_HEADS = 8
A_KV_HEADS = 2
B_HEADS = 8
NA_ROWS = 8
NA_COLS = 16
ROPE_THETA = 10000.0
GLA_HEADS = 4
GLA_GATE_RANK = 16
GLA_GATE_NORM = 16.0
GLA_CHUNK = 64
GLA_GROUP = 4
GLA_GROUPS_PER_TRIP = 8
ROW_SPLIT = 2
N_EXPERTS = 16
EC_CAPACITY = 2
EPS = 1e-6

LANES = 128
BF16_ROWS = 16
NA_QROWS = 4
TOK_BLK = LANES
SMALL_WIN = 48
GATHER_TRIP = 8
MASK_VALUE = -1e30
VMEM_LIMIT = 56 * 1024 * 1024

_NT = (((1,), (1,)), ((), ()))
_TN = (((0,), (0,)), ((), ()))


def _cparams(sem):
    return pltpu.CompilerParams(dimension_semantics=sem, vmem_limit_bytes=VMEM_LIMIT)


def _dot(a, b):
    return jnp.dot(a, b, preferred_element_type=F32)


def _dot_nt(a, b):
    return lax.dot_general(a, b, _NT, preferred_element_type=F32)


def _dot_tn(a, b):
    return lax.dot_general(a, b, _TN, preferred_element_type=F32)


def _split2(a):
    hi = a.astype(BF16)
    lo = (a - hi.astype(F32)).astype(BF16)
    return hi, lo


def _silu(a):
    return a / (1.0 + jnp.exp(-a))


def _rms_mod(x, g, shift, scale):
    ms = jnp.mean(x * x, axis=-1, keepdims=True)
    return (x * lax.rsqrt(ms + EPS) * g) * (1.0 + scale) + shift


def _mods_kernel(c_ref, w_ref, b_ref, o_ref):
    c = c_ref[...]
    s_hi, s_lo = _split2(_silu(c))
    w_hi, w_lo = _split2(w_ref[0])
    o_ref[0] = _dot(s_hi, w_hi) + _dot(s_lo, w_hi) + _dot(s_hi, w_lo) + b_ref[0]


def _mods(cvec, ada_w, ada_b):
    depth, d, d6 = ada_w.shape
    rows = cvec.shape[0]
    tn = 1536 if d6 % 1536 == 0 else d6
    return pl.pallas_call(
        _mods_kernel,
        grid=(depth, d6 // tn),
        in_specs=[pl.BlockSpec((rows, d), lambda l, j: (0, 0)),
                  pl.BlockSpec((1, d, tn), lambda l, j: (l, 0, j)),
                  pl.BlockSpec((1, 1, tn), lambda l, j: (l, 0, j))],
        out_specs=pl.BlockSpec((1, rows, tn), lambda l, j: (l, 0, j)),
        out_shape=jax.ShapeDtypeStruct((depth, rows, d6), F32),
        compiler_params=_cparams(("arbitrary", "arbitrary")),
        name="mods",
    )(cvec, ada_w, ada_b.reshape(depth, 1, d6))


_ATTN_COLS = (A_HEADS + 2 * A_KV_HEADS + 3 * B_HEADS) * HEAD_DIM
_QA = (0, 4)
_KA = (4, 5)
_VA = (5, 6)
_QB = (6, 10)
_KB = (10, 14)
_VB = (14, 18)


def _attn_proj_kernel(*refs, rope):
    if rope:
        x_ref, mod_ref, g1_ref, w_ref, gain_ref, bd_ref, cos_ref, sin_ref = refs[:8]
        outs = refs[8:]
    else:
        x_ref, mod_ref, g1_ref, w_ref, gain_ref, bd_ref = refs[:6]
        outs = refs[6:]
    qa_ref, ka_ref, va_ref, qb_ref, kb_ref, vb_ref = outs
    m = mod_ref[0]
    u = _rms_mod(x_ref[0], g1_ref[...], m[0:1], m[1:2]).astype(BF16)
    acc = _dot(u, w_ref[...])
    bd = bd_ref[...]
    if rope:
        cos = cos_ref[...]
        sin = sin_ref[...]
        lane = lax.broadcasted_iota(I32, cos.shape, 1)
        first = (lane & 31) < 16

    def chunk(j, norm, rot):
        cch = acc[:, j * LANES:(j + 1) * LANES]
        if norm:
            sq_hi, sq_lo = _split2(cch * cch)
            ss = _dot(sq_hi, bd) + _dot(sq_lo, bd)
            cch = cch * lax.rsqrt(ss * (1.0 / HEAD_DIM) + EPS) * gain_ref[:, j * LANES:(j + 1) * LANES]
        if rot:
            partner = jnp.where(first, pltpu.roll(cch, LANES - 16, 1), pltpu.roll(cch, 16, 1))
            cch = cch * cos + partner * sin
        return cch.astype(BF16)

    for (lo, hi), ref, norm, rot in ((_QA, qa_ref, True, rope), (_KA, ka_ref, True, rope), (_VA, va_ref, False, False),
                                     (_QB, qb_ref, True, False), (_KB, kb_ref, True, False), (_VB, vb_ref, False, False)):
        for j in range(lo, hi):
            ref[0, :, (j - lo) * LANES:(j - lo + 1) * LANES] = chunk(j, norm, rot)


def _attn_proj(h, mod, g1, w, gains, bd, cos=None, sin=None):
    b, t, d = h.shape
    tm = min(512, t)
    rope = cos is not None
    nb_mod = mod.shape[0]
    midx = (lambda i, j: (i, 0, 0)) if nb_mod > 1 else (lambda i, j: (0, 0, 0))
    in_specs = [pl.BlockSpec((1, tm, d), lambda i, j: (i, j, 0)),
                pl.BlockSpec((1, 6, d), midx),
                pl.BlockSpec((1, d), lambda i, j: (0, 0)),
                pl.BlockSpec((d, _ATTN_COLS), lambda i, j: (0, 0)),
                pl.BlockSpec((1, _ATTN_COLS), lambda i, j: (0, 0)),
                pl.BlockSpec((LANES, LANES), lambda i, j: (0, 0))]
    args = [h, mod, g1.reshape(1, d), w, gains, bd]
    if rope:
        in_specs += [pl.BlockSpec((tm, LANES), lambda i, j: (j, 0))] * 2
        args += [cos, sin]
    widths = [(hi - lo) * LANES for lo, hi in (_QA, _KA, _VA, _QB, _KB, _VB)]
    return pl.pallas_call(
        functools.partial(_attn_proj_kernel, rope=rope),
        grid=(b, t // tm),
        in_specs=in_specs,
        out_specs=[pl.BlockSpec((1, tm, wd), lambda i, j: (i, j, 0)) for wd in widths],
        out_shape=[jax.ShapeDtypeStruct((b, t, wd), BF16) for wd in widths],
        compiler_params=_cparams(("parallel", "parallel")),
        name="attn_proj_rope" if rope else "attn_proj",
    )(*args)


def _gqa_kernel(q_ref, k_ref, v_ref, o_ref, *, hq, hk):
    grp = hq // hk
    for h in range(hq):
        kv = h // grp
        q = q_ref[0, :, h * HEAD_DIM:(h + 1) * HEAD_DIM]
        k = k_ref[0, :, kv * HEAD_DIM:(kv + 1) * HEAD_DIM]
        v = v_ref[0, :, kv * HEAD_DIM:(kv + 1) * HEAD_DIM]
        s = _dot_nt(q, k)
        p = jnp.exp(s - jnp.max(s, axis=-1, keepdims=True))
        l = jnp.sum(p, axis=-1, keepdims=True)
        o = _dot(p.astype(BF16), v) / l
        o_ref[0, :, h * HEAD_DIM:(h + 1) * HEAD_DIM] = o.astype(BF16)


def _gqa(q, k, v, hq, hk):
    b, t, _ = q.shape
    s = k.shape[1]
    tq = min(512, t)
    return pl.pallas_call(
        functools.partial(_gqa_kernel, hq=hq, hk=hk),
        grid=(b, t // tq),
        in_specs=[pl.BlockSpec((1, tq, hq * HEAD_DIM), lambda i, j: (i, j, 0)),
                  pl.BlockSpec((1, s, hk * HEAD_DIM), lambda i, j: (i, 0, 0)),
                  pl.BlockSpec((1, s, hk * HEAD_DIM), lambda i, j: (i, 0, 0))],
        out_specs=pl.BlockSpec((1, tq, hq * HEAD_DIM), lambda i, j: (i, j, 0)),
        out_shape=jax.ShapeDtypeStruct((b, t, hq * HEAD_DIM), BF16),
        compiler_params=_cparams(("parallel", "parallel")),
        name="gqa",
    )(q, k, v)


def _na_geometry(rows):
    kr = min(NA_QROWS + NA_ROWS - 1, rows)
    nblk = rows // NA_QROWS
    return kr, nblk


def _na_bias_table(rpb, rows):
    kr, nblk = _na_geometry(rows)
    wr = min(NA_ROWS, rows)
    heads = rpb.shape[0]
    qc = np.arange(GRID_W)[:, None]
    kc = np.arange(GRID_W)[None, :]
    cs = np.clip(qc - NA_COLS // 2, 0, GRID_W - NA_COLS)
    col_ok = (kc >= cs) & (kc < cs + NA_COLS)
    pick = (np.arange(2 * NA_COLS - 1)[:, None, None] == (kc - qc + NA_COLS - 1)[None]) & col_ok[None]
    toeplitz = jnp.einsum("hrd,dqk->hrqk", rpb.astype(F32), jnp.asarray(pick, F32), precision=lax.Precision.HIGHEST)
    toeplitz = jnp.where(col_ok[None, None], toeplitz, MASK_VALUE)
    tabs = []
    for rb in (0, min(1, nblk - 1), nblk - 1):
        qr0 = rb * NA_QROWS
        kr0 = int(np.clip(qr0 - NA_ROWS // 2, 0, rows - kr))
        qr = qr0 + np.arange(NA_QROWS)[:, None]
        krr = kr0 + np.arange(kr)[None, :]
        rs = np.clip(qr - wr // 2, 0, rows - wr)
        row_ok = (krr >= rs) & (krr < rs + wr)
        dr = krr - qr + (NA_ROWS - 1)
        masked = jnp.full((heads, GRID_W, GRID_W), MASK_VALUE, F32)
        blocks = [jnp.concatenate([toeplitz[:, dr[i, j]] if row_ok[i, j] else masked for j in range(kr)], axis=2)
                  for i in range(NA_QROWS)]
        tabs.append(jnp.concatenate(blocks, axis=1))
    return jnp.stack(tabs)


def _na_kernel(q_ref, k_ref, v_ref, kc_ref, vc_ref, bias_ref, o_ref, *, rows, heads):
    kr, _ = _na_geometry(rows)
    rb = pl.program_id(1)
    kr0 = jnp.clip(rb * NA_QROWS - NA_ROWS // 2, 0, rows - kr)
    start = pl.multiple_of(kr0 * GRID_W, GRID_W)
    kw = k_ref[0, pl.ds(start, kr * GRID_W), :]
    vw = v_ref[0, pl.ds(start, kr * GRID_W), :]
    for h in range(heads):
        sl = slice(h * HEAD_DIM, (h + 1) * HEAD_DIM)
        q = q_ref[0, :, sl]
        s_w = _dot_nt(q, kw[:, sl]) + bias_ref[0, h]
        s_c = _dot_nt(q, kc_ref[0, :, sl])
        m = jnp.maximum(jnp.max(s_w, axis=-1, keepdims=True), jnp.max(s_c, axis=-1, keepdims=True))
        p_w = jnp.exp(s_w - m)
        p_c = jnp.exp(s_c - m)
        l = jnp.sum(p_w, axis=-1, keepdims=True) + jnp.sum(p_c, axis=-1, keepdims=True)
        o = (_dot(p_w.astype(BF16), vw[:, sl]) + _dot(p_c.astype(BF16), vc_ref[0, :, sl])) / l
        o_ref[0, :, sl] = o.astype(BF16)


def _na(q, k, v, kc, vc, bias):
    b, n, c = q.shape
    l = kc.shape[1]
    rows = n // GRID_W
    kr, nblk = _na_geometry(rows)
    tq = NA_QROWS * GRID_W
    heads = c // HEAD_DIM

    def pattern(i, j):
        return (jnp.where(j == 0, 0, jnp.where(j == nblk - 1, 2, 1)), 0, 0, 0)

    return pl.pallas_call(
        functools.partial(_na_kernel, rows=rows, heads=heads),
        grid=(b, nblk),
        in_specs=[pl.BlockSpec((1, tq, c), lambda i, j: (i, j, 0)),
                  pl.BlockSpec((1, n, c), lambda i, j: (i, 0, 0)),
                  pl.BlockSpec((1, n, c), lambda i, j: (i, 0, 0)),
                  pl.BlockSpec((1, l, c), lambda i, j: (i, 0, 0)),
                  pl.BlockSpec((1, l, c), lambda i, j: (i, 0, 0)),
                  pl.BlockSpec((1, heads, tq, kr * GRID_W), pattern)],
        out_specs=pl.BlockSpec((1, tq, c), lambda i, j: (i, j, 0)),
        out_shape=jax.ShapeDtypeStruct((b, n, c), BF16),
        compiler_params=_cparams(("parallel", "arbitrary")),
        name="na_attn",
    )(q, k, v, kc, vc, bias)


def _outproj_kernel(*refs, n_o, route):
    o_refs = refs[:n_o]
    h_ref, mod_ref, g2_ref, w_ref = refs[n_o:n_o + 4]
    rest = refs[n_o + 4:]
    if route:
        wr_hi_ref, wr_lo_ref, hn_ref, u2_ref, lg_ref = rest
    else:
        hn_ref, = rest
    m = mod_ref[0]
    o = o_refs[0][0] if n_o == 1 else jnp.concatenate([r[0] for r in o_refs], axis=1)
    hn = h_ref[0] + m[2:3] * _dot(o, w_ref[...])
    hn_ref[0] = hn
    if route:
        u2 = _rms_mod(hn, g2_ref[...], m[3:4], m[4:5])
        u_hi, u_lo = _split2(u2)
        u2_ref[0] = u_hi
        wr_hi = wr_hi_ref[...]
        lg_ref[0] = _dot_nt(wr_hi, u_hi) + _dot_nt(wr_hi, u_lo) + _dot_nt(wr_lo_ref[...], u_hi)


def _outproj(o_list, h, mod, g2, w, wr_hi=None, wr_lo=None):
    b, t, d = h.shape
    tm = min(512, t)
    route = wr_hi is not None
    nb_mod = mod.shape[0]
    midx = (lambda i, j: (i, 0, 0)) if nb_mod > 1 else (lambda i, j: (0, 0, 0))
    in_specs = [pl.BlockSpec((1, tm, o.shape[2]), lambda i, j: (i, j, 0)) for o in o_list]
    in_specs += [pl.BlockSpec((1, tm, d), lambda i, j: (i, j, 0)),
                 pl.BlockSpec((1, 6, d), midx),
                 pl.BlockSpec((1, d), lambda i, j: (0, 0)),
                 pl.BlockSpec((d, d), lambda i, j: (0, 0))]
    args = list(o_list) + [h, mod, g2.reshape(1, d), w]
    out_specs = [pl.BlockSpec((1, tm, d), lambda i, j: (i, j, 0))]
    out_shape = [jax.ShapeDtypeStruct((b, t, d), F32)]
    if route:
        in_specs += [pl.BlockSpec((N_EXPERTS, d), lambda i, j: (0, 0))] * 2
        args += [wr_hi, wr_lo]
        out_specs += [pl.BlockSpec((1, tm, d), lambda i, j: (i, j, 0)),
                      pl.BlockSpec((1, N_EXPERTS, tm), lambda i, j: (i, 0, j))]
        out_shape += [jax.ShapeDtypeStruct((b, t, d), BF16), jax.ShapeDtypeStruct((b, N_EXPERTS, t), F32)]
    return pl.pallas_call(
        functools.partial(_outproj_kernel, n_o=len(o_list), route=route),
        grid=(b, t // tm),
        in_specs=in_specs,
        out_specs=out_specs,
        out_shape=out_shape,
        compiler_params=_cparams(("parallel", "parallel")),
        name="outproj",
    )(*args)


def _cumsum_excl(x):
    e, n = x.shape
    nb = n // LANES
    st = jnp.concatenate([x[:, j * LANES:(j + 1) * LANES] for j in range(nb)], axis=0)
    ii = lax.broadcasted_iota(I32, (LANES, LANES), 0)
    jj = lax.broadcasted_iota(I32, (LANES, LANES), 1)
    incl = _dot(st.astype(BF16), jnp.where(ii <= jj, 1.0, 0.0).astype(BF16))
    tot = jnp.broadcast_to(incl[:, LANES - 1:LANES], incl.shape)
    r = lax.broadcasted_iota(I32, (nb * e, nb * e), 0)
    c = lax.broadcasted_iota(I32, (nb * e, nb * e), 1)
    sh = e.bit_length() - 1
    same = (r & (e - 1)) == (c & (e - 1))
    prev = jnp.where(same & (lax.shift_right_logical(c, sh) < lax.shift_right_logical(r, sh)), 1.0, 0.0).astype(BF16)
    starts = _dot(prev, tot.astype(BF16))
    excl_st = incl - st + starts
    excl = jnp.concatenate([excl_st[j * e:(j + 1) * e, :] for j in range(nb)], axis=1)
    return excl, starts


def _route_kernel(lg_ref, dest_ref, gate_ref, offs_ref, *, cap):
    x = lg_ref[0]
    ex = jnp.exp(x - jnp.max(x, axis=0, keepdims=True))
    aff = ex / jnp.sum(ex, axis=0, keepdims=True)
    bits = lax.bitcast_convert_type(aff, I32)

    def search(i, cur):
        cand = cur | lax.shift_left(jnp.int32(1), 30 - i)
        cnt = jnp.sum(jnp.where(bits >= cand, 1.0, 0.0), axis=1, keepdims=True)
        return jnp.where(cnt >= cap, cand, cur)

    thr = lax.fori_loop(0, 31, search, jnp.zeros((x.shape[0], 1), I32))
    gt = bits > thr
    eq = bits == thr
    need = cap - jnp.sum(jnp.where(gt, 1.0, 0.0), axis=1, keepdims=True)
    rank_eq, _ = _cumsum_excl(jnp.where(eq, 1.0, 0.0))
    sel = gt | (eq & (rank_eq < need))
    pos, starts = _cumsum_excl(jnp.where(sel, 1.0, 0.0))
    dest_ref[0] = jnp.where(sel, pos.astype(I32), -1)
    gate_ref[0] = jnp.where(sel, aff, 0.0)
    offs_ref[0] = starts.astype(I32)


def _route(lg, cap):
    b, e, n = lg.shape
    nb = n // LANES
    return pl.pallas_call(
        functools.partial(_route_kernel, cap=cap),
        grid=(b,),
        in_specs=[pl.BlockSpec((1, e, n), lambda i: (i, 0, 0))],
        out_specs=[pl.BlockSpec((1, e, n), lambda i: (i, 0, 0)),
                   pl.BlockSpec((1, e, n), lambda i: (i, 0, 0)),
                   pl.BlockSpec((1, nb * e, LANES), lambda i: (i, 0, 0))],
        out_shape=[jax.ShapeDtypeStruct((b, e, n), I32), jax.ShapeDtypeStruct((b, e, n), F32),
                   jax.ShapeDtypeStruct((b, nb * e, LANES), I32)],
        compiler_params=_cparams(("parallel",)),
        name="route",
    )(lg)


def _blk_range(offs_ref, b, j, e, nblk, cap):
    base = (b * nblk + j) * N_EXPERTS + e
    last = (b * nblk + nblk - 1) * N_EXPERTS + e
    return offs_ref[base], jnp.where(j + 1 < nblk, offs_ref[jnp.minimum(base + N_EXPERTS, last)], cap)


def _win_start(off, cap, win):
    return pl.multiple_of(jnp.minimum(lax.shift_right_logical(off, 4) * BF16_ROWS, cap - win), BF16_ROWS)


def _onehot(dest_row, off_al, win):
    rows = lax.broadcasted_iota(I32, (win, TOK_BLK), 0) + off_al
    return jnp.where(rows == dest_row, 1.0, 0.0)


def _moe_ffn_kernel(offs_ref, u_ref, dest_ref, gate_ref, wg_ref, wu_ref, wd_ref, y_ref,
                    xg_ref, gb_ref, wgb_ref, wub_ref, wdb_ref, *, cap, nblk, win, small, nbatch, batch_ffn):
    e = pl.program_id(0)

    def load_weights():
        wgb_ref[...] = wg_ref[0, 0].astype(BF16)
        wub_ref[...] = wu_ref[0, 0].astype(BF16)
        wdb_ref[...] = wd_ref[0, 0].astype(BF16)

    trip = min(GATHER_TRIP, nblk)

    def gather_sample(b, blk, row0):
        xg_ref[pl.ds(row0, cap), :] = jnp.zeros((cap, xg_ref.shape[1]), F32)
        gb_ref[pl.ds(row0, cap), :] = jnp.zeros((cap, LANES), F32)

        def gather(t, carry):
            js = [t * trip + i for i in range(trip)]
            ranges = [_blk_range(offs_ref, b, j, e, nblk, cap) for j in js]

            def run(w):
                for j, (off, _) in zip(js, ranges):
                    st = _win_start(off, cap, w)
                    g = _onehot(dest_ref[blk, 0, pl.ds(j, 1), :], st, w)
                    tok0 = j * TOK_BLK if isinstance(j, int) else pl.multiple_of(j * TOK_BLK, TOK_BLK)
                    uj = u_ref[blk, pl.ds(tok0, TOK_BLK), :]
                    r = pl.multiple_of(row0 + st, BF16_ROWS)
                    xg_ref[pl.ds(r, w), :] += _dot(g.astype(BF16), uj)
                    gsum = jnp.sum(g * gate_ref[blk, 0, pl.ds(j, 1), :], axis=1, keepdims=True)
                    gb_ref[pl.ds(r, w), :] += jnp.broadcast_to(gsum, (w, LANES))

            if small == win:
                run(win)
            else:
                fits = ranges[0][1] <= _win_start(ranges[0][0], cap, small) + small
                for off, end in ranges[1:]:
                    fits = fits & (end <= _win_start(off, cap, small) + small)
                pl.when(fits)(functools.partial(run, small))
                pl.when(jnp.logical_not(fits))(functools.partial(run, win))
            return carry

        if nblk == trip:
            gather(0, 0)
        else:
            lax.fori_loop(0, nblk // trip, gather, 0)

    def ffn(m):
        x = xg_ref[0:m, :].astype(BF16)
        ff = wgb_ref.shape[1]
        fc = min(512, ff)
        acc = jnp.zeros((m, wdb_ref.shape[1]), F32)
        for c in range(ff // fc):
            a = _dot(x, wgb_ref[:, c * fc:(c + 1) * fc])
            up = _dot(x, wub_ref[:, c * fc:(c + 1) * fc])
            acc = acc + _dot((_silu(a) * up).astype(BF16), wdb_ref[c * fc:(c + 1) * fc, :])
        return (acc * gb_ref[0:m, 0:1]).astype(BF16)

    if batch_ffn:
        load_weights()
        for bb in range(nbatch):
            gather_sample(bb, bb, bb * cap)
        y = ffn(nbatch * cap)
        for bb in range(nbatch):
            y_ref[bb, 0] = y[bb * cap:(bb + 1) * cap]
    else:
        b = pl.program_id(1)
        pl.when(b == 0)(load_weights)
        gather_sample(b, 0, 0)
        y_ref[0, 0] = ffn(cap)


def _moe_ffn(offs, u, dest4, gate4, wg, wu, wd, layer, cap):
    b, n, d = u.shape
    nblk = n // TOK_BLK
    win = min(TOK_BLK + BF16_ROWS, cap)
    ff = wg.shape[3]
    batch_ffn = cap < 256
    if batch_ffn:
        grid = (N_EXPERTS,)
        rows = b * cap
        tok_spec = pl.BlockSpec((b, n, d), lambda e, o: (0, 0, 0))
        sel_spec = pl.BlockSpec((b, 1, nblk, TOK_BLK), lambda e, o: (0, e, 0, 0))
        y_spec = pl.BlockSpec((b, 1, cap, d), lambda e, o: (0, e, 0, 0))

        def w_spec(r, c):
            return pl.BlockSpec((1, 1, r, c), lambda e, o: (layer, e, 0, 0))
    else:
        grid = (N_EXPERTS, b)
        rows = cap
        tok_spec = pl.BlockSpec((1, n, d), lambda e, i, o: (i, 0, 0))
        sel_spec = pl.BlockSpec((1, 1, nblk, TOK_BLK), lambda e, i, o: (i, e, 0, 0))
        y_spec = pl.BlockSpec((1, 1, cap, d), lambda e, i, o: (i, e, 0, 0))

        def w_spec(r, c):
            return pl.BlockSpec((1, 1, r, c), lambda e, i, o: (layer, e, 0, 0))
    return pl.pallas_call(
        functools.partial(_moe_ffn_kernel, cap=cap, nblk=nblk, win=win, small=min(SMALL_WIN, cap), nbatch=b,
                          batch_ffn=batch_ffn),
        grid_spec=pltpu.PrefetchScalarGridSpec(
            num_scalar_prefetch=1,
            grid=grid,
            in_specs=[tok_spec, sel_spec, sel_spec, w_spec(d, ff), w_spec(d, ff), w_spec(ff, d)],
            out_specs=y_spec,
            scratch_shapes=[pltpu.VMEM((rows, d), F32), pltpu.VMEM((rows, LANES), F32),
                            pltpu.VMEM((d, ff), BF16), pltpu.VMEM((d, ff), BF16), pltpu.VMEM((ff, d), BF16)]),
        out_shape=jax.ShapeDtypeStruct((b, N_EXPERTS, cap, d), BF16),
        compiler_params=_cparams(("arbitrary",) * len(grid)),
        name="moe_ffn",
    )(offs, u, dest4, gate4, wg, wu, wd)


def _moe_combine_kernel(offs_ref, y_ref, dest_ref, h_ref, mod_ref, out_ref, *, cap, nblk, win, small, nsub):
    b = pl.program_id(0)
    jb = pl.program_id(1)
    gate = mod_ref[0][5:6]
    for s in range(nsub):
        rows = slice(s * TOK_BLK, (s + 1) * TOK_BLK)
        ranges = [_blk_range(offs_ref, b, jb * nsub + s, e, nblk, cap) for e in range(N_EXPERTS)]

        def stacked(s=s, rows=rows, ranges=ranges):
            starts = [_win_start(off, cap, small) for off, _ in ranges]
            ycat = jnp.concatenate([y_ref[0, e, pl.ds(starts[e], small), :] for e in range(N_EXPERTS)], axis=0)
            gcat = jnp.concatenate([_onehot(dest_ref[0, e, s:s + 1, :], starts[e], small) for e in range(N_EXPERTS)], axis=0)
            out_ref[0, rows, :] = h_ref[0, rows, :] + gate * _dot_tn(gcat.astype(BF16), ycat)

        def per_expert(s=s, rows=rows, ranges=ranges):
            acc = jnp.zeros((TOK_BLK, out_ref.shape[2]), F32)
            for e in range(N_EXPERTS):
                st = _win_start(ranges[e][0], cap, win)
                g = _onehot(dest_ref[0, e, s:s + 1, :], st, win).astype(BF16)
                acc = acc + _dot_tn(g, y_ref[0, e, pl.ds(st, win), :])
            out_ref[0, rows, :] = h_ref[0, rows, :] + gate * acc

        if small == win:
            stacked()
        else:
            fits = ranges[0][1] <= _win_start(ranges[0][0], cap, small) + small
            for off, end in ranges[1:]:
                fits = fits & (end <= _win_start(off, cap, small) + small)
            pl.when(fits)(stacked)
            pl.when(jnp.logical_not(fits))(per_expert)


def _moe_combine(offs, y, dest4, h, mod, cap):
    b, n, d = h.shape
    nblk = n // TOK_BLK
    win = min(TOK_BLK + BF16_ROWS, cap)
    small = min(SMALL_WIN, cap)
    nsub = min(8, nblk)
    tb = nsub * TOK_BLK
    nb_mod = mod.shape[0]
    midx = (lambda i, j, o: (i, 0, 0)) if nb_mod > 1 else (lambda i, j, o: (0, 0, 0))
    return pl.pallas_call(
        functools.partial(_moe_combine_kernel, cap=cap, nblk=nblk, win=win, small=small, nsub=nsub),
        grid_spec=pltpu.PrefetchScalarGridSpec(
            num_scalar_prefetch=1,
            grid=(b, n // tb),
            in_specs=[pl.BlockSpec((1, N_EXPERTS, cap, d), lambda i, j, o: (i, 0, 0, 0), pipeline_mode=pl.Buffered(1)),
                      pl.BlockSpec((1, N_EXPERTS, nsub, TOK_BLK), lambda i, j, o: (i, 0, j, 0)),
                      pl.BlockSpec((1, tb, d), lambda i, j, o: (i, j, 0)),
                      pl.BlockSpec((1, 6, d), midx)],
            out_specs=pl.BlockSpec((1, tb, d), lambda i, j, o: (i, j, 0))),
        out_shape=jax.ShapeDtypeStruct((b, n, d), F32),
        compiler_params=_cparams(("arbitrary", "arbitrary")),
        name="moe_combine",
    )(offs, y, dest4, h, mod)


def _moe(h, u2, lg, mod, wg, wu, wd, layer):
    b, n, d = h.shape
    cap = EC_CAPACITY * n // N_EXPERTS
    nblk = n // TOK_BLK
    dest, gate, starts = _route(lg, cap)
    offs = starts[:, :, 0].reshape(-1)
    dest4 = dest.reshape(b, N_EXPERTS, nblk, TOK_BLK)
    gate4 = gate.reshape(b, N_EXPERTS, nblk, TOK_BLK)
    y = _moe_ffn(offs, u2, dest4, gate4, wg, wu, wd, layer, cap)
    return _moe_combine(offs, y, dest4, h, mod, cap)


def _gla_proj_kernel(x_ref, mod_ref, g1_ref, w_ref, wz_ref, wgate_ref, bgate_ref,
                     q_ref, k_ref, v_ref, og_ref, gf_ref, gb_ref, *, dk, dv):
    m = mod_ref[0]
    tm = x_ref.shape[1]
    half = tm // ROW_SPLIT if tm % (ROW_SPLIT * BF16_ROWS) == 0 else tm
    for r in range(tm // half):
        rows = slice(r * half, (r + 1) * half)
        u = _rms_mod(x_ref[0, rows, :], g1_ref[...], m[0:1], m[1:2]).astype(BF16)
        z = _dot(u, wz_ref[...]).astype(BF16)
        gp = _dot(z, wgate_ref[...]) + bgate_ref[...]
        g = (jnp.minimum(gp, 0.0) - jnp.log1p(jnp.exp(-jnp.abs(gp)))) * (1.0 / GLA_GATE_NORM)
        gf_ref[0, rows, :] = g[:, :dk]
        gb_ref[0, rows, :] = g[:, dk:]
        acc = _dot(u, w_ref[...])
        q_ref[0, rows, :] = acc[:, :dk]
        k_ref[0, rows, :] = acc[:, dk:2 * dk]
        v_ref[0, rows, :] = acc[:, 2 * dk:2 * dk + dv].astype(BF16)
        og_ref[0, rows, :] = acc[:, 2 * dk + dv:]


def _gla_proj(h, mod, g1, w, wz, wgate, bgate, dk, dv):
    b, t, d = h.shape
    tm = min(512, t)
    nb_mod = mod.shape[0]
    midx = (lambda i, j: (i, 0, 0)) if nb_mod > 1 else (lambda i, j: (0, 0, 0))
    cols = 2 * dk + 2 * dv
    widths = [(dk, F32), (dk, F32), (dv, BF16), (dv, F32), (dk, F32), (dk, F32)]
    return pl.pallas_call(
        functools.partial(_gla_proj_kernel, dk=dk, dv=dv),
        grid=(b, t // tm),
        in_specs=[pl.BlockSpec((1, tm, d), lambda i, j: (i, j, 0)),
                  pl.BlockSpec((1, 6, d), midx),
                  pl.BlockSpec((1, d), lambda i, j: (0, 0)),
                  pl.BlockSpec((d, cols), lambda i, j: (0, 0)),
                  pl.BlockSpec((d, LANES), lambda i, j: (0, 0)),
                  pl.BlockSpec((LANES, 2 * dk), lambda i, j: (0, 0)),
                  pl.BlockSpec((1, 2 * dk), lambda i, j: (0, 0))],
        out_specs=[pl.BlockSpec((1, tm, wd), lambda i, j: (i, j, 0)) for wd, _ in widths],
        out_shape=[jax.ShapeDtypeStruct((b, t, wd), dt) for wd, dt in widths],
        compiler_params=_cparams(("parallel", "parallel")),
        name="gla_proj",
    )(h, mod, g1.reshape(1, d), w, wz, wgate, bgate)


def _chunk_cumsum(x, c, reverse):
    rows = x.shape[0]
    pos = lax.broadcasted_iota(I32, x.shape, 0) & (c - 1)
    s = 1
    while s < c:
        if reverse:
            x = x + jnp.where(pos < c - s, pltpu.roll(x, rows - s, 0), 0.0)
        else:
            x = x + jnp.where(pos >= s, pltpu.roll(x, s, 0), 0.0)
        s *= 2
    return x


def _gla_scan(q_ref, k_ref, v_ref, g_ref, acc_ref, st_ref, *, n_rows, reverse, scale):
    c = GLA_CHUNK
    gr = min(GLA_GROUP * c, n_rows)
    cpg = gr // c
    gpi = min(GLA_GROUPS_PER_TRIP, n_rows // gr)
    rows_it = gpi * gr
    n_it = n_rows // rows_it
    ii = lax.broadcasted_iota(I32, (gr, gr), 0)
    jj = lax.broadcasted_iota(I32, (gr, gr), 1)
    sh = c.bit_length() - 1
    same = (ii >> sh) == (jj >> sh)
    tri = same & ((jj >= ii) if reverse else (jj <= ii))
    tri_b = jnp.where(tri, 1.0, 0.0).astype(BF16)
    g_order = range(gpi - 1, -1, -1) if reverse else range(gpi)
    c_order = range(cpg - 1, -1, -1) if reverse else range(cpg)

    def body(i, carry):
        r0 = 0 if n_it == 1 else pl.multiple_of(((n_it - 1 - i) if reverse else i) * rows_it, rows_it)
        q_all = q_ref[0, pl.ds(r0, rows_it), :]
        k_all = k_ref[0, pl.ds(r0, rows_it), :]
        v_all = v_ref[0, pl.ds(r0, rows_it), :]
        g_all = g_ref[0, pl.ds(r0, rows_it), :]
        prev = acc_ref[pl.ds(r0, rows_it), :] if reverse else None
        st = st_ref[...]
        outs = [None] * gpi
        for gi in g_order:
            rs = slice(gi * gr, (gi + 1) * gr)
            k = k_all[rs]
            v = v_all[rs]
            cum = _chunk_cumsum(g_all[rs], c, reverse)
            cl = [cum[ci * c:ci * c + 1] if reverse else cum[(ci + 1) * c - 1:(ci + 1) * c] for ci in range(cpg)]
            cl_rows = jnp.concatenate([jnp.broadcast_to(x, (c, x.shape[1])) for x in cl], axis=0)
            q_dec = (q_all[rs] * scale * jnp.exp(cum)).astype(BF16)
            k_inv = (k * jnp.exp(-cum)).astype(BF16)
            k_st = (k * jnp.exp(cl_rows - cum)).astype(BF16)
            a = jnp.where(tri, _dot_nt(q_dec, k_inv), 0.0)
            o_intra = _dot(a.astype(BF16), v)
            parts = [None] * cpg
            for ci in c_order:
                cs = slice(ci * c, (ci + 1) * c)
                parts[ci] = o_intra[cs] + _dot_nt(q_dec[cs], st.astype(BF16))
                st = st * jnp.exp(cl[ci]) + _dot_tn(v[cs], k_st[cs])
            outs[gi] = jnp.concatenate(parts, axis=0)
        st_ref[...] = st
        o_all = outs[0] if gpi == 1 else jnp.concatenate(outs, axis=0)
        acc_ref[pl.ds(r0, rows_it), :] = (prev + o_all) if reverse else o_all
        return carry

    if n_it == 1:
        body(0, 0)
    else:
        lax.fori_loop(0, n_it, body, 0)


def _gla_merge(acc_ref, og_ref, gain, o_ref, t):
    tm = min(512, t)
    for r in range(t // tm):
        o = acc_ref[r * tm:(r + 1) * tm, :]
        ms = jnp.mean(o * o, axis=-1, keepdims=True)
        y = (o * lax.rsqrt(ms + EPS) * gain) * _silu(og_ref[0, r * tm:(r + 1) * tm, :])
        o_ref[0, r * tm:(r + 1) * tm, :] = y.astype(BF16)


def _gla_kernel(*refs, n, l, dk, ctx_out):
    q_ref, k_ref, v_ref, gf_ref, gb_ref, og_ref, cq_ref, ck_ref, cv_ref, cgf_ref, cgb_ref, cog_ref, on_ref = refs[:13]
    if ctx_out:
        o_ref, co_ref, acc_ref, cacc_ref, st_ref = refs[13:]
    else:
        o_ref, acc_ref, cacc_ref, st_ref = refs[13:]
    scale = dk ** -0.5
    for reverse, g_ref, cg_ref in ((False, gf_ref, cgf_ref), (True, gb_ref, cgb_ref)):
        st_ref[...] = jnp.zeros_like(st_ref)
        _gla_scan(cq_ref, ck_ref, cv_ref, cg_ref, cacc_ref, st_ref, n_rows=l, reverse=reverse, scale=scale)
        _gla_scan(q_ref, k_ref, v_ref, g_ref, acc_ref, st_ref, n_rows=n, reverse=reverse, scale=scale)
    gain = on_ref[...]
    _gla_merge(acc_ref, og_ref, gain, o_ref, n)
    if ctx_out:
        _gla_merge(cacc_ref, cog_ref, gain, co_ref, l)


def _gla(lat, ctx, onorm, ctx_out):
    q, k, v, og, gf, gb = lat
    cq, ck, cv, cog, cgf, cgb = ctx
    b, n, dkt = q.shape
    l = cq.shape[1]
    dvt = v.shape[2]
    dk = dkt // GLA_HEADS
    dv = dvt // GLA_HEADS

    def spec(t, w):
        return pl.BlockSpec((1, t, w), lambda i, h: (i, 0, h))

    in_specs = [spec(n, dk), spec(n, dk), spec(n, dv), spec(n, dk), spec(n, dk), spec(n, dv),
                spec(l, dk), spec(l, dk), spec(l, dv), spec(l, dk), spec(l, dk), spec(l, dv),
                pl.BlockSpec((1, dv), lambda i, h: (0, 0))]
    out_specs = [spec(n, dv)]
    out_shape = [jax.ShapeDtypeStruct((b, n, dvt), BF16)]
    if ctx_out:
        out_specs.append(spec(l, dv))
        out_shape.append(jax.ShapeDtypeStruct((b, l, dvt), BF16))
    res = pl.pallas_call(
        functools.partial(_gla_kernel, n=n, l=l, dk=dk, ctx_out=ctx_out),
        grid=(b, GLA_HEADS),
        in_specs=in_specs,
        out_specs=out_specs,
        out_shape=out_shape,
        scratch_shapes=[pltpu.VMEM((n, dv), F32), pltpu.VMEM((l, dv), F32), pltpu.VMEM((dv, dk), F32)],
        compiler_params=_cparams(("parallel", "parallel")),
        name="gla",
    )(q, k, v, gf, gb, og, cq, ck, cv, cgf, cgb, cog, onorm.reshape(1, dv))
    return (res[0], res[1]) if ctx_out else (res[0], None)


def _rope_tables(n):
    half = HEAD_DIM // 2
    t = jnp.arange(n, dtype=I32)
    inv = ROPE_THETA ** (-jnp.arange(0, half, 2, dtype=F32) / half)

    def tab(pos):
        ang = pos.astype(F32)[:, None] * inv[None, :]
        ang = jnp.concatenate([ang, ang], axis=-1)
        return jnp.cos(ang), jnp.sin(ang)

    cos_r, sin_r = tab(t // GRID_W)
    cos_c, sin_c = tab(t % GRID_W)
    cos = jnp.concatenate([cos_r, cos_c], axis=-1)
    sin = jnp.concatenate([sin_r, sin_c], axis=-1)
    sign = jnp.where((jnp.arange(HEAD_DIM) % half) < half // 2, -1.0, 1.0).astype(F32)
    reps = LANES // HEAD_DIM
    return jnp.tile(cos, (1, reps)), jnp.tile(sin * sign[None, :], (1, reps))


def _block_ones():
    i = np.arange(LANES)
    return jnp.asarray((i[:, None] // HEAD_DIM) == (i[None, :] // HEAD_DIM), BF16)


def _attn_gains(qa_g, ka_g, qb_g, kb_g):
    qs = HEAD_DIM ** -0.5
    one_a = jnp.ones((A_KV_HEADS * HEAD_DIM,), F32)
    one_b = jnp.ones((B_HEADS * HEAD_DIM,), F32)
    return jnp.concatenate([jnp.tile(qa_g, A_HEADS) * qs, jnp.tile(ka_g, A_KV_HEADS), one_a,
                            jnp.tile(qb_g, B_HEADS) * qs, jnp.tile(kb_g, B_HEADS), one_b]).reshape(1, -1)


def kernel(x, c, ctx, c_ctx, ada_w, ada_b, norm1_g, norm2_g, attn_w_in, attn_w_out, a_q_norm, a_k_norm, b_q_norm,
           b_k_norm, na_rpb, gla_w_in, gla_gk_w_f, gla_gk_b_f, gla_gk_w_b, gla_gk_b_b, gla_o_norm, gla_w_out,
           moe_router, moe_w_gate, moe_w_up, moe_w_down):
    b, n, d = x.shape
    depth = ada_w.shape[0]
    dk = gla_gk_w_f.shape[2]
    dv = gla_w_out.shape[1]
    rows = -(-(b + 1) // 8) * 8
    cvec = jnp.zeros((rows, d), F32).at[:b].set(c).at[b].set(c_ctx)
    mods = _mods(cvec, ada_w, ada_b)
    cos, sin = _rope_tables(n)
    bd = _block_ones()
    h, hc = x, ctx
    for layer in range(depth):
        last = layer == depth - 1
        i = layer // 2
        ml = mods[layer, :b].reshape(b, 6, d)
        mc = mods[layer, b:b + 1].reshape(1, 6, d)
        if layer % 2 == 0:
            w_in = attn_w_in[i].astype(BF16)
            gains = _attn_gains(a_q_norm[i], a_k_norm[i], b_q_norm[i], b_k_norm[i])
            qa, ka, va, qb, kb, vb = _attn_proj(h, ml, norm1_g[layer], w_in, gains, bd, cos, sin)
            cqa, cka, cva, cqb, ckb, cvb = _attn_proj(hc, mc, norm1_g[layer], w_in, gains, bd)
            oa = _gqa(qa, jnp.concatenate([ka, cka], axis=1), jnp.concatenate([va, cva], axis=1), A_HEADS, A_KV_HEADS)
            ob = _na(qb, kb, vb, ckb, cvb, _na_bias_table(na_rpb[i], n // GRID_W))
            o_lat = [oa, ob]
            o_ctx = None if last else [_gqa(cqa, cka, cva, A_HEADS, A_KV_HEADS), _gqa(cqb, ckb, cvb, B_HEADS, B_HEADS)]
            w_out = attn_w_out[i].astype(BF16)
        else:
            w_main = gla_w_in[i][:, :2 * dk + 2 * dv].astype(BF16)
            wz = jnp.pad(gla_w_in[i][:, 2 * dk + 2 * dv:], ((0, 0), (0, LANES - 2 * GLA_GATE_RANK))).astype(BF16)
            wgate = jnp.zeros((LANES, 2 * dk), F32)
            wgate = wgate.at[:GLA_GATE_RANK, :dk].set(gla_gk_w_f[i]).at[GLA_GATE_RANK:2 * GLA_GATE_RANK, dk:].set(gla_gk_w_b[i])
            bgate = jnp.concatenate([gla_gk_b_f[i], gla_gk_b_b[i]]).reshape(1, 2 * dk)
            lat = _gla_proj(h, ml, norm1_g[layer], w_main, wz, wgate.astype(BF16), bgate, dk, dv)
            cx = _gla_proj(hc, mc, norm1_g[layer], w_main, wz, wgate.astype(BF16), bgate, dk, dv)
            o, co = _gla(lat, cx, gla_o_norm[i], not last)
            o_lat = [o]
            o_ctx = None if last else [co]
            w_out = gla_w_out[i].astype(BF16)
        wr_hi, wr_lo = _split2(moe_router[layer].T)
        h1, u2, lg = _outproj(o_lat, h, ml, norm2_g[layer], w_out, wr_hi, wr_lo)
        h = _moe(h1, u2, lg, ml, moe_w_gate, moe_w_up, moe_w_down, layer)
        if not last:
            hc1, uc2, lgc = _outproj(o_ctx, hc, mc, norm2_g[layer], w_out, wr_hi, wr_lo)
            hc = _moe(hc1, uc2, lgc, mc, moe_w_gate, moe_w_up, moe_w_down, layer)
    return h
```

```python
import functools

import numpy as np
import jax
import jax.numpy as jnp
from jax import lax
from jax.experimental import pallas as pl
from jax.experimental.pallas import tpu as pltpu

F32 = jnp.float32
BF16 = jnp.bfloat16
I32 = jnp.int32

GRID_W = 64
HEAD_DIM = 64
A_HEADS = 8
A_KV_HEADS = 2
B_HEADS = 8
NA_ROWS = 8
NA_COLS = 16
ROPE_THETA = 10000.0
GLA_HEADS = 4
GLA_GATE_RANK = 16
GLA_GATE_NORM = 16.0
GLA_CHUNK = 64
GLA_GROUP = 4
GLA_GROUPS_PER_TRIP = 8
N_EXPERTS = 16
EC_CAPACITY = 2
EPS = 1e-6

LANES = 128
BF16_ROWS = 16
NA_QROWS = 4
TOK_BLK = LANES
SMALL_WIN = 48
GATHER_TRIP = 8
MASK_VALUE = -1e30
VMEM_LIMIT = 56 * 1024 * 1024

_NT = (((1,), (1,)), ((), ()))
_TN = (((0,), (0,)), ((), ()))


def _cparams(sem):
    return pltpu.CompilerParams(dimension_semantics=sem, vmem_limit_bytes=VMEM_LIMIT)


def _dot(a, b):
    return jnp.dot(a, b, preferred_element_type=F32)


def _dot_nt(a, b):
    return lax.dot_general(a, b, _NT, preferred_element_type=F32)


def _dot_tn(a, b):
    return lax.dot_general(a, b, _TN, preferred_element_type=F32)


def _split2(a):
    hi = a.astype(BF16)
    lo = (a - hi.astype(F32)).astype(BF16)
    return hi, lo


def _silu(a):
    return a / (1.0 + jnp.exp(-a))


def _rms_mod(x, g, shift, scale):
    ms = jnp.mean(x * x, axis=-1, keepdims=True)
    return (x * lax.rsqrt(ms + EPS) * g) * (1.0 + scale) + shift


def _mods_kernel(c_ref, w_ref, b_ref, o_ref):
    c = c_ref[...]
    s_hi, s_lo = _split2(_silu(c))
    w_hi, w_lo = _split2(w_ref[0])
    o_ref[0] = _dot(s_hi, w_hi) + _dot(s_lo, w_hi) + _dot(s_hi, w_lo) + b_ref[0]


def _mods(cvec, ada_w, ada_b):
    depth, d, d6 = ada_w.shape
    rows = cvec.shape[0]
    tn = 1536 if d6 % 1536 == 0 else d6
    return pl.pallas_call(
        _mods_kernel,
        grid=(depth, d6 // tn),
        in_specs=[pl.BlockSpec((rows, d), lambda l, j: (0, 0)),
                  pl.BlockSpec((1, d, tn), lambda l, j: (l, 0, j)),
                  pl.BlockSpec((1, 1, tn), lambda l, j: (l, 0, j))],
        out_specs=pl.BlockSpec((1, rows, tn), lambda l, j: (l, 0, j)),
        out_shape=jax.ShapeDtypeStruct((depth, rows, d6), F32),
        compiler_params=_cparams(("arbitrary", "arbitrary")),
        name="mods",
    )(cvec, ada_w, ada_b.reshape(depth, 1, d6))


_ATTN_COLS = (A_HEADS + 2 * A_KV_HEADS + 3 * B_HEADS) * HEAD_DIM
_QA = (0, 4)
_KA = (4, 5)
_VA = (5, 6)
_QB = (6, 10)
_KB = (10, 14)
_VB = (14, 18)


def _attn_proj_kernel(*refs, rope):
    if rope:
        x_ref, mod_ref, g1_ref, w_ref, gain_ref, bd_ref, cos_ref, sin_ref = refs[:8]
        outs = refs[8:]
    else:
        x_ref, mod_ref, g1_ref, w_ref, gain_ref, bd_ref = refs[:6]
        outs = refs[6:]
    qa_ref, ka_ref, va_ref, qb_ref, kb_ref, vb_ref = outs
    m = mod_ref[0]
    u = _rms_mod(x_ref[0], g1_ref[...], m[0:1], m[1:2]).astype(BF16)
    acc = _dot(u, w_ref[...])
    bd = bd_ref[...]
    if rope:
        cos = cos_ref[...]
        sin = sin_ref[...]
        lane = lax.broadcasted_iota(I32, cos.shape, 1)
        first = (lane & 31) < 16

    def chunk(j, norm, rot):
        cch = acc[:, j * LANES:(j + 1) * LANES]
        if norm:
            sq_hi, sq_lo = _split2(cch * cch)
            ss = _dot(sq_hi, bd) + _dot(sq_lo, bd)
            cch = cch * lax.rsqrt(ss * (1.0 / HEAD_DIM) + EPS) * gain_ref[:, j * LANES:(j + 1) * LANES]
        if rot:
            partner = jnp.where(first, pltpu.roll(cch, LANES - 16, 1), pltpu.roll(cch, 16, 1))
            cch = cch * cos + partner * sin
        return cch.astype(BF16)

    for (lo, hi), ref, norm, rot in ((_QA, qa_ref, True, rope), (_KA, ka_ref, True, rope), (_VA, va_ref, False, False),
                                     (_QB, qb_ref, True, False), (_KB, kb_ref, True, False), (_VB, vb_ref, False, False)):
        for j in range(lo, hi):
            ref[0, :, (j - lo) * LANES:(j - lo + 1) * LANES] = chunk(j, norm, rot)


def _attn_proj(h, mod, g1, w, gains, bd, cos=None, sin=None):
    b, t, d = h.shape
    tm = min(512, t)
    rope = cos is not None
    nb_mod = mod.shape[0]
    midx = (lambda i, j: (i, 0, 0)) if nb_mod > 1 else (lambda i, j: (0, 0, 0))
    in_specs = [pl.BlockSpec((1, tm, d), lambda i, j: (i, j, 0)),
                pl.BlockSpec((1, 6, d), midx),
                pl.BlockSpec((1, d), lambda i, j: (0, 0)),
                pl.BlockSpec((d, _ATTN_COLS), lambda i, j: (0, 0)),
                pl.BlockSpec((1, _ATTN_COLS), lambda i, j: (0, 0)),
                pl.BlockSpec((LANES, LANES), lambda i, j: (0, 0))]
    args = [h, mod, g1.reshape(1, d), w, gains, bd]
    if rope:
        in_specs += [pl.BlockSpec((tm, LANES), lambda i, j: (j, 0))] * 2
        args += [cos, sin]
    widths = [(hi - lo) * LANES for lo, hi in (_QA, _KA, _VA, _QB, _KB, _VB)]
    return pl.pallas_call(
        functools.partial(_attn_proj_kernel, rope=rope),
        grid=(b, t // tm),
        in_specs=in_specs,
        out_specs=[pl.BlockSpec((1, tm, wd), lambda i, j: (i, j, 0)) for wd in widths],
        out_shape=[jax.ShapeDtypeStruct((b, t, wd), BF16) for wd in widths],
        compiler_params=_cparams(("parallel", "parallel")),
        name="attn_proj_rope" if rope else "attn_proj",
    )(*args)


def _gqa_kernel(q_ref, k_ref, v_ref, o_ref, *, hq, hk):
    grp = hq // hk
    for h in range(hq):
        kv = h // grp
        q = q_ref[0, :, h * HEAD_DIM:(h + 1) * HEAD_DIM]
        k = k_ref[0, :, kv * HEAD_DIM:(kv + 1) * HEAD_DIM]
        v = v_ref[0, :, kv * HEAD_DIM:(kv + 1) * HEAD_DIM]
        s = _dot_nt(q, k)
        p = jnp.exp(s - jnp.max(s, axis=-1, keepdims=True))
        l = jnp.sum(p, axis=-1, keepdims=True)
        o = _dot(p.astype(BF16), v) / l
        o_ref[0, :, h * HEAD_DIM:(h + 1) * HEAD_DIM] = o.astype(BF16)


def _gqa(q, k, v, hq, hk):
    b, t, _ = q.shape
    s = k.shape[1]
    tq = min(512, t)
    return pl.pallas_call(
        functools.partial(_gqa_kernel, hq=hq, hk=hk),
        grid=(b, t // tq),
        in_specs=[pl.BlockSpec((1, tq, hq * HEAD_DIM), lambda i, j: (i, j, 0)),
                  pl.BlockSpec((1, s, hk * HEAD_DIM), lambda i, j: (i, 0, 0)),
                  pl.BlockSpec((1, s, hk * HEAD_DIM), lambda i, j: (i, 0, 0))],
        out_specs=pl.BlockSpec((1, tq, hq * HEAD_DIM), lambda i, j: (i, j, 0)),
        out_shape=jax.ShapeDtypeStruct((b, t, hq * HEAD_DIM), BF16),
        compiler_params=_cparams(("parallel", "parallel")),
        name="gqa",
    )(q, k, v)


def _na_geometry(rows):
    kr = min(NA_QROWS + NA_ROWS - 1, rows)
    nblk = rows // NA_QROWS
    return kr, nblk


def _na_bias_table(rpb, rows):
    kr, nblk = _na_geometry(rows)
    wr = min(NA_ROWS, rows)
    heads = rpb.shape[0]
    qc = np.arange(GRID_W)[:, None]
    kc = np.arange(GRID_W)[None, :]
    cs = np.clip(qc - NA_COLS // 2, 0, GRID_W - NA_COLS)
    col_ok = (kc >= cs) & (kc < cs + NA_COLS)
    pick = (np.arange(2 * NA_COLS - 1)[:, None, None] == (kc - qc + NA_COLS - 1)[None]) & col_ok[None]
    toeplitz = jnp.einsum("hrd,dqk->hrqk", rpb.astype(F32), jnp.asarray(pick, F32), precision=lax.Precision.HIGHEST)
    toeplitz = jnp.where(col_ok[None, None], toeplitz, MASK_VALUE)
    tabs = []
    for rb in (0, min(1, nblk - 1), nblk - 1):
        qr0 = rb * NA_QROWS
        kr0 = int(np.clip(qr0 - NA_ROWS // 2, 0, rows - kr))
        qr = qr0 + np.arange(NA_QROWS)[:, None]
        krr = kr0 + np.arange(kr)[None, :]
        rs = np.clip(qr - wr // 2, 0, rows - wr)
        row_ok = (krr >= rs) & (krr < rs + wr)
        dr = krr - qr + (NA_ROWS - 1)
        masked = jnp.full((heads, GRID_W, GRID_W), MASK_VALUE, F32)
        blocks = [jnp.concatenate([toeplitz[:, dr[i, j]] if row_ok[i, j] else masked for j in range(kr)], axis=2)
                  for i in range(NA_QROWS)]
        tabs.append(jnp.concatenate(blocks, axis=1))
    return jnp.stack(tabs)


def _na_kernel(q_ref, k_ref, v_ref, kc_ref, vc_ref, bias_ref, o_ref, *, rows, heads):
    kr, _ = _na_geometry(rows)
    rb = pl.program_id(1)
    kr0 = jnp.clip(rb * NA_QROWS - NA_ROWS // 2, 0, rows - kr)
    start = pl.multiple_of(kr0 * GRID_W, GRID_W)
    kw = k_ref[0, pl.ds(start, kr * GRID_W), :]
    vw = v_ref[0, pl.ds(start, kr * GRID_W), :]
    for h in range(heads):
        sl = slice(h * HEAD_DIM, (h + 1) * HEAD_DIM)
        q = q_ref[0, :, sl]
        s_w = _dot_nt(q, kw[:, sl]) + bias_ref[0, h]
        s_c = _dot_nt(q, kc_ref[0, :, sl])
        m = jnp.maximum(jnp.max(s_w, axis=-1, keepdims=True), jnp.max(s_c, axis=-1, keepdims=True))
        p_w = jnp.exp(s_w - m)
        p_c = jnp.exp(s_c - m)
        l = jnp.sum(p_w, axis=-1, keepdims=True) + jnp.sum(p_c, axis=-1, keepdims=True)
        o = (_dot(p_w.astype(BF16), vw[:, sl]) + _dot(p_c.astype(BF16), vc_ref[0, :, sl])) / l
        o_ref[0, :, sl] = o.astype(BF16)


def _na(q, k, v, kc, vc, bias):
    b, n, c = q.shape
    l = kc.shape[1]
    rows = n // GRID_W
    kr, nblk = _na_geometry(rows)
    tq = NA_QROWS * GRID_W
    heads = c // HEAD_DIM

    def pattern(i, j):
        return (jnp.where(j == 0, 0, jnp.where(j == nblk - 1, 2, 1)), 0, 0, 0)

    return pl.pallas_call(
        functools.partial(_na_kernel, rows=rows, heads=heads),
        grid=(b, nblk),
        in_specs=[pl.BlockSpec((1, tq, c), lambda i, j: (i, j, 0)),
                  pl.BlockSpec((1, n, c), lambda i, j: (i, 0, 0)),
                  pl.BlockSpec((1, n, c), lambda i, j: (i, 0, 0)),
                  pl.BlockSpec((1, l, c), lambda i, j: (i, 0, 0)),
                  pl.BlockSpec((1, l, c), lambda i, j: (i, 0, 0)),
                  pl.BlockSpec((1, heads, tq, kr * GRID_W), pattern)],
        out_specs=pl.BlockSpec((1, tq, c), lambda i, j: (i, j, 0)),
        out_shape=jax.ShapeDtypeStruct((b, n, c), BF16),
        compiler_params=_cparams(("parallel", "arbitrary")),
        name="na_attn",
    )(q, k, v, kc, vc, bias)


def _outproj_kernel(*refs, n_o, route):
    o_refs = refs[:n_o]
    h_ref, mod_ref, g2_ref, w_ref = refs[n_o:n_o + 4]
    rest = refs[n_o + 4:]
    if route:
        wr_hi_ref, wr_lo_ref, hn_ref, u2_ref, lg_ref = rest
    else:
        hn_ref, = rest
    m = mod_ref[0]
    o = o_refs[0][0] if n_o == 1 else jnp.concatenate([r[0] for r in o_refs], axis=1)
    hn = h_ref[0] + m[2:3] * _dot(o, w_ref[...])
    hn_ref[0] = hn
    if route:
        u2 = _rms_mod(hn, g2_ref[...], m[3:4], m[4:5])
        u_hi, u_lo = _split2(u2)
        u2_ref[0] = u_hi
        wr_hi = wr_hi_ref[...]
        both = _dot_nt(jnp.concatenate([wr_hi, wr_lo_ref[...]], axis=0), u_hi)
        lg_ref[0] = both[:N_EXPERTS] + both[N_EXPERTS:] + _dot_nt(wr_hi, u_lo)


def _outproj(o_list, h, mod, g2, w, wr_hi=None, wr_lo=None):
    b, t, d = h.shape
    tm = min(512, t)
    route = wr_hi is not None
    nb_mod = mod.shape[0]
    midx = (lambda i, j: (i, 0, 0)) if nb_mod > 1 else (lambda i, j: (0, 0, 0))
    in_specs = [pl.BlockSpec((1, tm, o.shape[2]), lambda i, j: (i, j, 0)) for o in o_list]
    in_specs += [pl.BlockSpec((1, tm, d), lambda i, j: (i, j, 0)),
                 pl.BlockSpec((1, 6, d), midx),
                 pl.BlockSpec((1, d), lambda i, j: (0, 0)),
                 pl.BlockSpec((d, d), lambda i, j: (0, 0))]
    args = list(o_list) + [h, mod, g2.reshape(1, d), w]
    out_specs = [pl.BlockSpec((1, tm, d), lambda i, j: (i, j, 0))]
    out_shape = [jax.ShapeDtypeStruct((b, t, d), F32)]
    if route:
        in_specs += [pl.BlockSpec((N_EXPERTS, d), lambda i, j: (0, 0))] * 2
        args += [wr_hi, wr_lo]
        out_specs += [pl.BlockSpec((1, tm, d), lambda i, j: (i, j, 0)),
                      pl.BlockSpec((1, N_EXPERTS, tm), lambda i, j: (i, 0, j))]
        out_shape += [jax.ShapeDtypeStruct((b, t, d), BF16), jax.ShapeDtypeStruct((b, N_EXPERTS, t), F32)]
    return pl.pallas_call(
        functools.partial(_outproj_kernel, n_o=len(o_list), route=route),
        grid=(b, t // tm),
        in_specs=in_specs,
        out_specs=out_specs,
        out_shape=out_shape,
        compiler_params=_cparams(("parallel", "parallel")),
        name="outproj",
    )(*args)


def _cumsum_excl(x):
    e, n = x.shape
    nb = n // LANES
    st = jnp.concatenate([x[:, j * LANES:(j + 1) * LANES] for j in range(nb)], axis=0)
    ii = lax.broadcasted_iota(I32, (LANES, LANES), 0)
    jj = lax.broadcasted_iota(I32, (LANES, LANES), 1)
    incl = _dot(st.astype(BF16), jnp.where(ii <= jj, 1.0, 0.0).astype(BF16))
    tot = jnp.broadcast_to(incl[:, LANES - 1:LANES], incl.shape)
    r = lax.broadcasted_iota(I32, (nb * e, nb * e), 0)
    c = lax.broadcasted_iota(I32, (nb * e, nb * e), 1)
    sh = e.bit_length() - 1
    same = (r & (e - 1)) == (c & (e - 1))
    prev = jnp.where(same & (lax.shift_right_logical(c, sh) < lax.shift_right_logical(r, sh)), 1.0, 0.0).astype(BF16)
    starts = _dot(prev, tot.astype(BF16))
    excl_st = incl - st + starts
    excl = jnp.concatenate([excl_st[j * e:(j + 1) * e, :] for j in range(nb)], axis=1)
    return excl, starts


def _route_kernel(lg_ref, dest_ref, gate_ref, offs_ref, *, cap):
    x = lg_ref[0]
    ex = jnp.exp(x - jnp.max(x, axis=0, keepdims=True))
    aff = ex / jnp.sum(ex, axis=0, keepdims=True)
    bits = lax.bitcast_convert_type(aff, I32)

    def search(i, cur):
        cand = cur | lax.shift_left(jnp.int32(1), 30 - i)
        cnt = jnp.sum(jnp.where(bits >= cand, 1.0, 0.0), axis=1, keepdims=True)
        return jnp.where(cnt >= cap, cand, cur)

    thr = lax.fori_loop(0, 31, search, jnp.zeros((x.shape[0], 1), I32))
    gt = bits > thr
    eq = bits == thr
    need = cap - jnp.sum(jnp.where(gt, 1.0, 0.0), axis=1, keepdims=True)
    rank_eq, _ = _cumsum_excl(jnp.where(eq, 1.0, 0.0))
    sel = gt | (eq & (rank_eq < need))
    pos, starts = _cumsum_excl(jnp.where(sel, 1.0, 0.0))
    dest_ref[0] = jnp.where(sel, pos.astype(I32), -1)
    gate_ref[0] = jnp.where(sel, aff, 0.0)
    offs_ref[0] = starts.astype(I32)


def _route(lg, cap):
    b, e, n = lg.shape
    nb = n // LANES
    return pl.pallas_call(
        functools.partial(_route_kernel, cap=cap),
        grid=(b,),
        in_specs=[pl.BlockSpec((1, e, n), lambda i: (i, 0, 0))],
        out_specs=[pl.BlockSpec((1, e, n), lambda i: (i, 0, 0)),
                   pl.BlockSpec((1, e, n), lambda i: (i, 0, 0)),
                   pl.BlockSpec((1, nb * e, LANES), lambda i: (i, 0, 0))],
        out_shape=[jax.ShapeDtypeStruct((b, e, n), I32), jax.ShapeDtypeStruct((b, e, n), F32),
                   jax.ShapeDtypeStruct((b, nb * e, LANES), I32)],
        compiler_params=_cparams(("parallel",)),
        name="route",
    )(lg)


def _blk_range(offs_ref, b, j, e, nblk, cap):
    base = (b * nblk + j) * N_EXPERTS + e
    last = (b * nblk + nblk - 1) * N_EXPERTS + e
    return offs_ref[base], jnp.where(j + 1 < nblk, offs_ref[jnp.minimum(base + N_EXPERTS, last)], cap)


def _win_start(off, cap, win):
    return pl.multiple_of(jnp.minimum(lax.shift_right_logical(off, 4) * BF16_ROWS, cap - win), BF16_ROWS)


def _onehot(dest_row, off_al, win):
    rows = lax.broadcasted_iota(I32, (win, TOK_BLK), 0) + off_al
    return jnp.where(rows == dest_row, 1.0, 0.0)


def _moe_ffn_kernel(offs_ref, u_ref, dest_ref, gate_ref, wg_ref, wu_ref, wd_ref, y_ref,
                    xg_ref, gb_ref, wgb_ref, wub_ref, wdb_ref, *, cap, nblk, win, small, nbatch, batch_ffn):
    e = pl.program_id(0)

    def load_weights():
        wgb_ref[...] = wg_ref[0, 0].astype(BF16)
        wub_ref[...] = wu_ref[0, 0].astype(BF16)
        wdb_ref[...] = wd_ref[0, 0].astype(BF16)

    trip = min(GATHER_TRIP, nblk)

    def gather_sample(b, blk, row0):
        xg_ref[pl.ds(row0, cap), :] = jnp.zeros((cap, xg_ref.shape[1]), F32)
        gb_ref[pl.ds(row0, cap), :] = jnp.zeros((cap, LANES), F32)

        def gather(t, carry):
            js = [t * trip + i for i in range(trip)]
            ranges = [_blk_range(offs_ref, b, j, e, nblk, cap) for j in js]

            def run(w):
                for j, (off, _) in zip(js, ranges):
                    st = _win_start(off, cap, w)
                    g = _onehot(dest_ref[blk, 0, pl.ds(j, 1), :], st, w)
                    tok0 = j * TOK_BLK if isinstance(j, int) else pl.multiple_of(j * TOK_BLK, TOK_BLK)
                    uj = u_ref[blk, pl.ds(tok0, TOK_BLK), :]
                    r = pl.multiple_of(row0 + st, BF16_ROWS)
                    xg_ref[pl.ds(r, w), :] += _dot(g.astype(BF16), uj)
                    gsum = jnp.sum(g * gate_ref[blk, 0, pl.ds(j, 1), :], axis=1, keepdims=True)
                    gb_ref[pl.ds(r, w), :] += jnp.broadcast_to(gsum, (w, LANES))

            if small == win:
                run(win)
            else:
                fits = ranges[0][1] <= _win_start(ranges[0][0], cap, small) + small
                for off, end in ranges[1:]:
                    fits = fits & (end <= _win_start(off, cap, small) + small)
                pl.when(fits)(functools.partial(run, small))
                pl.when(jnp.logical_not(fits))(functools.partial(run, win))
            return carry

        if nblk == trip:
            gather(0, 0)
        else:
            lax.fori_loop(0, nblk // trip, gather, 0)

    def ffn(m):
        x = xg_ref[0:m, :].astype(BF16)
        ff = wgb_ref.shape[1]
        fc = min(512, ff)
        acc = jnp.zeros((m, wdb_ref.shape[1]), F32)
        for c in range(ff // fc):
            a = _dot(x, wgb_ref[:, c * fc:(c + 1) * fc])
            up = _dot(x, wub_ref[:, c * fc:(c + 1) * fc])
            acc = acc + _dot((_silu(a) * up).astype(BF16), wdb_ref[c * fc:(c + 1) * fc, :])
        return (acc * gb_ref[0:m, 0:1]).astype(BF16)

    if batch_ffn:
        load_weights()
        for bb in range(nbatch):
            gather_sample(bb, bb, bb * cap)
        y = ffn(nbatch * cap)
        for bb in range(nbatch):
            y_ref[bb, 0] = y[bb * cap:(bb + 1) * cap]
    else:
        b = pl.program_id(1)
        pl.when(b == 0)(load_weights)
        gather_sample(b, 0, 0)
        y_ref[0, 0] = ffn(cap)


def _moe_ffn(offs, u, dest4, gate4, wg, wu, wd, layer, cap):
    b, n, d = u.shape
    nblk = n // TOK_BLK
    win = min(TOK_BLK + BF16_ROWS, cap)
    ff = wg.shape[3]
    batch_ffn = cap < 256
    if batch_ffn:
        grid = (N_EXPERTS,)
        rows = b * cap
        tok_spec = pl.BlockSpec((b, n, d), lambda e, o: (0, 0, 0))
        sel_spec = pl.BlockSpec((b, 1, nblk, TOK_BLK), lambda e, o: (0, e, 0, 0))
        y_spec = pl.BlockSpec((b, 1, cap, d), lambda e, o: (0, e, 0, 0))

        def w_spec(r, c):
            return pl.BlockSpec((1, 1, r, c), lambda e, o: (layer, e, 0, 0))
    else:
        grid = (N_EXPERTS, b)
        rows = cap
        tok_spec = pl.BlockSpec((1, n, d), lambda e, i, o: (i, 0, 0))
        sel_spec = pl.BlockSpec((1, 1, nblk, TOK_BLK), lambda e, i, o: (i, e, 0, 0))
        y_spec = pl.BlockSpec((1, 1, cap, d), lambda e, i, o: (i, e, 0, 0))

        def w_spec(r, c):
            return pl.BlockSpec((1, 1, r, c), lambda e, i, o: (layer, e, 0, 0))
    return pl.pallas_call(
        functools.partial(_moe_ffn_kernel, cap=cap, nblk=nblk, win=win, small=min(SMALL_WIN, cap), nbatch=b,
                          batch_ffn=batch_ffn),
        grid_spec=pltpu.PrefetchScalarGridSpec(
            num_scalar_prefetch=1,
            grid=grid,
            in_specs=[tok_spec, sel_spec, sel_spec, w_spec(d, ff), w_spec(d, ff), w_spec(ff, d)],
            out_specs=y_spec,
            scratch_shapes=[pltpu.VMEM((rows, d), F32), pltpu.VMEM((rows, LANES), F32),
                            pltpu.VMEM((d, ff), BF16), pltpu.VMEM((d, ff), BF16), pltpu.VMEM((ff, d), BF16)]),
        out_shape=jax.ShapeDtypeStruct((b, N_EXPERTS, cap, d), BF16),
        compiler_params=_cparams(("arbitrary",) * len(grid)),
        name="moe_ffn",
    )(offs, u, dest4, gate4, wg, wu, wd)


def _moe_combine_kernel(offs_ref, y_ref, dest_ref, h_ref, mod_ref, out_ref, *, cap, nblk, win, small, nsub):
    b = pl.program_id(0)
    jb = pl.program_id(1)
    gate = mod_ref[0][5:6]
    for s in range(nsub):
        rows = slice(s * TOK_BLK, (s + 1) * TOK_BLK)
        ranges = [_blk_range(offs_ref, b, jb * nsub + s, e, nblk, cap) for e in range(N_EXPERTS)]

        def stacked(s=s, rows=rows, ranges=ranges):
            starts = [_win_start(off, cap, small) for off, _ in ranges]
            ycat = jnp.concatenate([y_ref[0, e, pl.ds(starts[e], small), :] for e in range(N_EXPERTS)], axis=0)
            gcat = jnp.concatenate([_onehot(dest_ref[0, e, s:s + 1, :], starts[e], small) for e in range(N_EXPERTS)], axis=0)
            out_ref[0, rows, :] = h_ref[0, rows, :] + gate * _dot_tn(gcat.astype(BF16), ycat)

        def per_expert(s=s, rows=rows, ranges=ranges):
            acc = jnp.zeros((TOK_BLK, out_ref.shape[2]), F32)
            for e in range(N_EXPERTS):
                st = _win_start(ranges[e][0], cap, win)
                g = _onehot(dest_ref[0, e, s:s + 1, :], st, win).astype(BF16)
                acc = acc + _dot_tn(g, y_ref[0, e, pl.ds(st, win), :])
            out_ref[0, rows, :] = h_ref[0, rows, :] + gate * acc

        if small == win:
            stacked()
        else:
            fits = ranges[0][1] <= _win_start(ranges[0][0], cap, small) + small
            for off, end in ranges[1:]:
                fits = fits & (end <= _win_start(off, cap, small) + small)
            pl.when(fits)(stacked)
            pl.when(jnp.logical_not(fits))(per_expert)


def _moe_combine(offs, y, dest4, h, mod, cap):
    b, n, d = h.shape
    nblk = n // TOK_BLK
    win = min(TOK_BLK + BF16_ROWS, cap)
    small = min(SMALL_WIN, cap)
    nsub = min(8, nblk)
    tb = nsub * TOK_BLK
    nb_mod = mod.shape[0]
    midx = (lambda i, j, o: (i, 0, 0)) if nb_mod > 1 else (lambda i, j, o: (0, 0, 0))
    return pl.pallas_call(
        functools.partial(_moe_combine_kernel, cap=cap, nblk=nblk, win=win, small=small, nsub=nsub),
        grid_spec=pltpu.PrefetchScalarGridSpec(
            num_scalar_prefetch=1,
            grid=(b, n // tb),
            in_specs=[pl.BlockSpec((1, N_EXPERTS, cap, d), lambda i, j, o: (i, 0, 0, 0), pipeline_mode=pl.Buffered(1)),
                      pl.BlockSpec((1, N_EXPERTS, nsub, TOK_BLK), lambda i, j, o: (i, 0, j, 0)),
                      pl.BlockSpec((1, tb, d), lambda i, j, o: (i, j, 0)),
                      pl.BlockSpec((1, 6, d), midx)],
            out_specs=pl.BlockSpec((1, tb, d), lambda i, j, o: (i, j, 0))),
        out_shape=jax.ShapeDtypeStruct((b, n, d), F32),
        compiler_params=_cparams(("arbitrary", "arbitrary")),
        name="moe_combine",
    )(offs, y, dest4, h, mod)


def _moe(h, u2, lg, mod, wg, wu, wd, layer):
    b, n, d = h.shape
    cap = EC_CAPACITY * n // N_EXPERTS
    nblk = n // TOK_BLK
    dest, gate, starts = _route(lg, cap)
    offs = starts[:, :, 0].reshape(-1)
    dest4 = dest.reshape(b, N_EXPERTS, nblk, TOK_BLK)
    gate4 = gate.reshape(b, N_EXPERTS, nblk, TOK_BLK)
    y = _moe_ffn(offs, u2, dest4, gate4, wg, wu, wd, layer, cap)
    return _moe_combine(offs, y, dest4, h, mod, cap)


def _gla_proj_kernel(x_ref, mod_ref, g1_ref, w_ref, wz_ref, wgate_ref, bgate_ref,
                     q_ref, k_ref, v_ref, og_ref, gf_ref, gb_ref, *, dk, dv):
    m = mod_ref[0]
    u = _rms_mod(x_ref[0], g1_ref[...], m[0:1], m[1:2]).astype(BF16)
    acc = _dot(u, w_ref[...])
    z = _dot(u, wz_ref[...]).astype(BF16)
    gp = _dot(z, wgate_ref[...]) + bgate_ref[...]
    g = (jnp.minimum(gp, 0.0) - jnp.log1p(jnp.exp(-jnp.abs(gp)))) * (1.0 / GLA_GATE_NORM)
    q_ref[0] = acc[:, :dk]
    k_ref[0] = acc[:, dk:2 * dk]
    v_ref[0] = acc[:, 2 * dk:2 * dk + dv].astype(BF16)
    og_ref[0] = acc[:, 2 * dk + dv:]
    gf_ref[0] = g[:, :dk]
    gb_ref[0] = g[:, dk:]


def _gla_proj(h, mod, g1, w, wz, wgate, bgate, dk, dv):
    b, t, d = h.shape
    tm = min(512, t)
    nb_mod = mod.shape[0]
    midx = (lambda i, j: (i, 0, 0)) if nb_mod > 1 else (lambda i, j: (0, 0, 0))
    cols = 2 * dk + 2 * dv
    widths = [(dk, F32), (dk, F32), (dv, BF16), (dv, F32), (dk, F32), (dk, F32)]
    return pl.pallas_call(
        functools.partial(_gla_proj_kernel, dk=dk, dv=dv),
        grid=(b, t // tm),
        in_specs=[pl.BlockSpec((1, tm, d), lambda i, j: (i, j, 0)),
                  pl.BlockSpec((1, 6, d), midx),
                  pl.BlockSpec((1, d), lambda i, j: (0, 0)),
                  pl.BlockSpec((d, cols), lambda i, j: (0, 0)),
                  pl.BlockSpec((d, LANES), lambda i, j: (0, 0)),
                  pl.BlockSpec((LANES, 2 * dk), lambda i, j: (0, 0)),
                  pl.BlockSpec((1, 2 * dk), lambda i, j: (0, 0))],
        out_specs=[pl.BlockSpec((1, tm, wd), lambda i, j: (i, j, 0)) for wd, _ in widths],
        out_shape=[jax.ShapeDtypeStruct((b, t, wd), dt) for wd, dt in widths],
        compiler_params=_cparams(("parallel", "parallel")),
        name="gla_proj",
    )(h, mod, g1.reshape(1, d), w, wz, wgate, bgate)


def _chunk_cumsum(x, c, reverse):
    rows = x.shape[0]
    pos = lax.broadcasted_iota(I32, x.shape, 0) & (c - 1)
    s = 1
    while s < c:
        if reverse:
            x = x + jnp.where(pos < c - s, pltpu.roll(x, rows - s, 0), 0.0)
        else:
            x = x + jnp.where(pos >= s, pltpu.roll(x, s, 0), 0.0)
        s *= 2
    return x


def _gla_scan(q_ref, k_ref, v_ref, g_ref, acc_ref, st_ref, *, n_rows, reverse, scale):
    c = GLA_CHUNK
    gr = min(GLA_GROUP * c, n_rows)
    cpg = gr // c
    gpi = min(GLA_GROUPS_PER_TRIP, n_rows // gr)
    rows_it = gpi * gr
    n_it = n_rows // rows_it
    ii = lax.broadcasted_iota(I32, (gr, gr), 0)
    jj = lax.broadcasted_iota(I32, (gr, gr), 1)
    sh = c.bit_length() - 1
    same = (ii >> sh) == (jj >> sh)
    tri = same & ((jj >= ii) if reverse else (jj <= ii))
    tri_b = jnp.where(tri, 1.0, 0.0).astype(BF16)
    g_order = range(gpi - 1, -1, -1) if reverse else range(gpi)
    c_order = range(cpg - 1, -1, -1) if reverse else range(cpg)

    def body(i, carry):
        r0 = 0 if n_it == 1 else pl.multiple_of(((n_it - 1 - i) if reverse else i) * rows_it, rows_it)
        q_all = q_ref[0, pl.ds(r0, rows_it), :]
        k_all = k_ref[0, pl.ds(r0, rows_it), :]
        v_all = v_ref[0, pl.ds(r0, rows_it), :]
        g_all = g_ref[0, pl.ds(r0, rows_it), :]
        prev = acc_ref[pl.ds(r0, rows_it), :] if reverse else None
        st = st_ref[...]
        outs = [None] * gpi
        for gi in g_order:
            rs = slice(gi * gr, (gi + 1) * gr)
            k = k_all[rs]
            v = v_all[rs]
            cum = _chunk_cumsum(g_all[rs], c, reverse)
            cl = [cum[ci * c:ci * c + 1] if reverse else cum[(ci + 1) * c - 1:(ci + 1) * c] for ci in range(cpg)]
            cl_rows = jnp.concatenate([jnp.broadcast_to(x, (c, x.shape[1])) for x in cl], axis=0)
            q_dec = (q_all[rs] * scale * jnp.exp(cum)).astype(BF16)
            k_inv = (k * jnp.exp(-cum)).astype(BF16)
            k_st = (k * jnp.exp(cl_rows - cum)).astype(BF16)
            a = jnp.where(tri, _dot_nt(q_dec, k_inv), 0.0)
            o_intra = _dot(a.astype(BF16), v)
            parts = [None] * cpg
            for ci in c_order:
                cs = slice(ci * c, (ci + 1) * c)
                parts[ci] = o_intra[cs] + _dot_nt(q_dec[cs], st.astype(BF16))
                st = st * jnp.exp(cl[ci]) + _dot_tn(v[cs], k_st[cs])
            outs[gi] = jnp.concatenate(parts, axis=0)
        st_ref[...] = st
        o_all = outs[0] if gpi == 1 else jnp.concatenate(outs, axis=0)
        acc_ref[pl.ds(r0, rows_it), :] = (prev + o_all) if reverse else o_all
        return carry

    if n_it == 1:
        body(0, 0)
    else:
        lax.fori_loop(0, n_it, body, 0)


def _gla_merge(acc_ref, og_ref, gain, o_ref, t):
    tm = min(512, t)
    for r in range(t // tm):
        o = acc_ref[r * tm:(r + 1) * tm, :]
        ms = jnp.mean(o * o, axis=-1, keepdims=True)
        y = (o * lax.rsqrt(ms + EPS) * gain) * _silu(og_ref[0, r * tm:(r + 1) * tm, :])
        o_ref[0, r * tm:(r + 1) * tm, :] = y.astype(BF16)


def _gla_kernel(*refs, n, l, dk, ctx_out):
    q_ref, k_ref, v_ref, gf_ref, gb_ref, og_ref, cq_ref, ck_ref, cv_ref, cgf_ref, cgb_ref, cog_ref, on_ref = refs[:13]
    if ctx_out:
        o_ref, co_ref, acc_ref, cacc_ref, st_ref = refs[13:]
    else:
        o_ref, acc_ref, cacc_ref, st_ref = refs[13:]
    scale = dk ** -0.5
    for reverse, g_ref, cg_ref in ((False, gf_ref, cgf_ref), (True, gb_ref, cgb_ref)):
        st_ref[...] = jnp.zeros_like(st_ref)
        _gla_scan(cq_ref, ck_ref, cv_ref, cg_ref, cacc_ref, st_ref, n_rows=l, reverse=reverse, scale=scale)
        _gla_scan(q_ref, k_ref, v_ref, g_ref, acc_ref, st_ref, n_rows=n, reverse=reverse, scale=scale)
    gain = on_ref[...]
    _gla_merge(acc_ref, og_ref, gain, o_ref, n)
    if ctx_out:
        _gla_merge(cacc_ref, cog_ref, gain, co_ref, l)


def _gla(lat, ctx, onorm, ctx_out):
    q, k, v, og, gf, gb = lat
    cq, ck, cv, cog, cgf, cgb = ctx
    b, n, dkt = q.shape
    l = cq.shape[1]
    dvt = v.shape[2]
    dk = dkt // GLA_HEADS
    dv = dvt // GLA_HEADS

    def spec(t, w):
        return pl.BlockSpec((1, t, w), lambda i, h: (i, 0, h))

    in_specs = [spec(n, dk), spec(n, dk), spec(n, dv), spec(n, dk), spec(n, dk), spec(n, dv),
                spec(l, dk), spec(l, dk), spec(l, dv), spec(l, dk), spec(l, dk), spec(l, dv),
                pl.BlockSpec((1, dv), lambda i, h: (0, 0))]
    out_specs = [spec(n, dv)]
    out_shape = [jax.ShapeDtypeStruct((b, n, dvt), BF16)]
    if ctx_out:
        out_specs.append(spec(l, dv))
        out_shape.append(jax.ShapeDtypeStruct((b, l, dvt), BF16))
    res = pl.pallas_call(
        functools.partial(_gla_kernel, n=n, l=l, dk=dk, ctx_out=ctx_out),
        grid=(b, GLA_HEADS),
        in_specs=in_specs,
        out_specs=out_specs,
        out_shape=out_shape,
        scratch_shapes=[pltpu.VMEM((n, dv), F32), pltpu.VMEM((l, dv), F32), pltpu.VMEM((dv, dk), F32)],
        compiler_params=_cparams(("parallel", "parallel")),
        name="gla",
    )(q, k, v, gf, gb, og, cq, ck, cv, cgf, cgb, cog, onorm.reshape(1, dv))
    return (res[0], res[1]) if ctx_out else (res[0], None)


def _rope_tables(n):
    half = HEAD_DIM // 2
    t = jnp.arange(n, dtype=I32)
    inv = ROPE_THETA ** (-jnp.arange(0, half, 2, dtype=F32) / half)

    def tab(pos):
        ang = pos.astype(F32)[:, None] * inv[None, :]
        ang = jnp.concatenate([ang, ang], axis=-1)
        return jnp.cos(ang), jnp.sin(ang)

    cos_r, sin_r = tab(t // GRID_W)
    cos_c, sin_c = tab(t % GRID_W)
    cos = jnp.concatenate([cos_r, cos_c], axis=-1)
    sin = jnp.concatenate([sin_r, sin_c], axis=-1)
    sign = jnp.where((jnp.arange(HEAD_DIM) % half) < half // 2, -1.0, 1.0).astype(F32)
    reps = LANES // HEAD_DIM
    return jnp.tile(cos, (1, reps)), jnp.tile(sin * sign[None, :], (1, reps))


def _block_ones():
    i = np.arange(LANES)
    return jnp.asarray((i[:, None] // HEAD_DIM) == (i[None, :] // HEAD_DIM), BF16)


def _attn_gains(qa_g, ka_g, qb_g, kb_g):
    qs = HEAD_DIM ** -0.5
    one_a = jnp.ones((A_KV_HEADS * HEAD_DIM,), F32)
    one_b = jnp.ones((B_HEADS * HEAD_DIM,), F32)
    return jnp.concatenate([jnp.tile(qa_g, A_HEADS) * qs, jnp.tile(ka_g, A_KV_HEADS), one_a,
                            jnp.tile(qb_g, B_HEADS) * qs, jnp.tile(kb_g, B_HEADS), one_b]).reshape(1, -1)


def kernel(x, c, ctx, c_ctx, ada_w, ada_b, norm1_g, norm2_g, attn_w_in, attn_w_out, a_q_norm, a_k_norm, b_q_norm,
           b_k_norm, na_rpb, gla_w_in, gla_gk_w_f, gla_gk_b_f, gla_gk_w_b, gla_gk_b_b, gla_o_norm, gla_w_out,
           moe_router, moe_w_gate, moe_w_up, moe_w_down):
    b, n, d = x.shape
    depth = ada_w.shape[0]
    dk = gla_gk_w_f.shape[2]
    dv = gla_w_out.shape[1]
    rows = -(-(b + 1) // 8) * 8
    cvec = jnp.zeros((rows, d), F32).at[:b].set(c).at[b].set(c_ctx)
    mods = _mods(cvec, ada_w, ada_b)
    cos, sin = _rope_tables(n)
    bd = _block_ones()
    h, hc = x, ctx
    for layer in range(depth):
        last = layer == depth - 1
        i = layer // 2
        ml = mods[layer, :b].reshape(b, 6, d)
        mc = mods[layer, b:b + 1].reshape(1, 6, d)
        if layer % 2 == 0:
            w_in = attn_w_in[i].astype(BF16)
            gains = _attn_gains(a_q_norm[i], a_k_norm[i], b_q_norm[i], b_k_norm[i])
            qa, ka, va, qb, kb, vb = _attn_proj(h, ml, norm1_g[layer], w_in, gains, bd, cos, sin)
            cqa, cka, cva, cqb, ckb, cvb = _attn_proj(hc, mc, norm1_g[layer], w_in, gains, bd)
            oa = _gqa(qa, jnp.concatenate([ka, cka], axis=1), jnp.concatenate([va, cva], axis=1), A_HEADS, A_KV_HEADS)
            ob = _na(qb, kb, vb, ckb, cvb, _na_bias_table(na_rpb[i], n // GRID_W))
            o_lat = [oa, ob]
            o_ctx = None if last else [_gqa(cqa, cka, cva, A_HEADS, A_KV_HEADS), _gqa(cqb, ckb, cvb, B_HEADS, B_HEADS)]
            w_out = attn_w_out[i].astype(BF16)
        else:
            w_main = gla_w_in[i][:, :2 * dk + 2 * dv].astype(BF16)
            wz = jnp.pad(gla_w_in[i][:, 2 * dk + 2 * dv:], ((0, 0), (0, LANES - 2 * GLA_GATE_RANK))).astype(BF16)
            wgate = jnp.zeros((LANES, 2 * dk), F32)
            wgate = wgate.at[:GLA_GATE_RANK, :dk].set(gla_gk_w_f[i]).at[GLA_GATE_RANK:2 * GLA_GATE_RANK, dk:].set(gla_gk_w_b[i])
            bgate = jnp.concatenate([gla_gk_b_f[i], gla_gk_b_b[i]]).reshape(1, 2 * dk)
            lat = _gla_proj(h, ml, norm1_g[layer], w_main, wz, wgate.astype(BF16), bgate, dk, dv)
            cx = _gla_proj(hc, mc, norm1_g[layer], w_main, wz, wgate.astype(BF16), bgate, dk, dv)
            o, co = _gla(lat, cx, gla_o_norm[i], not last)
            o_lat = [o]
            o_ctx = None if last else [co]
            w_out = gla_w_out[i].astype(BF16)
        wr_hi, wr_lo = _split2(moe_router[layer].T)
        h1, u2, lg = _outproj(o_lat, h, ml, norm2_g[layer], w_out, wr_hi, wr_lo)
        h = _moe(h1, u2, lg, ml, moe_w_gate, moe_w_up, moe_w_down, layer)
        if not last:
            hc1, uc2, lgc = _outproj(o_ctx, hc, mc, norm2_g[layer], w_out, wr_hi, wr_lo)
            hc = _moe(hc1, uc2, lgc, mc, moe_w_gate, moe_w_up, moe_w_down, layer)
    return h
```

```python
import functools

import numpy as np
import jax
import jax.numpy as jnp
from jax import lax
from jax.experimental import pallas as pl
from jax.experimental.pallas import tpu as pltpu

F32 = jnp.float32
BF16 = jnp.bfloat16
I32 = jnp.int32

GRID_W = 64
HEAD_DIM = 64
A_HEADS = 8
A_KV_HEADS = 2
B_HEADS = 8
NA_ROWS = 8
NA_COLS = 16
ROPE_THETA = 10000.0
GLA_HEADS = 4
GLA_GATE_RANK = 16
GLA_GATE_NORM = 16.0
GLA_CHUNK = 64
GLA_GROUP = 4
GLA_GROUPS_PER_TRIP = 8
N_EXPERTS = 16
EC_CAPACITY = 2
EPS = 1e-6

LANES = 128
BF16_ROWS = 16
NA_QROWS = 4
TOK_BLK = LANES
SMALL_WIN = 48
GATHER_TRIP = 32
MASK_VALUE = -1e30
VMEM_LIMIT = 56 * 1024 * 1024

_NT = (((1,), (1,)), ((), ()))
_TN = (((0,), (0,)), ((), ()))


def _cparams(sem):
    return pltpu.CompilerParams(dimension_semantics=sem, vmem_limit_bytes=VMEM_LIMIT)


def _dot(a, b):
    return jnp.dot(a, b, preferred_element_type=F32)


def _dot_nt(a, b):
    return lax.dot_general(a, b, _NT, preferred_element_type=F32)


def _dot_tn(a, b):
    return lax.dot_general(a, b, _TN, preferred_element_type=F32)


def _split2(a):
    hi = a.astype(BF16)
    lo = (a - hi.astype(F32)).astype(BF16)
    return hi, lo


def _silu(a):
    return a / (1.0 + jnp.exp(-a))


def _rms_mod(x, g, shift, scale):
    ms = jnp.mean(x * x, axis=-1, keepdims=True)
    return (x * lax.rsqrt(ms + EPS) * g) * (1.0 + scale) + shift


def _mods_kernel(c_ref, w_ref, b_ref, o_ref):
    c = c_ref[...]
    s_hi, s_lo = _split2(_silu(c))
    w_hi, w_lo = _split2(w_ref[0])
    o_ref[0] = _dot(s_hi, w_hi) + _dot(s_lo, w_hi) + _dot(s_hi, w_lo) + b_ref[0]


def _mods(cvec, ada_w, ada_b):
    depth, d, d6 = ada_w.shape
    rows = cvec.shape[0]
    tn = 1536 if d6 % 1536 == 0 else d6
    return pl.pallas_call(
        _mods_kernel,
        grid=(depth, d6 // tn),
        in_specs=[pl.BlockSpec((rows, d), lambda l, j: (0, 0)),
                  pl.BlockSpec((1, d, tn), lambda l, j: (l, 0, j)),
                  pl.BlockSpec((1, 1, tn), lambda l, j: (l, 0, j))],
        out_specs=pl.BlockSpec((1, rows, tn), lambda l, j: (l, 0, j)),
        out_shape=jax.ShapeDtypeStruct((depth, rows, d6), F32),
        compiler_params=_cparams(("arbitrary", "arbitrary")),
        name="mods",
    )(cvec, ada_w, ada_b.reshape(depth, 1, d6))


_ATTN_COLS = (A_HEADS + 2 * A_KV_HEADS + 3 * B_HEADS) * HEAD_DIM
_QA = (0, 4)
_KA = (4, 5)
_VA = (5, 6)
_QB = (6, 10)
_KB = (10, 14)
_VB = (14, 18)


def _attn_proj_kernel(*refs, rope):
    if rope:
        x_ref, mod_ref, g1_ref, w_ref, gain_ref, bd_ref, cos_ref, sin_ref = refs[:8]
        outs = refs[8:]
    else:
        x_ref, mod_ref, g1_ref, w_ref, gain_ref, bd_ref = refs[:6]
        outs = refs[6:]
    qa_ref, ka_ref, va_ref, qb_ref, kb_ref, vb_ref = outs
    m = mod_ref[0]
    u = _rms_mod(x_ref[0], g1_ref[...], m[0:1], m[1:2]).astype(BF16)
    acc = _dot(u, w_ref[...])
    bd = bd_ref[...]
    if rope:
        cos = cos_ref[...]
        sin = sin_ref[...]
        lane = lax.broadcasted_iota(I32, cos.shape, 1)
        first = (lane & 31) < 16

    def chunk(j, norm, rot):
        cch = acc[:, j * LANES:(j + 1) * LANES]
        if norm:
            sq_hi, sq_lo = _split2(cch * cch)
            ss = _dot(sq_hi, bd) + _dot(sq_lo, bd)
            cch = cch * lax.rsqrt(ss * (1.0 / HEAD_DIM) + EPS) * gain_ref[:, j * LANES:(j + 1) * LANES]
        if rot:
            partner = jnp.where(first, pltpu.roll(cch, LANES - 16, 1), pltpu.roll(cch, 16, 1))
            cch = cch * cos + partner * sin
        return cch.astype(BF16)

    for (lo, hi), ref, norm, rot in ((_QA, qa_ref, True, rope), (_KA, ka_ref, True, rope), (_VA, va_ref, False, False),
                                     (_QB, qb_ref, True, False), (_KB, kb_ref, True, False), (_VB, vb_ref, False, False)):
        for j in range(lo, hi):
            ref[0, :, (j - lo) * LANES:(j - lo + 1) * LANES] = chunk(j, norm, rot)


def _attn_proj(h, mod, g1, w, gains, bd, cos=None, sin=None):
    b, t, d = h.shape
    tm = min(512, t)
    rope = cos is not None
    nb_mod = mod.shape[0]
    midx = (lambda i, j: (i, 0, 0)) if nb_mod > 1 else (lambda i, j: (0, 0, 0))
    in_specs = [pl.BlockSpec((1, tm, d), lambda i, j: (i, j, 0)),
                pl.BlockSpec((1, 6, d), midx),
                pl.BlockSpec((1, d), lambda i, j: (0, 0)),
                pl.BlockSpec((d, _ATTN_COLS), lambda i, j: (0, 0)),
                pl.BlockSpec((1, _ATTN_COLS), lambda i, j: (0, 0)),
                pl.BlockSpec((LANES, LANES), lambda i, j: (0, 0))]
    args = [h, mod, g1.reshape(1, d), w, gains, bd]
    if rope:
        in_specs += [pl.BlockSpec((tm, LANES), lambda i, j: (j, 0))] * 2
        args += [cos, sin]
    widths = [(hi - lo) * LANES for lo, hi in (_QA, _KA, _VA, _QB, _KB, _VB)]
    return pl.pallas_call(
        functools.partial(_attn_proj_kernel, rope=rope),
        grid=(b, t // tm),
        in_specs=in_specs,
        out_specs=[pl.BlockSpec((1, tm, wd), lambda i, j: (i, j, 0)) for wd in widths],
        out_shape=[jax.ShapeDtypeStruct((b, t, wd), BF16) for wd in widths],
        compiler_params=_cparams(("parallel", "parallel")),
        name="attn_proj_rope" if rope else "attn_proj",
    )(*args)


def _gqa_kernel(q_ref, k_ref, v_ref, o_ref, *, hq, hk):
    grp = hq // hk
    for h in range(hq):
        kv = h // grp
        q = q_ref[0, :, h * HEAD_DIM:(h + 1) * HEAD_DIM]
        k = k_ref[0, :, kv * HEAD_DIM:(kv + 1) * HEAD_DIM]
        v = v_ref[0, :, kv * HEAD_DIM:(kv + 1) * HEAD_DIM]
        s = _dot_nt(q, k)
        p = jnp.exp(s - jnp.max(s, axis=-1, keepdims=True))
        l = jnp.sum(p, axis=-1, keepdims=True)
        o = _dot(p.astype(BF16), v) / l
        o_ref[0, :, h * HEAD_DIM:(h + 1) * HEAD_DIM] = o.astype(BF16)


def _gqa(q, k, v, hq, hk):
    b, t, _ = q.shape
    s = k.shape[1]
    tq = min(512, t)
    return pl.pallas_call(
        functools.partial(_gqa_kernel, hq=hq, hk=hk),
        grid=(b, t // tq),
        in_specs=[pl.BlockSpec((1, tq, hq * HEAD_DIM), lambda i, j: (i, j, 0)),
                  pl.BlockSpec((1, s, hk * HEAD_DIM), lambda i, j: (i, 0, 0)),
                  pl.BlockSpec((1, s, hk * HEAD_DIM), lambda i, j: (i, 0, 0))],
        out_specs=pl.BlockSpec((1, tq, hq * HEAD_DIM), lambda i, j: (i, j, 0)),
        out_shape=jax.ShapeDtypeStruct((b, t, hq * HEAD_DIM), BF16),
        compiler_params=_cparams(("parallel", "parallel")),
        name="gqa",
    )(q, k, v)


def _na_geometry(rows):
    kr = min(NA_QROWS + NA_ROWS - 1, rows)
    nblk = rows // NA_QROWS
    return kr, nblk


def _na_bias_table(rpb, rows):
    kr, nblk = _na_geometry(rows)
    wr = min(NA_ROWS, rows)
    heads = rpb.shape[0]
    qc = np.arange(GRID_W)[:, None]
    kc = np.arange(GRID_W)[None, :]
    cs = np.clip(qc - NA_COLS // 2, 0, GRID_W - NA_COLS)
    col_ok = (kc >= cs) & (kc < cs + NA_COLS)
    pick = (np.arange(2 * NA_COLS - 1)[:, None, None] == (kc - qc + NA_COLS - 1)[None]) & col_ok[None]
    toeplitz = jnp.einsum("hrd,dqk->hrqk", rpb.astype(F32), jnp.asarray(pick, F32), precision=lax.Precision.HIGHEST)
    toeplitz = jnp.where(col_ok[None, None], toeplitz, MASK_VALUE)
    tabs = []
    for rb in (0, min(1, nblk - 1), nblk - 1):
        qr0 = rb * NA_QROWS
        kr0 = int(np.clip(qr0 - NA_ROWS // 2, 0, rows - kr))
        qr = qr0 + np.arange(NA_QROWS)[:, None]
        krr = kr0 + np.arange(kr)[None, :]
        rs = np.clip(qr - wr // 2, 0, rows - wr)
        row_ok = (krr >= rs) & (krr < rs + wr)
        dr = krr - qr + (NA_ROWS - 1)
        masked = jnp.full((heads, GRID_W, GRID_W), MASK_VALUE, F32)
        blocks = [jnp.concatenate([toeplitz[:, dr[i, j]] if row_ok[i, j] else masked for j in range(kr)], axis=2)
                  for i in range(NA_QROWS)]
        tabs.append(jnp.concatenate(blocks, axis=1))
    return jnp.stack(tabs)


def _na_kernel(q_ref, k_ref, v_ref, kc_ref, vc_ref, bias_ref, o_ref, *, rows, heads):
    kr, _ = _na_geometry(rows)
    rb = pl.program_id(1)
    kr0 = jnp.clip(rb * NA_QROWS - NA_ROWS // 2, 0, rows - kr)
    start = pl.multiple_of(kr0 * GRID_W, GRID_W)
    kw = k_ref[0, pl.ds(start, kr * GRID_W), :]
    vw = v_ref[0, pl.ds(start, kr * GRID_W), :]
    for h in range(heads):
        sl = slice(h * HEAD_DIM, (h + 1) * HEAD_DIM)
        q = q_ref[0, :, sl]
        s_w = _dot_nt(q, kw[:, sl]) + bias_ref[0, h]
        s_c = _dot_nt(q, kc_ref[0, :, sl])
        m = jnp.maximum(jnp.max(s_w, axis=-1, keepdims=True), jnp.max(s_c, axis=-1, keepdims=True))
        p_w = jnp.exp(s_w - m)
        p_c = jnp.exp(s_c - m)
        l = jnp.sum(p_w, axis=-1, keepdims=True) + jnp.sum(p_c, axis=-1, keepdims=True)
        o = (_dot(p_w.astype(BF16), vw[:, sl]) + _dot(p_c.astype(BF16), vc_ref[0, :, sl])) / l
        o_ref[0, :, sl] = o.astype(BF16)


def _na(q, k, v, kc, vc, bias):
    b, n, c = q.shape
    l = kc.shape[1]
    rows = n // GRID_W
    kr, nblk = _na_geometry(rows)
    tq = NA_QROWS * GRID_W
    heads = c // HEAD_DIM

    def pattern(i, j):
        return (jnp.where(j == 0, 0, jnp.where(j == nblk - 1, 2, 1)), 0, 0, 0)

    return pl.pallas_call(
        functools.partial(_na_kernel, rows=rows, heads=heads),
        grid=(b, nblk),
        in_specs=[pl.BlockSpec((1, tq, c), lambda i, j: (i, j, 0)),
                  pl.BlockSpec((1, n, c), lambda i, j: (i, 0, 0)),
                  pl.BlockSpec((1, n, c), lambda i, j: (i, 0, 0)),
                  pl.BlockSpec((1, l, c), lambda i, j: (i, 0, 0)),
                  pl.BlockSpec((1, l, c), lambda i, j: (i, 0, 0)),
                  pl.BlockSpec((1, heads, tq, kr * GRID_W), pattern)],
        out_specs=pl.BlockSpec((1, tq, c), lambda i, j: (i, j, 0)),
        out_shape=jax.ShapeDtypeStruct((b, n, c), BF16),
        compiler_params=_cparams(("parallel", "arbitrary")),
        name="na_attn",
    )(q, k, v, kc, vc, bias)


def _outproj_kernel(*refs, n_o, route):
    o_refs = refs[:n_o]
    h_ref, mod_ref, g2_ref, w_ref = refs[n_o:n_o + 4]
    rest = refs[n_o + 4:]
    if route:
        wr_hi_ref, wr_lo_ref, hn_ref, u2_ref, lg_ref = rest
    else:
        hn_ref, = rest
    m = mod_ref[0]
    o = o_refs[0][0] if n_o == 1 else jnp.concatenate([r[0] for r in o_refs], axis=1)
    hn = h_ref[0] + m[2:3] * _dot(o, w_ref[...])
    hn_ref[0] = hn
    if route:
        u2 = _rms_mod(hn, g2_ref[...], m[3:4], m[4:5])
        u_hi, u_lo = _split2(u2)
        u2_ref[0] = u_hi
        wr_hi = wr_hi_ref[...]
        both = _dot_nt(jnp.concatenate([wr_hi, wr_lo_ref[...]], axis=0), u_hi)
        lg_ref[0] = both[:N_EXPERTS] + both[N_EXPERTS:] + _dot_nt(wr_hi, u_lo)


def _outproj(o_list, h, mod, g2, w, wr_hi=None, wr_lo=None):
    b, t, d = h.shape
    tm = min(512, t)
    route = wr_hi is not None
    nb_mod = mod.shape[0]
    midx = (lambda i, j: (i, 0, 0)) if nb_mod > 1 else (lambda i, j: (0, 0, 0))
    in_specs = [pl.BlockSpec((1, tm, o.shape[2]), lambda i, j: (i, j, 0)) for o in o_list]
    in_specs += [pl.BlockSpec((1, tm, d), lambda i, j: (i, j, 0)),
                 pl.BlockSpec((1, 6, d), midx),
                 pl.BlockSpec((1, d), lambda i, j: (0, 0)),
                 pl.BlockSpec((d, d), lambda i, j: (0, 0))]
    args = list(o_list) + [h, mod, g2.reshape(1, d), w]
    out_specs = [pl.BlockSpec((1, tm, d), lambda i, j: (i, j, 0))]
    out_shape = [jax.ShapeDtypeStruct((b, t, d), F32)]
    if route:
        in_specs += [pl.BlockSpec((N_EXPERTS, d), lambda i, j: (0, 0))] * 2
        args += [wr_hi, wr_lo]
        out_specs += [pl.BlockSpec((1, tm, d), lambda i, j: (i, j, 0)),
                      pl.BlockSpec((1, N_EXPERTS, tm), lambda i, j: (i, 0, j))]
        out_shape += [jax.ShapeDtypeStruct((b, t, d), BF16), jax.ShapeDtypeStruct((b, N_EXPERTS, t), F32)]
    return pl.pallas_call(
        functools.partial(_outproj_kernel, n_o=len(o_list), route=route),
        grid=(b, t // tm),
        in_specs=in_specs,
        out_specs=out_specs,
        out_shape=out_shape,
        compiler_params=_cparams(("parallel", "parallel")),
        name="outproj",
    )(*args)


def _cumsum_excl(x):
    e, n = x.shape
    nb = n // LANES
    st = jnp.concatenate([x[:, j * LANES:(j + 1) * LANES] for j in range(nb)], axis=0)
    ii = lax.broadcasted_iota(I32, (LANES, LANES), 0)
    jj = lax.broadcasted_iota(I32, (LANES, LANES), 1)
    incl = _dot(st.astype(BF16), jnp.where(ii <= jj, 1.0, 0.0).astype(BF16))
    tot = jnp.broadcast_to(incl[:, LANES - 1:LANES], incl.shape)
    r = lax.broadcasted_iota(I32, (nb * e, nb * e), 0)
    c = lax.broadcasted_iota(I32, (nb * e, nb * e), 1)
    sh = e.bit_length() - 1
    same = (r & (e - 1)) == (c & (e - 1))
    prev = jnp.where(same & (lax.shift_right_logical(c, sh) < lax.shift_right_logical(r, sh)), 1.0, 0.0).astype(BF16)
    starts = _dot(prev, tot.astype(BF16))
    excl_st = incl - st + starts
    excl = jnp.concatenate([excl_st[j * e:(j + 1) * e, :] for j in range(nb)], axis=1)
    return excl, starts


def _route_kernel(lg_ref, dest_ref, gate_ref, offs_ref, *, cap):
    x = lg_ref[0]
    ex = jnp.exp(x - jnp.max(x, axis=0, keepdims=True))
    aff = ex / jnp.sum(ex, axis=0, keepdims=True)
    bits = lax.bitcast_convert_type(aff, I32)

    def enough(cand):
        return jnp.sum(jnp.where(bits >= cand, 1.0, 0.0), axis=1, keepdims=True) >= cap

    def search(i, cur):
        sh = 28 - 2 * i
        c1, c2, c3 = (cur | lax.shift_left(jnp.int32(k), sh) for k in (1, 2, 3))
        return jnp.where(enough(c3), c3, jnp.where(enough(c2), c2, jnp.where(enough(c1), c1, cur)))

    top = jnp.full((x.shape[0], 1), 1 << 30, I32)
    thr = jnp.where(enough(top), top, jnp.zeros_like(top))
    thr = lax.fori_loop(0, 15, search, thr)
    gt = bits > thr
    eq = bits == thr
    need = cap - jnp.sum(jnp.where(gt, 1.0, 0.0), axis=1, keepdims=True)
    rank_eq, _ = _cumsum_excl(jnp.where(eq, 1.0, 0.0))
    sel = gt | (eq & (rank_eq < need))
    pos, starts = _cumsum_excl(jnp.where(sel, 1.0, 0.0))
    dest_ref[0] = jnp.where(sel, pos.astype(I32), -1)
    gate_ref[0] = jnp.where(sel, aff, 0.0)
    offs_ref[0] = starts.astype(I32)


def _route(lg, cap):
    b, e, n = lg.shape
    nb = n // LANES
    return pl.pallas_call(
        functools.partial(_route_kernel, cap=cap),
        grid=(b,),
        in_specs=[pl.BlockSpec((1, e, n), lambda i: (i, 0, 0))],
        out_specs=[pl.BlockSpec((1, e, n), lambda i: (i, 0, 0)),
                   pl.BlockSpec((1, e, n), lambda i: (i, 0, 0)),
                   pl.BlockSpec((1, nb * e, LANES), lambda i: (i, 0, 0))],
        out_shape=[jax.ShapeDtypeStruct((b, e, n), I32), jax.ShapeDtypeStruct((b, e, n), F32),
                   jax.ShapeDtypeStruct((b, nb * e, LANES), I32)],
        compiler_params=_cparams(("parallel",)),
        name="route",
    )(lg)


def _blk_range(offs_ref, b, j, e, nblk, cap):
    base = (b * nblk + j) * N_EXPERTS + e
    last = (b * nblk + nblk - 1) * N_EXPERTS + e
    return offs_ref[base], jnp.where(j + 1 < nblk, offs_ref[jnp.minimum(base + N_EXPERTS, last)], cap)


def _win_start(off, cap, win):
    aligned = lax.shift_right_logical(off, BF16_ROWS.bit_length() - 1) * BF16_ROWS
    return pl.multiple_of(jnp.minimum(aligned, cap - win), BF16_ROWS)


def _onehot(dest_row, off_al, win):
    rows = lax.broadcasted_iota(I32, (win, TOK_BLK), 0) + off_al
    return jnp.where(rows == dest_row, 1.0, 0.0)


def _moe_ffn_kernel(offs_ref, u_ref, dest_ref, gate_ref, wg_ref, wu_ref, wd_ref, y_ref,
                    xg_ref, gb_ref, wgb_ref, wub_ref, wdb_ref, *, cap, nblk, win, small, nbatch, batch_ffn):
    e = pl.program_id(0)

    def load_weights():
        wgb_ref[...] = wg_ref[0, 0].astype(BF16)
        wub_ref[...] = wu_ref[0, 0].astype(BF16)
        wdb_ref[...] = wd_ref[0, 0].astype(BF16)

    trip = min(GATHER_TRIP, nblk)

    def gather_sample(b, blk, row0):
        xg_ref[pl.ds(row0, cap), :] = jnp.zeros((cap, xg_ref.shape[1]), F32)
        gb_ref[pl.ds(row0, cap), :] = jnp.zeros((cap, LANES), F32)

        def gather(t, carry):
            js = [t * trip + i for i in range(trip)]
            ranges = [_blk_range(offs_ref, b, j, e, nblk, cap) for j in js]

            def run(w):
                for j, (off, _) in zip(js, ranges):
                    st = _win_start(off, cap, w)
                    g = _onehot(dest_ref[blk, 0, pl.ds(j, 1), :], st, w)
                    tok0 = j * TOK_BLK if isinstance(j, int) else pl.multiple_of(j * TOK_BLK, TOK_BLK)
                    uj = u_ref[blk, pl.ds(tok0, TOK_BLK), :]
                    r = pl.multiple_of(row0 + st, BF16_ROWS)
                    xg_ref[pl.ds(r, w), :] += _dot(g.astype(BF16), uj)
                    gsum = jnp.sum(g * gate_ref[blk, 0, pl.ds(j, 1), :], axis=1, keepdims=True)
                    gb_ref[pl.ds(r, w), :] += jnp.broadcast_to(gsum, (w, LANES))

            if small == win:
                run(win)
            else:
                fits = ranges[0][1] <= _win_start(ranges[0][0], cap, small) + small
                for off, end in ranges[1:]:
                    fits = fits & (end <= _win_start(off, cap, small) + small)
                pl.when(fits)(functools.partial(run, small))
                pl.when(jnp.logical_not(fits))(functools.partial(run, win))
            return carry

        if nblk == trip:
            gather(0, 0)
        else:
            lax.fori_loop(0, nblk // trip, gather, 0)

    def ffn(m):
        x = xg_ref[0:m, :].astype(BF16)
        ff = wgb_ref.shape[1]
        fc = min(512, ff)
        acc = jnp.zeros((m, wdb_ref.shape[1]), F32)
        for c in range(ff // fc):
            a = _dot(x, wgb_ref[:, c * fc:(c + 1) * fc])
            up = _dot(x, wub_ref[:, c * fc:(c + 1) * fc])
            acc = acc + _dot((_silu(a) * up).astype(BF16), wdb_ref[c * fc:(c + 1) * fc, :])
        return (acc * gb_ref[0:m, 0:1]).astype(BF16)

    if batch_ffn:
        load_weights()
        for bb in range(nbatch):
            gather_sample(bb, bb, bb * cap)
        y = ffn(nbatch * cap)
        for bb in range(nbatch):
            y_ref[bb, 0] = y[bb * cap:(bb + 1) * cap]
    else:
        b = pl.program_id(1)
        pl.when(b == 0)(load_weights)
        gather_sample(b, 0, 0)
        y_ref[0, 0] = ffn(cap)


def _moe_ffn(offs, u, dest4, gate4, wg, wu, wd, layer, cap):
    b, n, d = u.shape
    nblk = n // TOK_BLK
    win = min(TOK_BLK + BF16_ROWS, cap)
    ff = wg.shape[3]
    batch_ffn = cap < 256
    if batch_ffn:
        grid = (N_EXPERTS,)
        rows = b * cap
        tok_spec = pl.BlockSpec((b, n, d), lambda e, o: (0, 0, 0))
        sel_spec = pl.BlockSpec((b, 1, nblk, TOK_BLK), lambda e, o: (0, e, 0, 0))
        y_spec = pl.BlockSpec((b, 1, cap, d), lambda e, o: (0, e, 0, 0))

        def w_spec(r, c):
            return pl.BlockSpec((1, 1, r, c), lambda e, o: (layer, e, 0, 0))
    else:
        grid = (N_EXPERTS, b)
        rows = cap
        tok_spec = pl.BlockSpec((1, n, d), lambda e, i, o: (i, 0, 0))
        sel_spec = pl.BlockSpec((1, 1, nblk, TOK_BLK), lambda e, i, o: (i, e, 0, 0))
        y_spec = pl.BlockSpec((1, 1, cap, d), lambda e, i, o: (i, e, 0, 0))

        def w_spec(r, c):
            return pl.BlockSpec((1, 1, r, c), lambda e, i, o: (layer, e, 0, 0))
    return pl.pallas_call(
        functools.partial(_moe_ffn_kernel, cap=cap, nblk=nblk, win=win, small=min(SMALL_WIN, cap), nbatch=b,
                          batch_ffn=batch_ffn),
        grid_spec=pltpu.PrefetchScalarGridSpec(
            num_scalar_prefetch=1,
            grid=grid,
            in_specs=[tok_spec, sel_spec, sel_spec, w_spec(d, ff), w_spec(d, ff), w_spec(ff, d)],
            out_specs=y_spec,
            scratch_shapes=[pltpu.VMEM((rows, d), F32), pltpu.VMEM((rows, LANES), F32),
                            pltpu.VMEM((d, ff), BF16), pltpu.VMEM((d, ff), BF16), pltpu.VMEM((ff, d), BF16)]),
        out_shape=jax.ShapeDtypeStruct((b, N_EXPERTS, cap, d), BF16),
        compiler_params=_cparams(("arbitrary",) * len(grid)),
        name="moe_ffn",
    )(offs, u, dest4, gate4, wg, wu, wd)


def _moe_combine_kernel(offs_ref, y_ref, dest_ref, h_ref, mod_ref, out_ref, *, cap, nblk, win, small, nsub):
    b = pl.program_id(0)
    jb = pl.program_id(1)
    gate = mod_ref[0][5:6]
    ranges = [[_blk_range(offs_ref, b, jb * nsub + s, e, nblk, cap) for e in range(N_EXPERTS)] for s in range(nsub)]

    def stacked():
        for s in range(nsub):
            rows = slice(s * TOK_BLK, (s + 1) * TOK_BLK)
            starts = [_win_start(off, cap, small) for off, _ in ranges[s]]
            ycat = jnp.concatenate([y_ref[0, e, pl.ds(starts[e], small), :] for e in range(N_EXPERTS)], axis=0)
            gcat = jnp.concatenate([_onehot(dest_ref[0, e, s:s + 1, :], starts[e], small) for e in range(N_EXPERTS)], axis=0)
            out_ref[0, rows, :] = h_ref[0, rows, :] + gate * _dot_tn(gcat.astype(BF16), ycat)

    def per_expert():
        for s in range(nsub):
            rows = slice(s * TOK_BLK, (s + 1) * TOK_BLK)
            acc = jnp.zeros((TOK_BLK, out_ref.shape[2]), F32)
            for e in range(N_EXPERTS):
                st = _win_start(ranges[s][e][0], cap, win)
                g = _onehot(dest_ref[0, e, s:s + 1, :], st, win).astype(BF16)
                acc = acc + _dot_tn(g, y_ref[0, e, pl.ds(st, win), :])
            out_ref[0, rows, :] = h_ref[0, rows, :] + gate * acc

    if small == win:
        stacked()
    else:
        fits = None
        for per_block in ranges:
            for off, end in per_block:
                ok = end <= _win_start(off, cap, small) + small
                fits = ok if fits is None else fits & ok
        pl.when(fits)(stacked)
        pl.when(jnp.logical_not(fits))(per_expert)


def _moe_combine(offs, y, dest4, h, mod, cap):
    b, n, d = h.shape
    nblk = n // TOK_BLK
    win = min(TOK_BLK + BF16_ROWS, cap)
    small = min(SMALL_WIN, cap)
    nsub = min(8, nblk)
    tb = nsub * TOK_BLK
    nb_mod = mod.shape[0]
    midx = (lambda i, j, o: (i, 0, 0)) if nb_mod > 1 else (lambda i, j, o: (0, 0, 0))
    return pl.pallas_call(
        functools.partial(_moe_combine_kernel, cap=cap, nblk=nblk, win=win, small=small, nsub=nsub),
        grid_spec=pltpu.PrefetchScalarGridSpec(
            num_scalar_prefetch=1,
            grid=(b, n // tb),
            in_specs=[pl.BlockSpec((1, N_EXPERTS, cap, d), lambda i, j, o: (i, 0, 0, 0), pipeline_mode=pl.Buffered(1)),
                      pl.BlockSpec((1, N_EXPERTS, nsub, TOK_BLK), lambda i, j, o: (i, 0, j, 0)),
                      pl.BlockSpec((1, tb, d), lambda i, j, o: (i, j, 0)),
                      pl.BlockSpec((1, 6, d), midx)],
            out_specs=pl.BlockSpec((1, tb, d), lambda i, j, o: (i, j, 0))),
        out_shape=jax.ShapeDtypeStruct((b, n, d), F32),
        compiler_params=_cparams(("arbitrary", "arbitrary")),
        name="moe_combine",
    )(offs, y, dest4, h, mod)


def _moe(h, u2, lg, mod, wg, wu, wd, layer):
    b, n, d = h.shape
    cap = EC_CAPACITY * n // N_EXPERTS
    nblk = n // TOK_BLK
    dest, gate, starts = _route(lg, cap)
    offs = starts[:, :, 0].reshape(-1)
    dest4 = dest.reshape(b, N_EXPERTS, nblk, TOK_BLK)
    gate4 = gate.reshape(b, N_EXPERTS, nblk, TOK_BLK)
    y = _moe_ffn(offs, u2, dest4, gate4, wg, wu, wd, layer, cap)
    return _moe_combine(offs, y, dest4, h, mod, cap)


def _gla_proj_kernel(x_ref, mod_ref, g1_ref, w_ref, wz_ref, wgate_ref, bgate_ref,
                     q_ref, k_ref, v_ref, og_ref, gf_ref, gb_ref, *, dk, dv):
    m = mod_ref[0]
    u = _rms_mod(x_ref[0], g1_ref[...], m[0:1], m[1:2]).astype(BF16)
    acc = _dot(u, w_ref[...])
    z = _dot(u, wz_ref[...]).astype(BF16)
    gp = _dot(z, wgate_ref[...]) + bgate_ref[...]
    g = (jnp.minimum(gp, 0.0) - jnp.log1p(jnp.exp(-jnp.abs(gp)))) * (1.0 / GLA_GATE_NORM)
    q_ref[0] = acc[:, :dk]
    k_ref[0] = acc[:, dk:2 * dk]
    v_ref[0] = acc[:, 2 * dk:2 * dk + dv].astype(BF16)
    og_ref[0] = acc[:, 2 * dk + dv:]
    gf_ref[0] = g[:, :dk]
    gb_ref[0] = g[:, dk:]


def _gla_proj(h, mod, g1, w, wz, wgate, bgate, dk, dv):
    b, t, d = h.shape
    tm = min(512, t)
    nb_mod = mod.shape[0]
    midx = (lambda i, j: (i, 0, 0)) if nb_mod > 1 else (lambda i, j: (0, 0, 0))
    cols = 2 * dk + 2 * dv
    widths = [(dk, F32), (dk, F32), (dv, BF16), (dv, F32), (dk, F32), (dk, F32)]
    return pl.pallas_call(
        functools.partial(_gla_proj_kernel, dk=dk, dv=dv),
        grid=(b, t // tm),
        in_specs=[pl.BlockSpec((1, tm, d), lambda i, j: (i, j, 0)),
                  pl.BlockSpec((1, 6, d), midx),
                  pl.BlockSpec((1, d), lambda i, j: (0, 0)),
                  pl.BlockSpec((d, cols), lambda i, j: (0, 0)),
                  pl.BlockSpec((d, LANES), lambda i, j: (0, 0)),
                  pl.BlockSpec((LANES, 2 * dk), lambda i, j: (0, 0)),
                  pl.BlockSpec((1, 2 * dk), lambda i, j: (0, 0))],
        out_specs=[pl.BlockSpec((1, tm, wd), lambda i, j: (i, j, 0)) for wd, _ in widths],
        out_shape=[jax.ShapeDtypeStruct((b, t, wd), dt) for wd, dt in widths],
        compiler_params=_cparams(("parallel", "parallel")),
        name="gla_proj",
    )(h, mod, g1.reshape(1, d), w, wz, wgate, bgate)


def _chunk_cumsum(x, c, reverse):
    rows = x.shape[0]
    pos = lax.broadcasted_iota(I32, x.shape, 0) & (c - 1)
    s = 1
    while s < c:
        if reverse:
            x = x + jnp.where(pos < c - s, pltpu.roll(x, rows - s, 0), 0.0)
        else:
            x = x + jnp.where(pos >= s, pltpu.roll(x, s, 0), 0.0)
        s *= 2
    return x


def _gla_scan(q_ref, k_ref, v_ref, g_ref, acc_ref, st_ref, *, n_rows, reverse, scale):
    c = GLA_CHUNK
    gr = min(GLA_GROUP * c, n_rows)
    cpg = gr // c
    gpi = min(GLA_GROUPS_PER_TRIP, n_rows // gr)
    rows_it = gpi * gr
    n_it = n_rows // rows_it
    ii = lax.broadcasted_iota(I32, (gr, gr), 0)
    jj = lax.broadcasted_iota(I32, (gr, gr), 1)
    sh = c.bit_length() - 1
    same = (ii >> sh) == (jj >> sh)
    tri = same & ((jj >= ii) if reverse else (jj <= ii))
    tri_b = jnp.where(tri, 1.0, 0.0).astype(BF16)
    g_order = range(gpi - 1, -1, -1) if reverse else range(gpi)
    c_order = range(cpg - 1, -1, -1) if reverse else range(cpg)

    def body(i, carry):
        r0 = 0 if n_it == 1 else pl.multiple_of(((n_it - 1 - i) if reverse else i) * rows_it, rows_it)
        q_all = q_ref[0, pl.ds(r0, rows_it), :]
        k_all = k_ref[0, pl.ds(r0, rows_it), :]
        v_all = v_ref[0, pl.ds(r0, rows_it), :]
        g_all = g_ref[0, pl.ds(r0, rows_it), :]
        prev = acc_ref[pl.ds(r0, rows_it), :] if reverse else None
        st = st_ref[...]
        outs = [None] * gpi
        for gi in g_order:
            rs = slice(gi * gr, (gi + 1) * gr)
            k = k_all[rs]
            v = v_all[rs]
            cum = _chunk_cumsum(g_all[rs], c, reverse)
            cl = [cum[ci * c:ci * c + 1] if reverse else cum[(ci + 1) * c - 1:(ci + 1) * c] for ci in range(cpg)]
            cl_rows = jnp.concatenate([jnp.broadcast_to(x, (c, x.shape[1])) for x in cl], axis=0)
            q_dec = (q_all[rs] * scale * jnp.exp(cum)).astype(BF16)
            k_inv = (k * jnp.exp(-cum)).astype(BF16)
            k_st = (k * jnp.exp(cl_rows - cum)).astype(BF16)
            a = jnp.where(tri, _dot_nt(q_dec, k_inv), 0.0)
            o_intra = _dot(a.astype(BF16), v)
            parts = [None] * cpg
            for ci in c_order:
                cs = slice(ci * c, (ci + 1) * c)
                parts[ci] = o_intra[cs] + _dot_nt(q_dec[cs], st.astype(BF16))
                st = st * jnp.exp(cl[ci]) + _dot_tn(v[cs], k_st[cs])
            outs[gi] = jnp.concatenate(parts, axis=0)
        st_ref[...] = st
        o_all = outs[0] if gpi == 1 else jnp.concatenate(outs, axis=0)
        acc_ref[pl.ds(r0, rows_it), :] = (prev + o_all) if reverse else o_all
        return carry

    if n_it == 1:
        body(0, 0)
    else:
        lax.fori_loop(0, n_it, body, 0)


def _gla_merge(acc_ref, og_ref, gain, o_ref, t):
    tm = min(512, t)
    for r in range(t // tm):
        o = acc_ref[r * tm:(r + 1) * tm, :]
        ms = jnp.mean(o * o, axis=-1, keepdims=True)
        y = (o * lax.rsqrt(ms + EPS) * gain) * _silu(og_ref[0, r * tm:(r + 1) * tm, :])
        o_ref[0, r * tm:(r + 1) * tm, :] = y.astype(BF16)


def _gla_kernel(*refs, n, l, dk, ctx_out):
    q_ref, k_ref, v_ref, gf_ref, gb_ref, og_ref, cq_ref, ck_ref, cv_ref, cgf_ref, cgb_ref, cog_ref, on_ref = refs[:13]
    if ctx_out:
        o_ref, co_ref, acc_ref, cacc_ref, st_ref = refs[13:]
    else:
        o_ref, acc_ref, cacc_ref, st_ref = refs[13:]
    scale = dk ** -0.5
    for reverse, g_ref, cg_ref in ((False, gf_ref, cgf_ref), (True, gb_ref, cgb_ref)):
        st_ref[...] = jnp.zeros_like(st_ref)
        _gla_scan(cq_ref, ck_ref, cv_ref, cg_ref, cacc_ref, st_ref, n_rows=l, reverse=reverse, scale=scale)
        _gla_scan(q_ref, k_ref, v_ref, g_ref, acc_ref, st_ref, n_rows=n, reverse=reverse, scale=scale)
    gain = on_ref[...]
    _gla_merge(acc_ref, og_ref, gain, o_ref, n)
    if ctx_out:
        _gla_merge(cacc_ref, cog_ref, gain, co_ref, l)


def _gla(lat, ctx, onorm, ctx_out):
    q, k, v, og, gf, gb = lat
    cq, ck, cv, cog, cgf, cgb = ctx
    b, n, dkt = q.shape
    l = cq.shape[1]
    dvt = v.shape[2]
    dk = dkt // GLA_HEADS
    dv = dvt // GLA_HEADS

    def spec(t, w):
        return pl.BlockSpec((1, t, w), lambda i, h: (i, 0, h))

    in_specs = [spec(n, dk), spec(n, dk), spec(n, dv), spec(n, dk), spec(n, dk), spec(n, dv),
                spec(l, dk), spec(l, dk), spec(l, dv), spec(l, dk), spec(l, dk), spec(l, dv),
                pl.BlockSpec((1, dv), lambda i, h: (0, 0))]
    out_specs = [spec(n, dv)]
    out_shape = [jax.ShapeDtypeStruct((b, n, dvt), BF16)]
    if ctx_out:
        out_specs.append(spec(l, dv))
        out_shape.append(jax.ShapeDtypeStruct((b, l, dvt), BF16))
    res = pl.pallas_call(
        functools.partial(_gla_kernel, n=n, l=l, dk=dk, ctx_out=ctx_out),
        grid=(b, GLA_HEADS),
        in_specs=in_specs,
        out_specs=out_specs,
        out_shape=out_shape,
        scratch_shapes=[pltpu.VMEM((n, dv), F32), pltpu.VMEM((l, dv), F32), pltpu.VMEM((dv, dk), F32)],
        compiler_params=_cparams(("parallel", "parallel")),
        name="gla",
    )(q, k, v, gf, gb, og, cq, ck, cv, cgf, cgb, cog, onorm.reshape(1, dv))
    return (res[0], res[1]) if ctx_out else (res[0], None)


def _rope_tables(n):
    half = HEAD_DIM // 2
    t = jnp.arange(n, dtype=I32)
    inv = ROPE_THETA ** (-jnp.arange(0, half, 2, dtype=F32) / half)

    def tab(pos):
        ang = pos.astype(F32)[:, None] * inv[None, :]
        ang = jnp.concatenate([ang, ang], axis=-1)
        return jnp.cos(ang), jnp.sin(ang)

    cos_r, sin_r = tab(t // GRID_W)
    cos_c, sin_c = tab(t % GRID_W)
    cos = jnp.concatenate([cos_r, cos_c], axis=-1)
    sin = jnp.concatenate([sin_r, sin_c], axis=-1)
    sign = jnp.where((jnp.arange(HEAD_DIM) % half) < half // 2, -1.0, 1.0).astype(F32)
    reps = LANES // HEAD_DIM
    return jnp.tile(cos, (1, reps)), jnp.tile(sin * sign[None, :], (1, reps))


def _block_ones():
    i = np.arange(LANES)
    return jnp.asarray((i[:, None] // HEAD_DIM) == (i[None, :] // HEAD_DIM), BF16)


def _attn_gains(qa_g, ka_g, qb_g, kb_g):
    qs = HEAD_DIM ** -0.5
    one_a = jnp.ones((A_KV_HEADS * HEAD_DIM,), F32)
    one_b = jnp.ones((B_HEADS * HEAD_DIM,), F32)
    return jnp.concatenate([jnp.tile(qa_g, A_HEADS) * qs, jnp.tile(ka_g, A_KV_HEADS), one_a,
                            jnp.tile(qb_g, B_HEADS) * qs, jnp.tile(kb_g, B_HEADS), one_b]).reshape(1, -1)


def kernel(x, c, ctx, c_ctx, ada_w, ada_b, norm1_g, norm2_g, attn_w_in, attn_w_out, a_q_norm, a_k_norm, b_q_norm,
           b_k_norm, na_rpb, gla_w_in, gla_gk_w_f, gla_gk_b_f, gla_gk_w_b, gla_gk_b_b, gla_o_norm, gla_w_out,
           moe_router, moe_w_gate, moe_w_up, moe_w_down):
    b, n, d = x.shape
    depth = ada_w.shape[0]
    dk = gla_gk_w_f.shape[2]
    dv = gla_w_out.shape[1]
    rows = -(-(b + 1) // 8) * 8
    cvec = jnp.zeros((rows, d), F32).at[:b].set(c).at[b].set(c_ctx)
    mods = _mods(cvec, ada_w, ada_b)
    cos, sin = _rope_tables(n)
    bd = _block_ones()
    h, hc = x, ctx
    for layer in range(depth):
        last = layer == depth - 1
        i = layer // 2
        ml = mods[layer, :b].reshape(b, 6, d)
        mc = mods[layer, b:b + 1].reshape(1, 6, d)
        if layer % 2 == 0:
            w_in = attn_w_in[i].astype(BF16)
            gains = _attn_gains(a_q_norm[i], a_k_norm[i], b_q_norm[i], b_k_norm[i])
            qa, ka, va, qb, kb, vb = _attn_proj(h, ml, norm1_g[layer], w_in, gains, bd, cos, sin)
            cqa, cka, cva, cqb, ckb, cvb = _attn_proj(hc, mc, norm1_g[layer], w_in, gains, bd)
            oa = _gqa(qa, jnp.concatenate([ka, cka], axis=1), jnp.concatenate([va, cva], axis=1), A_HEADS, A_KV_HEADS)
            ob = _na(qb, kb, vb, ckb, cvb, _na_bias_table(na_rpb[i], n // GRID_W))
            o_lat = [oa, ob]
            o_ctx = None if last else [_gqa(cqa, cka, cva, A_HEADS, A_KV_HEADS), _gqa(cqb, ckb, cvb, B_HEADS, B_HEADS)]
            w_out = attn_w_out[i].astype(BF16)
        else:
            w_main = gla_w_in[i][:, :2 * dk + 2 * dv].astype(BF16)
            wz = jnp.pad(gla_w_in[i][:, 2 * dk + 2 * dv:], ((0, 0), (0, LANES - 2 * GLA_GATE_RANK))).astype(BF16)
            wgate = jnp.zeros((LANES, 2 * dk), F32)
            wgate = wgate.at[:GLA_GATE_RANK, :dk].set(gla_gk_w_f[i]).at[GLA_GATE_RANK:2 * GLA_GATE_RANK, dk:].set(gla_gk_w_b[i])
            bgate = jnp.concatenate([gla_gk_b_f[i], gla_gk_b_b[i]]).reshape(1, 2 * dk)
            lat = _gla_proj(h, ml, norm1_g[layer], w_main, wz, wgate.astype(BF16), bgate, dk, dv)
            cx = _gla_proj(hc, mc, norm1_g[layer], w_main, wz, wgate.astype(BF16), bgate, dk, dv)
            o, co = _gla(lat, cx, gla_o_norm[i], not last)
            o_lat = [o]
            o_ctx = None if last else [co]
            w_out = gla_w_out[i].astype(BF16)
        wr_hi, wr_lo = _split2(moe_router[layer].T)
        h1, u2, lg = _outproj(o_lat, h, ml, norm2_g[layer], w_out, wr_hi, wr_lo)
        h = _moe(h1, u2, lg, ml, moe_w_gate, moe_w_up, moe_w_down, layer)
        if not last:
            hc1, uc2, lgc = _outproj(o_ctx, hc, mc, norm2_g[layer], w_out, wr_hi, wr_lo)
            hc = _moe(hc1, uc2, lgc, mc, moe_w_gate, moe_w_up, moe_w_down, layer)
    return h
```

```python
import functools

import numpy as np
import jax
import jax.numpy as jnp
from jax import lax
from jax.experimental import pallas as pl
from jax.experimental.pallas import tpu as pltpu

F32 = jnp.float32
BF16 = jnp.bfloat16
I32 = jnp.int32

GRID_W = 64
HEAD_DIM = 64
A_HEADS = 8
A_KV_HEADS = 2
B_HEADS = 8
NA_ROWS = 8
NA_COLS = 16
ROPE_THETA = 10000.0
GLA_HEADS = 4
GLA_GATE_RANK = 16
GLA_GATE_NORM = 16.0
GLA_CHUNK = 64
GLA_GROUP = 4
GLA_GROUPS_PER_TRIP = 16
N_EXPERTS = 16
EC_CAPACITY = 2
EPS = 1e-6

LANES = 128
BF16_ROWS = 16
NA_QROWS = 4
TOK_BLK = LANES
SMALL_WIN = 48
GATHER_TRIP = 32
MASK_VALUE = -1e30
VMEM_LIMIT = 56 * 1024 * 1024

_NT = (((1,), (1,)), ((), ()))
_TN = (((0,), (0,)), ((), ()))


def _cparams(sem):
    return pltpu.CompilerParams(dimension_semantics=sem, vmem_limit_bytes=VMEM_LIMIT)


def _dot(a, b):
    return jnp.dot(a, b, preferred_element_type=F32)


def _dot_nt(a, b):
    return lax.dot_general(a, b, _NT, preferred_element_type=F32)


def _dot_tn(a, b):
    return lax.dot_general(a, b, _TN, preferred_element_type=F32)


def _split2(a):
    hi = a.astype(BF16)
    lo = (a - hi.astype(F32)).astype(BF16)
    return hi, lo


def _silu(a):
    return a / (1.0 + jnp.exp(-a))


def _rms_mod(x, g, shift, scale):
    ms = jnp.mean(x * x, axis=-1, keepdims=True)
    return (x * lax.rsqrt(ms + EPS) * g) * (1.0 + scale) + shift


def _mods_kernel(c_ref, w_ref, b_ref, o_ref):
    c = c_ref[...]
    s_hi, s_lo = _split2(_silu(c))
    w_hi, w_lo = _split2(w_ref[0])
    o_ref[0] = _dot(s_hi, w_hi) + _dot(s_lo, w_hi) + _dot(s_hi, w_lo) + b_ref[0]


def _mods(cvec, ada_w, ada_b):
    depth, d, d6 = ada_w.shape
    rows = cvec.shape[0]
    tn = 1536 if d6 % 1536 == 0 else d6
    return pl.pallas_call(
        _mods_kernel,
        grid=(depth, d6 // tn),
        in_specs=[pl.BlockSpec((rows, d), lambda l, j: (0, 0)),
                  pl.BlockSpec((1, d, tn), lambda l, j: (l, 0, j)),
                  pl.BlockSpec((1, 1, tn), lambda l, j: (l, 0, j))],
        out_specs=pl.BlockSpec((1, rows, tn), lambda l, j: (l, 0, j)),
        out_shape=jax.ShapeDtypeStruct((depth, rows, d6), F32),
        compiler_params=_cparams(("arbitrary", "arbitrary")),
        name="mods",
    )(cvec, ada_w, ada_b.reshape(depth, 1, d6))


_ATTN_COLS = (A_HEADS + 2 * A_KV_HEADS + 3 * B_HEADS) * HEAD_DIM
_QA = (0, 4)
_KA = (4, 5)
_VA = (5, 6)
_QB = (6, 10)
_KB = (10, 14)
_VB = (14, 18)


def _attn_proj_kernel(*refs, rope):
    if rope:
        x_ref, mod_ref, g1_ref, w_ref, gain_ref, bd_ref, cos_ref, sin_ref = refs[:8]
        outs = refs[8:]
    else:
        x_ref, mod_ref, g1_ref, w_ref, gain_ref, bd_ref = refs[:6]
        outs = refs[6:]
    qa_ref, ka_ref, va_ref, qb_ref, kb_ref, vb_ref = outs
    m = mod_ref[0]
    u = _rms_mod(x_ref[0], g1_ref[...], m[0:1], m[1:2]).astype(BF16)
    acc = _dot(u, w_ref[...])
    bd = bd_ref[...]
    if rope:
        cos = cos_ref[...]
        sin = sin_ref[...]
        lane = lax.broadcasted_iota(I32, cos.shape, 1)
        first = (lane & 31) < 16

    def chunk(j, norm, rot):
        cch = acc[:, j * LANES:(j + 1) * LANES]
        if norm:
            sq_hi, sq_lo = _split2(cch * cch)
            ss = _dot(sq_hi, bd) + _dot(sq_lo, bd)
            cch = cch * lax.rsqrt(ss * (1.0 / HEAD_DIM) + EPS) * gain_ref[:, j * LANES:(j + 1) * LANES]
        if rot:
            partner = jnp.where(first, pltpu.roll(cch, LANES - 16, 1), pltpu.roll(cch, 16, 1))
            cch = cch * cos + partner * sin
        return cch.astype(BF16)

    for (lo, hi), ref, norm, rot in ((_QA, qa_ref, True, rope), (_KA, ka_ref, True, rope), (_VA, va_ref, False, False),
                                     (_QB, qb_ref, True, False), (_KB, kb_ref, True, False), (_VB, vb_ref, False, False)):
        for j in range(lo, hi):
            ref[0, :, (j - lo) * LANES:(j - lo + 1) * LANES] = chunk(j, norm, rot)


def _attn_proj(h, mod, g1, w, gains, bd, cos=None, sin=None):
    b, t, d = h.shape
    tm = min(512, t)
    rope = cos is not None
    nb_mod = mod.shape[0]
    midx = (lambda i, j: (i, 0, 0)) if nb_mod > 1 else (lambda i, j: (0, 0, 0))
    in_specs = [pl.BlockSpec((1, tm, d), lambda i, j: (i, j, 0)),
                pl.BlockSpec((1, 6, d), midx),
                pl.BlockSpec((1, d), lambda i, j: (0, 0)),
                pl.BlockSpec((d, _ATTN_COLS), lambda i, j: (0, 0)),
                pl.BlockSpec((1, _ATTN_COLS), lambda i, j: (0, 0)),
                pl.BlockSpec((LANES, LANES), lambda i, j: (0, 0))]
    args = [h, mod, g1.reshape(1, d), w, gains, bd]
    if rope:
        in_specs += [pl.BlockSpec((tm, LANES), lambda i, j: (j, 0))] * 2
        args += [cos, sin]
    widths = [(hi - lo) * LANES for lo, hi in (_QA, _KA, _VA, _QB, _KB, _VB)]
    return pl.pallas_call(
        functools.partial(_attn_proj_kernel, rope=rope),
        grid=(b, t // tm),
        in_specs=in_specs,
        out_specs=[pl.BlockSpec((1, tm, wd), lambda i, j: (i, j, 0)) for wd in widths],
        out_shape=[jax.ShapeDtypeStruct((b, t, wd), BF16) for wd in widths],
        compiler_params=_cparams(("parallel", "parallel")),
        name="attn_proj_rope" if rope else "attn_proj",
    )(*args)


def _gqa_kernel(q_ref, k_ref, v_ref, o_ref, *, hq, hk):
    grp = hq // hk
    for h in range(hq):
        kv = h // grp
        q = q_ref[0, :, h * HEAD_DIM:(h + 1) * HEAD_DIM]
        k = k_ref[0, :, kv * HEAD_DIM:(kv + 1) * HEAD_DIM]
        v = v_ref[0, :, kv * HEAD_DIM:(kv + 1) * HEAD_DIM]
        s = _dot_nt(q, k)
        p = jnp.exp(s - jnp.max(s, axis=-1, keepdims=True))
        l = jnp.sum(p, axis=-1, keepdims=True)
        o = _dot(p.astype(BF16), v) / l
        o_ref[0, :, h * HEAD_DIM:(h + 1) * HEAD_DIM] = o.astype(BF16)


def _gqa(q, k, v, hq, hk):
    b, t, _ = q.shape
    s = k.shape[1]
    tq = min(512, t)
    return pl.pallas_call(
        functools.partial(_gqa_kernel, hq=hq, hk=hk),
        grid=(b, t // tq),
        in_specs=[pl.BlockSpec((1, tq, hq * HEAD_DIM), lambda i, j: (i, j, 0)),
                  pl.BlockSpec((1, s, hk * HEAD_DIM), lambda i, j: (i, 0, 0)),
                  pl.BlockSpec((1, s, hk * HEAD_DIM), lambda i, j: (i, 0, 0))],
        out_specs=pl.BlockSpec((1, tq, hq * HEAD_DIM), lambda i, j: (i, j, 0)),
        out_shape=jax.ShapeDtypeStruct((b, t, hq * HEAD_DIM), BF16),
        compiler_params=_cparams(("parallel", "parallel")),
        name="gqa",
    )(q, k, v)


def _na_geometry(rows):
    kr = min(NA_QROWS + NA_ROWS - 1, rows)
    nblk = rows // NA_QROWS
    return kr, nblk


def _na_bias_table(rpb, rows):
    kr, nblk = _na_geometry(rows)
    wr = min(NA_ROWS, rows)
    heads = rpb.shape[0]
    qc = np.arange(GRID_W)[:, None]
    kc = np.arange(GRID_W)[None, :]
    cs = np.clip(qc - NA_COLS // 2, 0, GRID_W - NA_COLS)
    col_ok = (kc >= cs) & (kc < cs + NA_COLS)
    pick = (np.arange(2 * NA_COLS - 1)[:, None, None] == (kc - qc + NA_COLS - 1)[None]) & col_ok[None]
    toeplitz = jnp.einsum("hrd,dqk->hrqk", rpb.astype(F32), jnp.asarray(pick, F32), precision=lax.Precision.HIGHEST)
    toeplitz = jnp.where(col_ok[None, None], toeplitz, MASK_VALUE)
    tabs = []
    for rb in (0, min(1, nblk - 1), nblk - 1):
        qr0 = rb * NA_QROWS
        kr0 = int(np.clip(qr0 - NA_ROWS // 2, 0, rows - kr))
        qr = qr0 + np.arange(NA_QROWS)[:, None]
        krr = kr0 + np.arange(kr)[None, :]
        rs = np.clip(qr - wr // 2, 0, rows - wr)
        row_ok = (krr >= rs) & (krr < rs + wr)
        dr = krr - qr + (NA_ROWS - 1)
        masked = jnp.full((heads, GRID_W, GRID_W), MASK_VALUE, F32)
        blocks = [jnp.concatenate([toeplitz[:, dr[i, j]] if row_ok[i, j] else masked for j in range(kr)], axis=2)
                  for i in range(NA_QROWS)]
        tabs.append(jnp.concatenate(blocks, axis=1))
    return jnp.stack(tabs)


def _na_kernel(q_ref, k_ref, v_ref, kc_ref, vc_ref, bias_ref, o_ref, *, rows, heads):
    kr, _ = _na_geometry(rows)
    rb = pl.program_id(1)
    kr0 = jnp.clip(rb * NA_QROWS - NA_ROWS // 2, 0, rows - kr)
    start = pl.multiple_of(kr0 * GRID_W, GRID_W)
    kw = k_ref[0, pl.ds(start, kr * GRID_W), :]
    vw = v_ref[0, pl.ds(start, kr * GRID_W), :]
    for h in range(heads):
        sl = slice(h * HEAD_DIM, (h + 1) * HEAD_DIM)
        q = q_ref[0, :, sl]
        s_w = _dot_nt(q, kw[:, sl]) + bias_ref[0, h]
        s_c = _dot_nt(q, kc_ref[0, :, sl])
        m = jnp.maximum(jnp.max(s_w, axis=-1, keepdims=True), jnp.max(s_c, axis=-1, keepdims=True))
        p_w = jnp.exp(s_w - m)
        p_c = jnp.exp(s_c - m)
        l = jnp.sum(p_w, axis=-1, keepdims=True) + jnp.sum(p_c, axis=-1, keepdims=True)
        o = (_dot(p_w.astype(BF16), vw[:, sl]) + _dot(p_c.astype(BF16), vc_ref[0, :, sl])) / l
        o_ref[0, :, sl] = o.astype(BF16)


def _na(q, k, v, kc, vc, bias):
    b, n, c = q.shape
    l = kc.shape[1]
    rows = n // GRID_W
    kr, nblk = _na_geometry(rows)
    tq = NA_QROWS * GRID_W
    heads = c // HEAD_DIM

    def pattern(i, j):
        return (jnp.where(j == 0, 0, jnp.where(j == nblk - 1, 2, 1)), 0, 0, 0)

    return pl.pallas_call(
        functools.partial(_na_kernel, rows=rows, heads=heads),
        grid=(b, nblk),
        in_specs=[pl.BlockSpec((1, tq, c), lambda i, j: (i, j, 0)),
                  pl.BlockSpec((1, n, c), lambda i, j: (i, 0, 0)),
                  pl.BlockSpec((1, n, c), lambda i, j: (i, 0, 0)),
                  pl.BlockSpec((1, l, c), lambda i, j: (i, 0, 0)),
                  pl.BlockSpec((1, l, c), lambda i, j: (i, 0, 0)),
                  pl.BlockSpec((1, heads, tq, kr * GRID_W), pattern)],
        out_specs=pl.BlockSpec((1, tq, c), lambda i, j: (i, j, 0)),
        out_shape=jax.ShapeDtypeStruct((b, n, c), BF16),
        compiler_params=_cparams(("parallel", "arbitrary")),
        name="na_attn",
    )(q, k, v, kc, vc, bias)


def _outproj_kernel(*refs, n_o, route):
    o_refs = refs[:n_o]
    h_ref, mod_ref, g2_ref, w_ref = refs[n_o:n_o + 4]
    rest = refs[n_o + 4:]
    if route:
        wr_hi_ref, wr_lo_ref, hn_ref, u2_ref, lg_ref = rest
    else:
        hn_ref, = rest
    m = mod_ref[0]
    o = o_refs[0][0] if n_o == 1 else jnp.concatenate([r[0] for r in o_refs], axis=1)
    hn = h_ref[0] + m[2:3] * _dot(o, w_ref[...])
    hn_ref[0] = hn
    if route:
        u2 = _rms_mod(hn, g2_ref[...], m[3:4], m[4:5])
        u_hi, u_lo = _split2(u2)
        u2_ref[0] = u_hi
        wr_hi = wr_hi_ref[...]
        both = _dot_nt(jnp.concatenate([wr_hi, wr_lo_ref[...]], axis=0), u_hi)
        lg_ref[0] = both[:N_EXPERTS] + both[N_EXPERTS:] + _dot_nt(wr_hi, u_lo)


def _outproj(o_list, h, mod, g2, w, wr_hi=None, wr_lo=None):
    b, t, d = h.shape
    tm = min(512, t)
    route = wr_hi is not None
    nb_mod = mod.shape[0]
    midx = (lambda i, j: (i, 0, 0)) if nb_mod > 1 else (lambda i, j: (0, 0, 0))
    in_specs = [pl.BlockSpec((1, tm, o.shape[2]), lambda i, j: (i, j, 0)) for o in o_list]
    in_specs += [pl.BlockSpec((1, tm, d), lambda i, j: (i, j, 0)),
                 pl.BlockSpec((1, 6, d), midx),
                 pl.BlockSpec((1, d), lambda i, j: (0, 0)),
                 pl.BlockSpec((d, d), lambda i, j: (0, 0))]
    args = list(o_list) + [h, mod, g2.reshape(1, d), w]
    out_specs = [pl.BlockSpec((1, tm, d), lambda i, j: (i, j, 0))]
    out_shape = [jax.ShapeDtypeStruct((b, t, d), F32)]
    if route:
        in_specs += [pl.BlockSpec((N_EXPERTS, d), lambda i, j: (0, 0))] * 2
        args += [wr_hi, wr_lo]
        out_specs += [pl.BlockSpec((1, tm, d), lambda i, j: (i, j, 0)),
                      pl.BlockSpec((1, N_EXPERTS, tm), lambda i, j: (i, 0, j))]
        out_shape += [jax.ShapeDtypeStruct((b, t, d), BF16), jax.ShapeDtypeStruct((b, N_EXPERTS, t), F32)]
    return pl.pallas_call(
        functools.partial(_outproj_kernel, n_o=len(o_list), route=route),
        grid=(b, t // tm),
        in_specs=in_specs,
        out_specs=out_specs,
        out_shape=out_shape,
        compiler_params=_cparams(("parallel", "parallel")),
        name="outproj",
    )(*args)


def _cumsum_excl(x):
    e, n = x.shape
    nb = n // LANES
    st = jnp.concatenate([x[:, j * LANES:(j + 1) * LANES] for j in range(nb)], axis=0)
    ii = lax.broadcasted_iota(I32, (LANES, LANES), 0)
    jj = lax.broadcasted_iota(I32, (LANES, LANES), 1)
    incl = _dot(st.astype(BF16), jnp.where(ii <= jj, 1.0, 0.0).astype(BF16))
    tot = jnp.broadcast_to(incl[:, LANES - 1:LANES], incl.shape)
    r = lax.broadcasted_iota(I32, (nb * e, nb * e), 0)
    c = lax.broadcasted_iota(I32, (nb * e, nb * e), 1)
    sh = e.bit_length() - 1
    same = (r & (e - 1)) == (c & (e - 1))
    prev = jnp.where(same & (lax.shift_right_logical(c, sh) < lax.shift_right_logical(r, sh)), 1.0, 0.0).astype(BF16)
    starts = _dot(prev, tot.astype(BF16))
    excl_st = incl - st + starts
    excl = jnp.concatenate([excl_st[j * e:(j + 1) * e, :] for j in range(nb)], axis=1)
    return excl, starts


def _route_kernel(lg_ref, dest_ref, gate_ref, offs_ref, *, cap):
    x = lg_ref[0]
    ex = jnp.exp(x - jnp.max(x, axis=0, keepdims=True))
    aff = ex / jnp.sum(ex, axis=0, keepdims=True)

    def enough(cand):
        return jnp.sum(jnp.where(aff >= lax.bitcast_convert_type(cand, F32), 1.0, 0.0), axis=1, keepdims=True) >= cap

    def search(i, cur):
        sh = 28 - 2 * i
        c1, c2, c3 = (cur | lax.shift_left(jnp.int32(k), sh) for k in (1, 2, 3))
        return jnp.where(enough(c3), c3, jnp.where(enough(c2), c2, jnp.where(enough(c1), c1, cur)))

    top = jnp.full((x.shape[0], 1), 1 << 30, I32)
    thr = jnp.where(enough(top), top, jnp.zeros_like(top))
    thr = lax.bitcast_convert_type(lax.fori_loop(0, 15, search, thr), F32)
    gt = aff > thr
    eq = aff == thr
    need = cap - jnp.sum(jnp.where(gt, 1.0, 0.0), axis=1, keepdims=True)
    rank_eq, _ = _cumsum_excl(jnp.where(eq, 1.0, 0.0))
    sel = gt | (eq & (rank_eq < need))
    pos, starts = _cumsum_excl(jnp.where(sel, 1.0, 0.0))
    dest_ref[0] = jnp.where(sel, pos.astype(I32), -1)
    gate_ref[0] = jnp.where(sel, aff, 0.0)
    offs_ref[0] = starts.astype(I32)


def _route(lg, cap):
    b, e, n = lg.shape
    nb = n // LANES
    return pl.pallas_call(
        functools.partial(_route_kernel, cap=cap),
        grid=(b,),
        in_specs=[pl.BlockSpec((1, e, n), lambda i: (i, 0, 0))],
        out_specs=[pl.BlockSpec((1, e, n), lambda i: (i, 0, 0)),
                   pl.BlockSpec((1, e, n), lambda i: (i, 0, 0)),
                   pl.BlockSpec((1, nb * e, LANES), lambda i: (i, 0, 0))],
        out_shape=[jax.ShapeDtypeStruct((b, e, n), I32), jax.ShapeDtypeStruct((b, e, n), F32),
                   jax.ShapeDtypeStruct((b, nb * e, LANES), I32)],
        compiler_params=_cparams(("parallel",)),
        name="route",
    )(lg)


def _blk_range(offs_ref, b, j, e, nblk, cap):
    base = (b * nblk + j) * N_EXPERTS + e
    last = (b * nblk + nblk - 1) * N_EXPERTS + e
    return offs_ref[base], jnp.where(j + 1 < nblk, offs_ref[jnp.minimum(base + N_EXPERTS, last)], cap)


def _win_start(off, cap, win):
    aligned = lax.shift_right_logical(off, BF16_ROWS.bit_length() - 1) * BF16_ROWS
    return pl.multiple_of(jnp.minimum(aligned, cap - win), BF16_ROWS)


def _onehot(dest_row, off_al, win):
    rows = lax.broadcasted_iota(I32, (win, TOK_BLK), 0) + off_al
    return jnp.where(rows == dest_row, 1.0, 0.0)


def _moe_ffn_kernel(offs_ref, u_ref, dest_ref, gate_ref, wg_ref, wu_ref, wd_ref, y_ref,
                    xg_ref, gb_ref, wgb_ref, wub_ref, wdb_ref, *, cap, nblk, win, small, nbatch, batch_ffn):
    e = pl.program_id(0)

    def load_weights():
        wgb_ref[...] = wg_ref[0, 0].astype(BF16)
        wub_ref[...] = wu_ref[0, 0].astype(BF16)
        wdb_ref[...] = wd_ref[0, 0].astype(BF16)

    trip = min(GATHER_TRIP, nblk)

    def gather_sample(b, blk, row0):
        xg_ref[pl.ds(row0, cap), :] = jnp.zeros((cap, xg_ref.shape[1]), F32)
        gb_ref[pl.ds(row0, cap), :] = jnp.zeros((cap, LANES), F32)

        def gather(t, carry):
            js = [t * trip + i for i in range(trip)]
            ranges = [_blk_range(offs_ref, b, j, e, nblk, cap) for j in js]

            def run(w):
                for j, (off, _) in zip(js, ranges):
                    st = _win_start(off, cap, w)
                    g = _onehot(dest_ref[blk, 0, pl.ds(j, 1), :], st, w)
                    tok0 = j * TOK_BLK if isinstance(j, int) else pl.multiple_of(j * TOK_BLK, TOK_BLK)
                    uj = u_ref[blk, pl.ds(tok0, TOK_BLK), :]
                    r = pl.multiple_of(row0 + st, BF16_ROWS)
                    xg_ref[pl.ds(r, w), :] += _dot(g.astype(BF16), uj)
                    gsum = jnp.sum(g * gate_ref[blk, 0, pl.ds(j, 1), :], axis=1, keepdims=True)
                    gb_ref[pl.ds(r, w), :] += jnp.broadcast_to(gsum, (w, LANES))

            if small == win:
                run(win)
            else:
                fits = ranges[0][1] <= _win_start(ranges[0][0], cap, small) + small
                for off, end in ranges[1:]:
                    fits = fits & (end <= _win_start(off, cap, small) + small)
                pl.when(fits)(functools.partial(run, small))
                pl.when(jnp.logical_not(fits))(functools.partial(run, win))
            return carry

        if nblk == trip:
            gather(0, 0)
        else:
            lax.fori_loop(0, nblk // trip, gather, 0)

    def ffn(m):
        x = xg_ref[0:m, :].astype(BF16)
        ff = wgb_ref.shape[1]
        fc = min(512, ff)
        acc = jnp.zeros((m, wdb_ref.shape[1]), F32)
        for c in range(ff // fc):
            a = _dot(x, wgb_ref[:, c * fc:(c + 1) * fc])
            up = _dot(x, wub_ref[:, c * fc:(c + 1) * fc])
            acc = acc + _dot((_silu(a) * up).astype(BF16), wdb_ref[c * fc:(c + 1) * fc, :])
        return (acc * gb_ref[0:m, 0:1]).astype(BF16)

    if batch_ffn:
        load_weights()
        for bb in range(nbatch):
            gather_sample(bb, bb, bb * cap)
        y = ffn(nbatch * cap)
        for bb in range(nbatch):
            y_ref[bb, 0] = y[bb * cap:(bb + 1) * cap]
    else:
        b = pl.program_id(1)
        pl.when(b == 0)(load_weights)
        gather_sample(b, 0, 0)
        y_ref[0, 0] = ffn(cap)


def _moe_ffn(offs, u, dest4, gate4, wg, wu, wd, layer, cap):
    b, n, d = u.shape
    nblk = n // TOK_BLK
    win = min(TOK_BLK + BF16_ROWS, cap)
    ff = wg.shape[3]
    batch_ffn = cap < 256
    if batch_ffn:
        grid = (N_EXPERTS,)
        rows = b * cap
        tok_spec = pl.BlockSpec((b, n, d), lambda e, o: (0, 0, 0))
        sel_spec = pl.BlockSpec((b, 1, nblk, TOK_BLK), lambda e, o: (0, e, 0, 0))
        y_spec = pl.BlockSpec((b, 1, cap, d), lambda e, o: (0, e, 0, 0))

        def w_spec(r, c):
            return pl.BlockSpec((1, 1, r, c), lambda e, o: (layer, e, 0, 0))
    else:
        grid = (N_EXPERTS, b)
        rows = cap
        tok_spec = pl.BlockSpec((1, n, d), lambda e, i, o: (i, 0, 0))
        sel_spec = pl.BlockSpec((1, 1, nblk, TOK_BLK), lambda e, i, o: (i, e, 0, 0))
        y_spec = pl.BlockSpec((1, 1, cap, d), lambda e, i, o: (i, e, 0, 0))

        def w_spec(r, c):
            return pl.BlockSpec((1, 1, r, c), lambda e, i, o: (layer, e, 0, 0))
    return pl.pallas_call(
        functools.partial(_moe_ffn_kernel, cap=cap, nblk=nblk, win=win, small=min(SMALL_WIN, cap), nbatch=b,
                          batch_ffn=batch_ffn),
        grid_spec=pltpu.PrefetchScalarGridSpec(
            num_scalar_prefetch=1,
            grid=grid,
            in_specs=[tok_spec, sel_spec, sel_spec, w_spec(d, ff), w_spec(d, ff), w_spec(ff, d)],
            out_specs=y_spec,
            scratch_shapes=[pltpu.VMEM((rows, d), F32), pltpu.VMEM((rows, LANES), F32),
                            pltpu.VMEM((d, ff), BF16), pltpu.VMEM((d, ff), BF16), pltpu.VMEM((ff, d), BF16)]),
        out_shape=jax.ShapeDtypeStruct((b, N_EXPERTS, cap, d), BF16),
        compiler_params=_cparams(("arbitrary",) * len(grid)),
        name="moe_ffn",
    )(offs, u, dest4, gate4, wg, wu, wd)


def _moe_combine_kernel(offs_ref, y_ref, dest_ref, h_ref, mod_ref, out_ref, *, cap, nblk, win, small, nsub):
    b = pl.program_id(0)
    jb = pl.program_id(1)
    gate = mod_ref[0][5:6]
    ranges = [[_blk_range(offs_ref, b, jb * nsub + s, e, nblk, cap) for e in range(N_EXPERTS)] for s in range(nsub)]

    def stacked():
        for s in range(nsub):
            rows = slice(s * TOK_BLK, (s + 1) * TOK_BLK)
            starts = [_win_start(off, cap, small) for off, _ in ranges[s]]
            ycat = jnp.concatenate([y_ref[0, e, pl.ds(starts[e], small), :] for e in range(N_EXPERTS)], axis=0)
            gcat = jnp.concatenate([_onehot(dest_ref[0, e, s:s + 1, :], starts[e], small) for e in range(N_EXPERTS)], axis=0)
            out_ref[0, rows, :] = h_ref[0, rows, :] + gate * _dot_tn(gcat.astype(BF16), ycat)

    def per_expert():
        for s in range(nsub):
            rows = slice(s * TOK_BLK, (s + 1) * TOK_BLK)
            acc = jnp.zeros((TOK_BLK, out_ref.shape[2]), F32)
            for e in range(N_EXPERTS):
                st = _win_start(ranges[s][e][0], cap, win)
                g = _onehot(dest_ref[0, e, s:s + 1, :], st, win).astype(BF16)
                acc = acc + _dot_tn(g, y_ref[0, e, pl.ds(st, win), :])
            out_ref[0, rows, :] = h_ref[0, rows, :] + gate * acc

    if small == win:
        stacked()
    else:
        fits = None
        for per_block in ranges:
            for off, end in per_block:
                ok = end <= _win_start(off, cap, small) + small
                fits = ok if fits is None else fits & ok
        pl.when(fits)(stacked)
        pl.when(jnp.logical_not(fits))(per_expert)


def _moe_combine(offs, y, dest4, h, mod, cap):
    b, n, d = h.shape
    nblk = n // TOK_BLK
    win = min(TOK_BLK + BF16_ROWS, cap)
    small = min(SMALL_WIN, cap)
    nsub = min(8, nblk)
    tb = nsub * TOK_BLK
    nb_mod = mod.shape[0]
    midx = (lambda i, j, o: (i, 0, 0)) if nb_mod > 1 else (lambda i, j, o: (0, 0, 0))
    return pl.pallas_call(
        functools.partial(_moe_combine_kernel, cap=cap, nblk=nblk, win=win, small=small, nsub=nsub),
        grid_spec=pltpu.PrefetchScalarGridSpec(
            num_scalar_prefetch=1,
            grid=(b, n // tb),
            in_specs=[pl.BlockSpec((1, N_EXPERTS, cap, d), lambda i, j, o: (i, 0, 0, 0), pipeline_mode=pl.Buffered(1)),
                      pl.BlockSpec((1, N_EXPERTS, nsub, TOK_BLK), lambda i, j, o: (i, 0, j, 0)),
                      pl.BlockSpec((1, tb, d), lambda i, j, o: (i, j, 0)),
                      pl.BlockSpec((1, 6, d), midx)],
            out_specs=pl.BlockSpec((1, tb, d), lambda i, j, o: (i, j, 0))),
        out_shape=jax.ShapeDtypeStruct((b, n, d), F32),
        compiler_params=_cparams(("arbitrary", "arbitrary")),
        name="moe_combine",
    )(offs, y, dest4, h, mod)


def _moe(h, u2, lg, mod, wg, wu, wd, layer):
    b, n, d = h.shape
    cap = EC_CAPACITY * n // N_EXPERTS
    nblk = n // TOK_BLK
    dest, gate, starts = _route(lg, cap)
    offs = starts[:, :, 0].reshape(-1)
    dest4 = dest.reshape(b, N_EXPERTS, nblk, TOK_BLK)
    gate4 = gate.reshape(b, N_EXPERTS, nblk, TOK_BLK)
    y = _moe_ffn(offs, u2, dest4, gate4, wg, wu, wd, layer, cap)
    return _moe_combine(offs, y, dest4, h, mod, cap)


def _gla_proj_kernel(x_ref, mod_ref, g1_ref, w_ref, wz_ref, wgate_ref, bgate_ref,
                     q_ref, k_ref, v_ref, og_ref, gf_ref, gb_ref, *, dk, dv):
    m = mod_ref[0]
    u = _rms_mod(x_ref[0], g1_ref[...], m[0:1], m[1:2]).astype(BF16)
    acc = _dot(u, w_ref[...])
    z = _dot(u, wz_ref[...]).astype(BF16)
    gp = _dot(z, wgate_ref[...]) + bgate_ref[...]
    g = (jnp.minimum(gp, 0.0) - jnp.log1p(jnp.exp(-jnp.abs(gp)))) * (1.0 / GLA_GATE_NORM)
    q_ref[0] = acc[:, :dk]
    k_ref[0] = acc[:, dk:2 * dk]
    v_ref[0] = acc[:, 2 * dk:2 * dk + dv].astype(BF16)
    og_ref[0] = acc[:, 2 * dk + dv:]
    gf_ref[0] = g[:, :dk]
    gb_ref[0] = g[:, dk:]


def _gla_proj(h, mod, g1, w, wz, wgate, bgate, dk, dv):
    b, t, d = h.shape
    tm = min(512, t)
    nb_mod = mod.shape[0]
    midx = (lambda i, j: (i, 0, 0)) if nb_mod > 1 else (lambda i, j: (0, 0, 0))
    cols = 2 * dk + 2 * dv
    widths = [(dk, F32), (dk, F32), (dv, BF16), (dv, F32), (dk, F32), (dk, F32)]
    return pl.pallas_call(
        functools.partial(_gla_proj_kernel, dk=dk, dv=dv),
        grid=(b, t // tm),
        in_specs=[pl.BlockSpec((1, tm, d), lambda i, j: (i, j, 0)),
                  pl.BlockSpec((1, 6, d), midx),
                  pl.BlockSpec((1, d), lambda i, j: (0, 0)),
                  pl.BlockSpec((d, cols), lambda i, j: (0, 0)),
                  pl.BlockSpec((d, LANES), lambda i, j: (0, 0)),
                  pl.BlockSpec((LANES, 2 * dk), lambda i, j: (0, 0)),
                  pl.BlockSpec((1, 2 * dk), lambda i, j: (0, 0))],
        out_specs=[pl.BlockSpec((1, tm, wd), lambda i, j: (i, j, 0)) for wd, _ in widths],
        out_shape=[jax.ShapeDtypeStruct((b, t, wd), dt) for wd, dt in widths],
        compiler_params=_cparams(("parallel", "parallel")),
        name="gla_proj",
    )(h, mod, g1.reshape(1, d), w, wz, wgate, bgate)


def _chunk_cumsum(x, c, reverse):
    rows = x.shape[0]
    pos = lax.broadcasted_iota(I32, x.shape, 0) & (c - 1)
    s = 1
    while s < c:
        if reverse:
            x = x + jnp.where(pos < c - s, pltpu.roll(x, rows - s, 0), 0.0)
        else:
            x = x + jnp.where(pos >= s, pltpu.roll(x, s, 0), 0.0)
        s *= 2
    return x


def _gla_scan(q_ref, k_ref, v_ref, g_ref, acc_ref, st_ref, *, n_rows, reverse, scale):
    c = GLA_CHUNK
    gr = min(GLA_GROUP * c, n_rows)
    cpg = gr // c
    gpi = min(GLA_GROUPS_PER_TRIP, n_rows // gr)
    rows_it = gpi * gr
    n_it = n_rows // rows_it
    ii = lax.broadcasted_iota(I32, (gr, gr), 0)
    jj = lax.broadcasted_iota(I32, (gr, gr), 1)
    sh = c.bit_length() - 1
    same = (ii >> sh) == (jj >> sh)
    tri = same & ((jj >= ii) if reverse else (jj <= ii))
    tri_b = jnp.where(tri, 1.0, 0.0).astype(BF16)
    g_order = range(gpi - 1, -1, -1) if reverse else range(gpi)
    c_order = range(cpg - 1, -1, -1) if reverse else range(cpg)

    def body(i, carry):
        r0 = 0 if n_it == 1 else pl.multiple_of(((n_it - 1 - i) if reverse else i) * rows_it, rows_it)
        q_all = q_ref[0, pl.ds(r0, rows_it), :]
        k_all = k_ref[0, pl.ds(r0, rows_it), :]
        v_all = v_ref[0, pl.ds(r0, rows_it), :]
        g_all = g_ref[0, pl.ds(r0, rows_it), :]
        prev = acc_ref[pl.ds(r0, rows_it), :] if reverse else None
        st = st_ref[...]
        outs = [None] * gpi
        for gi in g_order:
            rs = slice(gi * gr, (gi + 1) * gr)
            k = k_all[rs]
            v = v_all[rs]
            cum = _chunk_cumsum(g_all[rs], c, reverse)
            cl = [cum[ci * c:ci * c + 1] if reverse else cum[(ci + 1) * c - 1:(ci + 1) * c] for ci in range(cpg)]
            cl_rows = jnp.concatenate([jnp.broadcast_to(x, (c, x.shape[1])) for x in cl], axis=0)
            q_dec = (q_all[rs] * scale * jnp.exp(cum)).astype(BF16)
            k_inv = (k * jnp.exp(-cum)).astype(BF16)
            k_st = (k * jnp.exp(cl_rows - cum)).astype(BF16)
            a = jnp.where(tri, _dot_nt(q_dec, k_inv), 0.0)
            o_intra = _dot(a.astype(BF16), v)
            parts = [None] * cpg
            for ci in c_order:
                cs = slice(ci * c, (ci + 1) * c)
                parts[ci] = o_intra[cs] + _dot_nt(q_dec[cs], st.astype(BF16))
                st = st * jnp.exp(cl[ci]) + _dot_tn(v[cs], k_st[cs])
            outs[gi] = jnp.concatenate(parts, axis=0)
        st_ref[...] = st
        o_all = outs[0] if gpi == 1 else jnp.concatenate(outs, axis=0)
        acc_ref[pl.ds(r0, rows_it), :] = (prev + o_all) if reverse else o_all
        return carry

    if n_it == 1:
        body(0, 0)
    else:
        lax.fori_loop(0, n_it, body, 0)


def _gla_merge(acc_ref, og_ref, gain, o_ref, t):
    tm = min(512, t)
    for r in range(t // tm):
        o = acc_ref[r * tm:(r + 1) * tm, :]
        ms = jnp.mean(o * o, axis=-1, keepdims=True)
        y = (o * lax.rsqrt(ms + EPS) * gain) * _silu(og_ref[0, r * tm:(r + 1) * tm, :])
        o_ref[0, r * tm:(r + 1) * tm, :] = y.astype(BF16)


def _gla_kernel(*refs, n, l, dk, ctx_out):
    q_ref, k_ref, v_ref, gf_ref, gb_ref, og_ref, cq_ref, ck_ref, cv_ref, cgf_ref, cgb_ref, cog_ref, on_ref = refs[:13]
    if ctx_out:
        o_ref, co_ref, acc_ref, cacc_ref, st_ref = refs[13:]
    else:
        o_ref, acc_ref, cacc_ref, st_ref = refs[13:]
    scale = dk ** -0.5
    for reverse, g_ref, cg_ref in ((False, gf_ref, cgf_ref), (True, gb_ref, cgb_ref)):
        st_ref[...] = jnp.zeros_like(st_ref)
        _gla_scan(cq_ref, ck_ref, cv_ref, cg_ref, cacc_ref, st_ref, n_rows=l, reverse=reverse, scale=scale)
        _gla_scan(q_ref, k_ref, v_ref, g_ref, acc_ref, st_ref, n_rows=n, reverse=reverse, scale=scale)
    gain = on_ref[...]
    _gla_merge(acc_ref, og_ref, gain, o_ref, n)
    if ctx_out:
        _gla_merge(cacc_ref, cog_ref, gain, co_ref, l)


def _gla(lat, ctx, onorm, ctx_out):
    q, k, v, og, gf, gb = lat
    cq, ck, cv, cog, cgf, cgb = ctx
    b, n, dkt = q.shape
    l = cq.shape[1]
    dvt = v.shape[2]
    dk = dkt // GLA_HEADS
    dv = dvt // GLA_HEADS

    def spec(t, w):
        return pl.BlockSpec((1, t, w), lambda i, h: (i, 0, h))

    in_specs = [spec(n, dk), spec(n, dk), spec(n, dv), spec(n, dk), spec(n, dk), spec(n, dv),
                spec(l, dk), spec(l, dk), spec(l, dv), spec(l, dk), spec(l, dk), spec(l, dv),
                pl.BlockSpec((1, dv), lambda i, h: (0, 0))]
    out_specs = [spec(n, dv)]
    out_shape = [jax.ShapeDtypeStruct((b, n, dvt), BF16)]
    if ctx_out:
        out_specs.append(spec(l, dv))
        out_shape.append(jax.ShapeDtypeStruct((b, l, dvt), BF16))
    res = pl.pallas_call(
        functools.partial(_gla_kernel, n=n, l=l, dk=dk, ctx_out=ctx_out),
        grid=(b, GLA_HEADS),
        in_specs=in_specs,
        out_specs=out_specs,
        out_shape=out_shape,
        scratch_shapes=[pltpu.VMEM((n, dv), F32), pltpu.VMEM((l, dv), F32), pltpu.VMEM((dv, dk), F32)],
        compiler_params=_cparams(("parallel", "parallel")),
        name="gla",
    )(q, k, v, gf, gb, og, cq, ck, cv, cgf, cgb, cog, onorm.reshape(1, dv))
    return (res[0], res[1]) if ctx_out else (res[0], None)


def _rope_tables(n):
    half = HEAD_DIM // 2
    t = jnp.arange(n, dtype=I32)
    inv = ROPE_THETA ** (-jnp.arange(0, half, 2, dtype=F32) / half)

    def tab(pos):
        ang = pos.astype(F32)[:, None] * inv[None, :]
        ang = jnp.concatenate([ang, ang], axis=-1)
        return jnp.cos(ang), jnp.sin(ang)

    cos_r, sin_r = tab(t // GRID_W)
    cos_c, sin_c = tab(t % GRID_W)
    cos = jnp.concatenate([cos_r, cos_c], axis=-1)
    sin = jnp.concatenate([sin_r, sin_c], axis=-1)
    sign = jnp.where((jnp.arange(HEAD_DIM) % half) < half // 2, -1.0, 1.0).astype(F32)
    reps = LANES // HEAD_DIM
    return jnp.tile(cos, (1, reps)), jnp.tile(sin * sign[None, :], (1, reps))


def _block_ones():
    i = np.arange(LANES)
    return jnp.asarray((i[:, None] // HEAD_DIM) == (i[None, :] // HEAD_DIM), BF16)


def _attn_gains(qa_g, ka_g, qb_g, kb_g):
    qs = HEAD_DIM ** -0.5
    one_a = jnp.ones((A_KV_HEADS * HEAD_DIM,), F32)
    one_b = jnp.ones((B_HEADS * HEAD_DIM,), F32)
    return jnp.concatenate([jnp.tile(qa_g, A_HEADS) * qs, jnp.tile(ka_g, A_KV_HEADS), one_a,
                            jnp.tile(qb_g, B_HEADS) * qs, jnp.tile(kb_g, B_HEADS), one_b]).reshape(1, -1)


def kernel(x, c, ctx, c_ctx, ada_w, ada_b, norm1_g, norm2_g, attn_w_in, attn_w_out, a_q_norm, a_k_norm, b_q_norm,
           b_k_norm, na_rpb, gla_w_in, gla_gk_w_f, gla_gk_b_f, gla_gk_w_b, gla_gk_b_b, gla_o_norm, gla_w_out,
           moe_router, moe_w_gate, moe_w_up, moe_w_down):
    b, n, d = x.shape
    depth = ada_w.shape[0]
    dk = gla_gk_w_f.shape[2]
    dv = gla_w_out.shape[1]
    rows = -(-(b + 1) // 8) * 8
    cvec = jnp.zeros((rows, d), F32).at[:b].set(c).at[b].set(c_ctx)
    mods = _mods(cvec, ada_w, ada_b)
    cos, sin = _rope_tables(n)
    bd = _block_ones()
    h, hc = x, ctx
    for layer in range(depth):
        last = layer == depth - 1
        i = layer // 2
        ml = mods[layer, :b].reshape(b, 6, d)
        mc = mods[layer, b:b + 1].reshape(1, 6, d)
        if layer % 2 == 0:
            w_in = attn_w_in[i].astype(BF16)
            gains = _attn_gains(a_q_norm[i], a_k_norm[i], b_q_norm[i], b_k_norm[i])
            qa, ka, va, qb, kb, vb = _attn_proj(h, ml, norm1_g[layer], w_in, gains, bd, cos, sin)
            cqa, cka, cva, cqb, ckb, cvb = _attn_proj(hc, mc, norm1_g[layer], w_in, gains, bd)
            oa = _gqa(qa, jnp.concatenate([ka, cka], axis=1), jnp.concatenate([va, cva], axis=1), A_HEADS, A_KV_HEADS)
            ob = _na(qb, kb, vb, ckb, cvb, _na_bias_table(na_rpb[i], n // GRID_W))
            o_lat = [oa, ob]
            o_ctx = None if last else [_gqa(cqa, cka, cva, A_HEADS, A_KV_HEADS), _gqa(cqb, ckb, cvb, B_HEADS, B_HEADS)]
            w_out = attn_w_out[i].astype(BF16)
        else:
            w_main = gla_w_in[i][:, :2 * dk + 2 * dv].astype(BF16)
            wz = jnp.pad(gla_w_in[i][:, 2 * dk + 2 * dv:], ((0, 0), (0, LANES - 2 * GLA_GATE_RANK))).astype(BF16)
            wgate = jnp.zeros((LANES, 2 * dk), F32)
            wgate = wgate.at[:GLA_GATE_RANK, :dk].set(gla_gk_w_f[i]).at[GLA_GATE_RANK:2 * GLA_GATE_RANK, dk:].set(gla_gk_w_b[i])
            bgate = jnp.concatenate([gla_gk_b_f[i], gla_gk_b_b[i]]).reshape(1, 2 * dk)
            lat = _gla_proj(h, ml, norm1_g[layer], w_main, wz, wgate.astype(BF16), bgate, dk, dv)
            cx = _gla_proj(hc, mc, norm1_g[layer], w_main, wz, wgate.astype(BF16), bgate, dk, dv)
            o, co = _gla(lat, cx, gla_o_norm[i], not last)
            o_lat = [o]
            o_ctx = None if last else [co]
            w_out = gla_w_out[i].astype(BF16)
        wr_hi, wr_lo = _split2(moe_router[layer].T)
        h1, u2, lg = _outproj(o_lat, h, ml, norm2_g[layer], w_out, wr_hi, wr_lo)
        h = _moe(h1, u2, lg, ml, moe_w_gate, moe_w_up, moe_w_down, layer)
        if not last:
            hc1, uc2, lgc = _outproj(o_ctx, hc, mc, norm2_g[layer], w_out, wr_hi, wr_lo)
            hc = _moe(hc1, uc2, lgc, mc, moe_w_gate, moe_w_up, moe_w_down, layer)
    return h
```

```python
import functools

import numpy as np
import jax
import jax.numpy as jnp
from jax import lax
from jax.experimental import pallas as pl
from jax.experimental.pallas import tpu as pltpu

F32 = jnp.float32
BF16 = jnp.bfloat16
I32 = jnp.int32

GRID_W = 64
HEAD_DIM = 64
A_HEADS = 8
A_KV_HEADS = 2
B_HEADS = 8
NA_ROWS = 8
NA_COLS = 16
ROPE_THETA = 10000.0
GLA_HEADS = 4
GLA_GATE_RANK = 16
GLA_GATE_NORM = 16.0
GLA_CHUNK = 64
GLA_GROUP = 4
GLA_GROUPS_PER_TRIP = 16
N_EXPERTS = 16
EC_CAPACITY = 2
EPS = 1e-6

LANES = 128
BF16_ROWS = 16
NA_QROWS = 4
TOK_BLK = LANES
SMALL_WIN = 48
GATHER_TRIP = 32
MASK_VALUE = -1e30
VMEM_LIMIT = 56 * 1024 * 1024

_NT = (((1,), (1,)), ((), ()))
_TN = (((0,), (0,)), ((), ()))


def _cparams(sem):
    return pltpu.CompilerParams(dimension_semantics=sem, vmem_limit_bytes=VMEM_LIMIT)


def _dot(a, b):
    return jnp.dot(a, b, preferred_element_type=F32)


def _dot_nt(a, b):
    return lax.dot_general(a, b, _NT, preferred_element_type=F32)


def _dot_tn(a, b):
    return lax.dot_general(a, b, _TN, preferred_element_type=F32)


def _split2(a):
    hi = a.astype(BF16)
    lo = (a - hi.astype(F32)).astype(BF16)
    return hi, lo


def _silu(a):
    return a / (1.0 + jnp.exp(-a))


def _rms_mod(x, g, shift, scale):
    ms = jnp.mean(x * x, axis=-1, keepdims=True)
    return (x * lax.rsqrt(ms + EPS) * g) * (1.0 + scale) + shift


def _mods_kernel(c_ref, w_ref, b_ref, o_ref):
    c = c_ref[...]
    s_hi, s_lo = _split2(_silu(c))
    w_hi, w_lo = _split2(w_ref[0])
    o_ref[0] = _dot(s_hi, w_hi) + _dot(s_lo, w_hi) + _dot(s_hi, w_lo) + b_ref[0]


def _mods(cvec, ada_w, ada_b):
    depth, d, d6 = ada_w.shape
    rows = cvec.shape[0]
    tn = 1536 if d6 % 1536 == 0 else d6
    return pl.pallas_call(
        _mods_kernel,
        grid=(depth, d6 // tn),
        in_specs=[pl.BlockSpec((rows, d), lambda l, j: (0, 0)),
                  pl.BlockSpec((1, d, tn), lambda l, j: (l, 0, j)),
                  pl.BlockSpec((1, 1, tn), lambda l, j: (l, 0, j))],
        out_specs=pl.BlockSpec((1, rows, tn), lambda l, j: (l, 0, j)),
        out_shape=jax.ShapeDtypeStruct((depth, rows, d6), F32),
        compiler_params=_cparams(("arbitrary", "arbitrary")),
        name="mods",
    )(cvec, ada_w, ada_b.reshape(depth, 1, d6))


_ATTN_COLS = (A_HEADS + 2 * A_KV_HEADS + 3 * B_HEADS) * HEAD_DIM
_QA = (0, 4)
_KA = (4, 5)
_VA = (5, 6)
_QB = (6, 10)
_KB = (10, 14)
_VB = (14, 18)


def _attn_proj_kernel(*refs, rope):
    if rope:
        x_ref, mod_ref, g1_ref, w_ref, gain_ref, bd_ref, cos_ref, sin_ref = refs[:8]
        outs = refs[8:]
    else:
        x_ref, mod_ref, g1_ref, w_ref, gain_ref, bd_ref = refs[:6]
        outs = refs[6:]
    qa_ref, ka_ref, va_ref, qb_ref, kb_ref, vb_ref = outs
    m = mod_ref[0]
    u = _rms_mod(x_ref[0], g1_ref[...], m[0:1], m[1:2]).astype(BF16)
    acc = _dot(u, w_ref[...])
    bd = bd_ref[...]
    if rope:
        cos = cos_ref[...]
        sin = sin_ref[...]
        lane = lax.broadcasted_iota(I32, cos.shape, 1)
        first = (lane & 31) < 16

    def chunk(j, norm, rot):
        cch = acc[:, j * LANES:(j + 1) * LANES]
        if norm:
            sq_hi, sq_lo = _split2(cch * cch)
            ss = _dot(sq_hi, bd) + _dot(sq_lo, bd)
            cch = cch * lax.rsqrt(ss * (1.0 / HEAD_DIM) + EPS) * gain_ref[:, j * LANES:(j + 1) * LANES]
        if rot:
            partner = jnp.where(first, pltpu.roll(cch, LANES - 16, 1), pltpu.roll(cch, 16, 1))
            cch = cch * cos + partner * sin
        return cch.astype(BF16)

    for (lo, hi), ref, norm, rot in ((_QA, qa_ref, True, rope), (_KA, ka_ref, True, rope), (_VA, va_ref, False, False),
                                     (_QB, qb_ref, True, False), (_KB, kb_ref, True, False), (_VB, vb_ref, False, False)):
        for j in range(lo, hi):
            ref[0, :, (j - lo) * LANES:(j - lo + 1) * LANES] = chunk(j, norm, rot)


def _attn_proj(h, mod, g1, w, gains, bd, cos=None, sin=None):
    b, t, d = h.shape
    tm = min(512, t)
    rope = cos is not None
    nb_mod = mod.shape[0]
    midx = (lambda i, j: (i, 0, 0)) if nb_mod > 1 else (lambda i, j: (0, 0, 0))
    in_specs = [pl.BlockSpec((1, tm, d), lambda i, j: (i, j, 0)),
                pl.BlockSpec((1, 6, d), midx),
                pl.BlockSpec((1, d), lambda i, j: (0, 0)),
                pl.BlockSpec((d, _ATTN_COLS), lambda i, j: (0, 0)),
                pl.BlockSpec((1, _ATTN_COLS), lambda i, j: (0, 0)),
                pl.BlockSpec((LANES, LANES), lambda i, j: (0, 0))]
    args = [h, mod, g1.reshape(1, d), w, gains, bd]
    if rope:
        in_specs += [pl.BlockSpec((tm, LANES), lambda i, j: (j, 0))] * 2
        args += [cos, sin]
    widths = [(hi - lo) * LANES for lo, hi in (_QA, _KA, _VA, _QB, _KB, _VB)]
    return pl.pallas_call(
        functools.partial(_attn_proj_kernel, rope=rope),
        grid=(b, t // tm),
        in_specs=in_specs,
        out_specs=[pl.BlockSpec((1, tm, wd), lambda i, j: (i, j, 0)) for wd in widths],
        out_shape=[jax.ShapeDtypeStruct((b, t, wd), BF16) for wd in widths],
        compiler_params=_cparams(("parallel", "parallel")),
        name="attn_proj_rope" if rope else "attn_proj",
    )(*args)


def _gqa_kernel(q_ref, k_ref, v_ref, o_ref, *, hq, hk):
    grp = hq // hk
    for h in range(hq):
        kv = h // grp
        q = q_ref[0, :, h * HEAD_DIM:(h + 1) * HEAD_DIM]
        k = k_ref[0, :, kv * HEAD_DIM:(kv + 1) * HEAD_DIM]
        v = v_ref[0, :, kv * HEAD_DIM:(kv + 1) * HEAD_DIM]
        s = _dot_nt(q, k)
        p = jnp.exp(s - jnp.max(s, axis=-1, keepdims=True))
        l = jnp.sum(p, axis=-1, keepdims=True)
        o = _dot(p.astype(BF16), v) / l
        o_ref[0, :, h * HEAD_DIM:(h + 1) * HEAD_DIM] = o.astype(BF16)


def _gqa(q, k, v, hq, hk):
    b, t, _ = q.shape
    s = k.shape[1]
    tq = min(512, t)
    return pl.pallas_call(
        functools.partial(_gqa_kernel, hq=hq, hk=hk),
        grid=(b, t // tq),
        in_specs=[pl.BlockSpec((1, tq, hq * HEAD_DIM), lambda i, j: (i, j, 0)),
                  pl.BlockSpec((1, s, hk * HEAD_DIM), lambda i, j: (i, 0, 0)),
                  pl.BlockSpec((1, s, hk * HEAD_DIM), lambda i, j: (i, 0, 0))],
        out_specs=pl.BlockSpec((1, tq, hq * HEAD_DIM), lambda i, j: (i, j, 0)),
        out_shape=jax.ShapeDtypeStruct((b, t, hq * HEAD_DIM), BF16),
        compiler_params=_cparams(("parallel", "parallel")),
        name="gqa",
    )(q, k, v)


def _na_geometry(rows):
    kr = min(NA_QROWS + NA_ROWS - 1, rows)
    nblk = rows // NA_QROWS
    return kr, nblk


def _na_bias_table(rpb, rows):
    kr, nblk = _na_geometry(rows)
    wr = min(NA_ROWS, rows)
    heads = rpb.shape[0]
    qc = np.arange(GRID_W)[:, None]
    kc = np.arange(GRID_W)[None, :]
    cs = np.clip(qc - NA_COLS // 2, 0, GRID_W - NA_COLS)
    col_ok = (kc >= cs) & (kc < cs + NA_COLS)
    pick = (np.arange(2 * NA_COLS - 1)[:, None, None] == (kc - qc + NA_COLS - 1)[None]) & col_ok[None]
    toeplitz = jnp.einsum("hrd,dqk->hrqk", rpb.astype(F32), jnp.asarray(pick, F32), precision=lax.Precision.HIGHEST)
    toeplitz = jnp.where(col_ok[None, None], toeplitz, MASK_VALUE)
    tabs = []
    for rb in (0, min(1, nblk - 1), nblk - 1):
        qr0 = rb * NA_QROWS
        kr0 = int(np.clip(qr0 - NA_ROWS // 2, 0, rows - kr))
        qr = qr0 + np.arange(NA_QROWS)[:, None]
        krr = kr0 + np.arange(kr)[None, :]
        rs = np.clip(qr - wr // 2, 0, rows - wr)
        row_ok = (krr >= rs) & (krr < rs + wr)
        dr = krr - qr + (NA_ROWS - 1)
        masked = jnp.full((heads, GRID_W, GRID_W), MASK_VALUE, F32)
        blocks = [jnp.concatenate([toeplitz[:, dr[i, j]] if row_ok[i, j] else masked for j in range(kr)], axis=2)
                  for i in range(NA_QROWS)]
        tabs.append(jnp.concatenate(blocks, axis=1))
    return jnp.stack(tabs)


def _na_kernel(q_ref, k_ref, v_ref, kc_ref, vc_ref, bias_ref, o_ref, *, rows, heads):
    kr, _ = _na_geometry(rows)
    rb = pl.program_id(1)
    kr0 = jnp.clip(rb * NA_QROWS - NA_ROWS // 2, 0, rows - kr)
    start = pl.multiple_of(kr0 * GRID_W, GRID_W)
    kw = k_ref[0, pl.ds(start, kr * GRID_W), :]
    vw = v_ref[0, pl.ds(start, kr * GRID_W), :]
    for h in range(heads):
        sl = slice(h * HEAD_DIM, (h + 1) * HEAD_DIM)
        q = q_ref[0, :, sl]
        s_w = _dot_nt(q, kw[:, sl]) + bias_ref[0, h]
        s_c = _dot_nt(q, kc_ref[0, :, sl])
        m = jnp.maximum(jnp.max(s_w, axis=-1, keepdims=True), jnp.max(s_c, axis=-1, keepdims=True))
        p_w = jnp.exp(s_w - m)
        p_c = jnp.exp(s_c - m)
        l = jnp.sum(p_w, axis=-1, keepdims=True) + jnp.sum(p_c, axis=-1, keepdims=True)
        o = (_dot(p_w.astype(BF16), vw[:, sl]) + _dot(p_c.astype(BF16), vc_ref[0, :, sl])) / l
        o_ref[0, :, sl] = o.astype(BF16)


def _na(q, k, v, kc, vc, bias):
    b, n, c = q.shape
    l = kc.shape[1]
    rows = n // GRID_W
    kr, nblk = _na_geometry(rows)
    tq = NA_QROWS * GRID_W
    heads = c // HEAD_DIM

    def pattern(i, j):
        return (jnp.where(j == 0, 0, jnp.where(j == nblk - 1, 2, 1)), 0, 0, 0)

    return pl.pallas_call(
        functools.partial(_na_kernel, rows=rows, heads=heads),
        grid=(b, nblk),
        in_specs=[pl.BlockSpec((1, tq, c), lambda i, j: (i, j, 0)),
                  pl.BlockSpec((1, n, c), lambda i, j: (i, 0, 0)),
                  pl.BlockSpec((1, n, c), lambda i, j: (i, 0, 0)),
                  pl.BlockSpec((1, l, c), lambda i, j: (i, 0, 0)),
                  pl.BlockSpec((1, l, c), lambda i, j: (i, 0, 0)),
                  pl.BlockSpec((1, heads, tq, kr * GRID_W), pattern)],
        out_specs=pl.BlockSpec((1, tq, c), lambda i, j: (i, j, 0)),
        out_shape=jax.ShapeDtypeStruct((b, n, c), BF16),
        compiler_params=_cparams(("parallel", "arbitrary")),
        name="na_attn",
    )(q, k, v, kc, vc, bias)


def _outproj_kernel(*refs, n_o, route):
    o_refs = refs[:n_o]
    h_ref, mod_ref, g2_ref, w_ref = refs[n_o:n_o + 4]
    rest = refs[n_o + 4:]
    if route:
        wr_hi_ref, wr_lo_ref, hn_ref, u2_ref, lg_ref = rest
    else:
        hn_ref, = rest
    m = mod_ref[0]
    o = o_refs[0][0] if n_o == 1 else jnp.concatenate([r[0] for r in o_refs], axis=1)
    hn = h_ref[0] + m[2:3] * _dot(o, w_ref[...])
    hn_ref[0] = hn
    if route:
        u2 = _rms_mod(hn, g2_ref[...], m[3:4], m[4:5])
        u_hi, u_lo = _split2(u2)
        u2_ref[0] = u_hi
        wr_hi = wr_hi_ref[...]
        both = _dot_nt(jnp.concatenate([wr_hi, wr_lo_ref[...]], axis=0), u_hi)
        lg_ref[0] = both[:N_EXPERTS] + both[N_EXPERTS:] + _dot_nt(wr_hi, u_lo)


def _outproj(o_list, h, mod, g2, w, wr_hi=None, wr_lo=None):
    b, t, d = h.shape
    tm = min(512, t)
    route = wr_hi is not None
    nb_mod = mod.shape[0]
    midx = (lambda i, j: (i, 0, 0)) if nb_mod > 1 else (lambda i, j: (0, 0, 0))
    in_specs = [pl.BlockSpec((1, tm, o.shape[2]), lambda i, j: (i, j, 0)) for o in o_list]
    in_specs += [pl.BlockSpec((1, tm, d), lambda i, j: (i, j, 0)),
                 pl.BlockSpec((1, 6, d), midx),
                 pl.BlockSpec((1, d), lambda i, j: (0, 0)),
                 pl.BlockSpec((d, d), lambda i, j: (0, 0))]
    args = list(o_list) + [h, mod, g2.reshape(1, d), w]
    out_specs = [pl.BlockSpec((1, tm, d), lambda i, j: (i, j, 0))]
    out_shape = [jax.ShapeDtypeStruct((b, t, d), F32)]
    if route:
        in_specs += [pl.BlockSpec((N_EXPERTS, d), lambda i, j: (0, 0))] * 2
        args += [wr_hi, wr_lo]
        out_specs += [pl.BlockSpec((1, tm, d), lambda i, j: (i, j, 0)),
                      pl.BlockSpec((1, N_EXPERTS, tm), lambda i, j: (i, 0, j))]
        out_shape += [jax.ShapeDtypeStruct((b, t, d), BF16), jax.ShapeDtypeStruct((b, N_EXPERTS, t), F32)]
    return pl.pallas_call(
        functools.partial(_outproj_kernel, n_o=len(o_list), route=route),
        grid=(b, t // tm),
        in_specs=in_specs,
        out_specs=out_specs,
        out_shape=out_shape,
        compiler_params=_cparams(("parallel", "parallel")),
        name="outproj",
    )(*args)


def _cumsum_excl(x):
    e, n = x.shape
    nb = n // LANES
    st = jnp.concatenate([x[:, j * LANES:(j + 1) * LANES] for j in range(nb)], axis=0)
    ii = lax.broadcasted_iota(I32, (LANES, LANES), 0)
    jj = lax.broadcasted_iota(I32, (LANES, LANES), 1)
    incl = _dot(st.astype(BF16), jnp.where(ii <= jj, 1.0, 0.0).astype(BF16))
    tot = jnp.broadcast_to(incl[:, LANES - 1:LANES], incl.shape)
    r = lax.broadcasted_iota(I32, (nb * e, nb * e), 0)
    c = lax.broadcasted_iota(I32, (nb * e, nb * e), 1)
    sh = e.bit_length() - 1
    same = (r & (e - 1)) == (c & (e - 1))
    prev = jnp.where(same & (lax.shift_right_logical(c, sh) < lax.shift_right_logical(r, sh)), 1.0, 0.0).astype(BF16)
    starts = _dot(prev, tot.astype(BF16))
    excl_st = incl - st + starts
    excl = jnp.concatenate([excl_st[j * e:(j + 1) * e, :] for j in range(nb)], axis=1)
    return excl, starts


def _route_kernel(lg_ref, dest_ref, gate_ref, offs_ref, *, cap):
    x = lg_ref[0]
    ex = jnp.exp(x - jnp.max(x, axis=0, keepdims=True))
    aff = ex / jnp.sum(ex, axis=0, keepdims=True)

    def enough(cand):
        return jnp.sum(jnp.where(aff >= lax.bitcast_convert_type(cand, F32), 1.0, 0.0), axis=1, keepdims=True) >= cap

    def search(i, cur):
        sh = 28 - 2 * i
        c1, c2, c3 = (cur | lax.shift_left(jnp.int32(k), sh) for k in (1, 2, 3))
        return jnp.where(enough(c3), c3, jnp.where(enough(c2), c2, jnp.where(enough(c1), c1, cur)))

    top = jnp.full((x.shape[0], 1), 1 << 30, I32)
    thr = jnp.where(enough(top), top, jnp.zeros_like(top))
    thr = lax.bitcast_convert_type(lax.fori_loop(0, 15, search, thr), F32)
    gt = aff > thr
    eq = aff == thr
    need = cap - jnp.sum(jnp.where(gt, 1.0, 0.0), axis=1, keepdims=True)
    rank_eq, _ = _cumsum_excl(jnp.where(eq, 1.0, 0.0))
    sel = gt | (eq & (rank_eq < need))
    pos, starts = _cumsum_excl(jnp.where(sel, 1.0, 0.0))
    dest_ref[0] = jnp.where(sel, pos.astype(I32), -1)
    gate_ref[0] = jnp.where(sel, aff, 0.0)
    offs_ref[0] = starts.astype(I32)


def _route(lg, cap):
    b, e, n = lg.shape
    nb = n // LANES
    return pl.pallas_call(
        functools.partial(_route_kernel, cap=cap),
        grid=(b,),
        in_specs=[pl.BlockSpec((1, e, n), lambda i: (i, 0, 0))],
        out_specs=[pl.BlockSpec((1, e, n), lambda i: (i, 0, 0)),
                   pl.BlockSpec((1, e, n), lambda i: (i, 0, 0)),
                   pl.BlockSpec((1, nb * e, LANES), lambda i: (i, 0, 0))],
        out_shape=[jax.ShapeDtypeStruct((b, e, n), I32), jax.ShapeDtypeStruct((b, e, n), F32),
                   jax.ShapeDtypeStruct((b, nb * e, LANES), I32)],
        compiler_params=_cparams(("parallel",)),
        name="route",
    )(lg)


def _blk_range(offs_ref, b, j, e, nblk, cap):
    base = (b * nblk + j) * N_EXPERTS + e
    last = (b * nblk + nblk - 1) * N_EXPERTS + e
    return offs_ref[base], jnp.where(j + 1 < nblk, offs_ref[jnp.minimum(base + N_EXPERTS, last)], cap)


def _win_start(off, cap, win):
    aligned = lax.shift_right_logical(off, BF16_ROWS.bit_length() - 1) * BF16_ROWS
    return pl.multiple_of(jnp.minimum(aligned, cap - win), BF16_ROWS)


def _onehot(dest_row, off_al, win):
    rows = lax.broadcasted_iota(I32, (win, TOK_BLK), 0) + off_al
    return jnp.where(rows == dest_row, 1.0, 0.0)


def _moe_ffn_kernel(offs_ref, u_ref, dest_ref, gate_ref, wg_ref, wu_ref, wd_ref, y_ref,
                    xg_ref, gb_ref, wgb_ref, wub_ref, wdb_ref, *, cap, nblk, win, small, nbatch, batch_ffn):
    e = pl.program_id(0)

    def load_weights():
        wgb_ref[...] = wg_ref[0, 0].astype(BF16)
        wub_ref[...] = wu_ref[0, 0].astype(BF16)
        wdb_ref[...] = wd_ref[0, 0].astype(BF16)

    trip = min(GATHER_TRIP, nblk)

    def gather_sample(b, blk, row0):
        xg_ref[pl.ds(row0, cap), :] = jnp.zeros((cap, xg_ref.shape[1]), F32)
        gb_ref[pl.ds(row0, cap), :] = jnp.zeros((cap, LANES), F32)

        def gather(t, carry):
            js = [t * trip + i for i in range(trip)]
            ranges = [_blk_range(offs_ref, b, j, e, nblk, cap) for j in js]

            def run(w):
                for j, (off, _) in zip(js, ranges):
                    st = _win_start(off, cap, w)
                    g = _onehot(dest_ref[blk, 0, pl.ds(j, 1), :], st, w)
                    tok0 = j * TOK_BLK if isinstance(j, int) else pl.multiple_of(j * TOK_BLK, TOK_BLK)
                    uj = u_ref[blk, pl.ds(tok0, TOK_BLK), :]
                    r = pl.multiple_of(row0 + st, BF16_ROWS)
                    xg_ref[pl.ds(r, w), :] += _dot(g.astype(BF16), uj)
                    gsum = jnp.sum(g * gate_ref[blk, 0, pl.ds(j, 1), :], axis=1, keepdims=True)
                    gb_ref[pl.ds(r, w), :] += jnp.broadcast_to(gsum, (w, LANES))

            if small == win:
                run(win)
            else:
                fits = ranges[0][1] <= _win_start(ranges[0][0], cap, small) + small
                for off, end in ranges[1:]:
                    fits = fits & (end <= _win_start(off, cap, small) + small)
                pl.when(fits)(functools.partial(run, small))
                pl.when(jnp.logical_not(fits))(functools.partial(run, win))
            return carry

        if nblk == trip:
            gather(0, 0)
        else:
            lax.fori_loop(0, nblk // trip, gather, 0)

    def ffn(m):
        x = xg_ref[0:m, :].astype(BF16)
        ff = wgb_ref.shape[1]
        fc = min(512, ff)
        acc = jnp.zeros((m, wdb_ref.shape[1]), F32)
        for c in range(ff // fc):
            a = _dot(x, wgb_ref[:, c * fc:(c + 1) * fc])
            up = _dot(x, wub_ref[:, c * fc:(c + 1) * fc])
            acc = acc + _dot((_silu(a) * up).astype(BF16), wdb_ref[c * fc:(c + 1) * fc, :])
        return (acc * gb_ref[0:m, 0:1]).astype(BF16)

    if batch_ffn:
        load_weights()
        for bb in range(nbatch):
            gather_sample(bb, bb, bb * cap)
        y = ffn(nbatch * cap)
        for bb in range(nbatch):
            y_ref[bb, 0] = y[bb * cap:(bb + 1) * cap]
    else:
        b = pl.program_id(1)
        pl.when(b == 0)(load_weights)
        gather_sample(b, 0, 0)
        y_ref[0, 0] = ffn(cap)


def _moe_ffn(offs, u, dest4, gate4, wg, wu, wd, layer, cap):
    b, n, d = u.shape
    nblk = n // TOK_BLK
    win = min(TOK_BLK + BF16_ROWS, cap)
    ff = wg.shape[3]
    batch_ffn = cap < 256
    if batch_ffn:
        grid = (N_EXPERTS,)
        rows = b * cap
        tok_spec = pl.BlockSpec((b, n, d), lambda e, o: (0, 0, 0))
        sel_spec = pl.BlockSpec((b, 1, nblk, TOK_BLK), lambda e, o: (0, e, 0, 0))
        y_spec = pl.BlockSpec((b, 1, cap, d), lambda e, o: (0, e, 0, 0))

        def w_spec(r, c):
            return pl.BlockSpec((1, 1, r, c), lambda e, o: (layer, e, 0, 0))
    else:
        grid = (N_EXPERTS, b)
        rows = cap
        tok_spec = pl.BlockSpec((1, n, d), lambda e, i, o: (i, 0, 0))
        sel_spec = pl.BlockSpec((1, 1, nblk, TOK_BLK), lambda e, i, o: (i, e, 0, 0))
        y_spec = pl.BlockSpec((1, 1, cap, d), lambda e, i, o: (i, e, 0, 0))

        def w_spec(r, c):
            return pl.BlockSpec((1, 1, r, c), lambda e, i, o: (layer, e, 0, 0))
    return pl.pallas_call(
        functools.partial(_moe_ffn_kernel, cap=cap, nblk=nblk, win=win, small=min(SMALL_WIN, cap), nbatch=b,
                          batch_ffn=batch_ffn),
        grid_spec=pltpu.PrefetchScalarGridSpec(
            num_scalar_prefetch=1,
            grid=grid,
            in_specs=[tok_spec, sel_spec, sel_spec, w_spec(d, ff), w_spec(d, ff), w_spec(ff, d)],
            out_specs=y_spec,
            scratch_shapes=[pltpu.VMEM((rows, d), F32), pltpu.VMEM((rows, LANES), F32),
                            pltpu.VMEM((d, ff), BF16), pltpu.VMEM((d, ff), BF16), pltpu.VMEM((ff, d), BF16)]),
        out_shape=jax.ShapeDtypeStruct((b, N_EXPERTS, cap, d), BF16),
        compiler_params=_cparams(("arbitrary",) * len(grid)),
        name="moe_ffn",
    )(offs, u, dest4, gate4, wg, wu, wd)


def _moe_combine_kernel(offs_ref, y_ref, dest_ref, h_ref, mod_ref, out_ref, *, cap, nblk, win, small, nsub):
    b = pl.program_id(0)
    jb = pl.program_id(1)
    gate = mod_ref[0][5:6]
    ranges = [[_blk_range(offs_ref, b, jb * nsub + s, e, nblk, cap) for e in range(N_EXPERTS)] for s in range(nsub)]

    def stacked():
        for s in range(nsub):
            rows = slice(s * TOK_BLK, (s + 1) * TOK_BLK)
            starts = [_win_start(off, cap, small) for off, _ in ranges[s]]
            ycat = jnp.concatenate([y_ref[0, e, pl.ds(starts[e], small), :] for e in range(N_EXPERTS)], axis=0)
            gcat = jnp.concatenate([_onehot(dest_ref[0, e, s:s + 1, :], starts[e], small) for e in range(N_EXPERTS)], axis=0)
            out_ref[0, rows, :] = h_ref[0, rows, :] + gate * _dot_tn(gcat.astype(BF16), ycat)

    def per_expert():
        for s in range(nsub):
            rows = slice(s * TOK_BLK, (s + 1) * TOK_BLK)
            acc = jnp.zeros((TOK_BLK, out_ref.shape[2]), F32)
            for e in range(N_EXPERTS):
                st = _win_start(ranges[s][e][0], cap, win)
                g = _onehot(dest_ref[0, e, s:s + 1, :], st, win).astype(BF16)
                acc = acc + _dot_tn(g, y_ref[0, e, pl.ds(st, win), :])
            out_ref[0, rows, :] = h_ref[0, rows, :] + gate * acc

    if small == win:
        stacked()
    else:
        fits = None
        for per_block in ranges:
            for off, end in per_block:
                ok = end <= _win_start(off, cap, small) + small
                fits = ok if fits is None else fits & ok
        pl.when(fits)(stacked)
        pl.when(jnp.logical_not(fits))(per_expert)


def _moe_combine(offs, y, dest4, h, mod, cap):
    b, n, d = h.shape
    nblk = n // TOK_BLK
    win = min(TOK_BLK + BF16_ROWS, cap)
    small = min(SMALL_WIN, cap)
    nsub = min(8, nblk)
    tb = nsub * TOK_BLK
    nb_mod = mod.shape[0]
    midx = (lambda i, j, o: (i, 0, 0)) if nb_mod > 1 else (lambda i, j, o: (0, 0, 0))
    return pl.pallas_call(
        functools.partial(_moe_combine_kernel, cap=cap, nblk=nblk, win=win, small=small, nsub=nsub),
        grid_spec=pltpu.PrefetchScalarGridSpec(
            num_scalar_prefetch=1,
            grid=(b, n // tb),
            in_specs=[pl.BlockSpec((1, N_EXPERTS, cap, d), lambda i, j, o: (i, 0, 0, 0), pipeline_mode=pl.Buffered(1)),
                      pl.BlockSpec((1, N_EXPERTS, nsub, TOK_BLK), lambda i, j, o: (i, 0, j, 0)),
                      pl.BlockSpec((1, tb, d), lambda i, j, o: (i, j, 0)),
                      pl.BlockSpec((1, 6, d), midx)],
            out_specs=pl.BlockSpec((1, tb, d), lambda i, j, o: (i, j, 0))),
        out_shape=jax.ShapeDtypeStruct((b, n, d), F32),
        compiler_params=_cparams(("arbitrary", "arbitrary")),
        name="moe_combine",
    )(offs, y, dest4, h, mod)


def _moe(h, u2, lg, mod, wg, wu, wd, layer):
    b, n, d = h.shape
    cap = EC_CAPACITY * n // N_EXPERTS
    nblk = n // TOK_BLK
    dest, gate, starts = _route(lg, cap)
    offs = starts[:, :, 0].reshape(-1)
    dest4 = dest.reshape(b, N_EXPERTS, nblk, TOK_BLK)
    gate4 = gate.reshape(b, N_EXPERTS, nblk, TOK_BLK)
    y = _moe_ffn(offs, u2, dest4, gate4, wg, wu, wd, layer, cap)
    return _moe_combine(offs, y, dest4, h, mod, cap)


def _gla_proj_kernel(x_ref, mod_ref, g1_ref, w_ref, wz_ref, wgate_ref, bgate_ref,
                     q_ref, k_ref, v_ref, og_ref, gf_ref, gb_ref, *, dk, dv):
    m = mod_ref[0]
    u = _rms_mod(x_ref[0], g1_ref[...], m[0:1], m[1:2]).astype(BF16)
    acc = _dot(u, w_ref[...])
    z = _dot(u, wz_ref[...]).astype(BF16)
    gp = _dot(z, wgate_ref[...]) + bgate_ref[...]
    g = (jnp.minimum(gp, 0.0) - jnp.log1p(jnp.exp(-jnp.abs(gp)))) * (1.0 / GLA_GATE_NORM)
    q_ref[0] = acc[:, :dk]
    k_ref[0] = acc[:, dk:2 * dk]
    v_ref[0] = acc[:, 2 * dk:2 * dk + dv].astype(BF16)
    og_ref[0] = acc[:, 2 * dk + dv:]
    gf_ref[0] = g[:, :dk]
    gb_ref[0] = g[:, dk:]


def _gla_proj(h, mod, g1, w, wz, wgate, bgate, dk, dv):
    b, t, d = h.shape
    tm = min(512, t)
    nb_mod = mod.shape[0]
    midx = (lambda i, j: (i, 0, 0)) if nb_mod > 1 else (lambda i, j: (0, 0, 0))
    cols = 2 * dk + 2 * dv
    widths = [(dk, F32), (dk, F32), (dv, BF16), (dv, F32), (dk, F32), (dk, F32)]
    return pl.pallas_call(
        functools.partial(_gla_proj_kernel, dk=dk, dv=dv),
        grid=(b, t // tm),
        in_specs=[pl.BlockSpec((1, tm, d), lambda i, j: (i, j, 0)),
                  pl.BlockSpec((1, 6, d), midx),
                  pl.BlockSpec((1, d), lambda i, j: (0, 0)),
                  pl.BlockSpec((d, cols), lambda i, j: (0, 0)),
                  pl.BlockSpec((d, LANES), lambda i, j: (0, 0)),
                  pl.BlockSpec((LANES, 2 * dk), lambda i, j: (0, 0)),
                  pl.BlockSpec((1, 2 * dk), lambda i, j: (0, 0))],
        out_specs=[pl.BlockSpec((1, tm, wd), lambda i, j: (i, j, 0)) for wd, _ in widths],
        out_shape=[jax.ShapeDtypeStruct((b, t, wd), dt) for wd, dt in widths],
        compiler_params=_cparams(("parallel", "parallel")),
        name="gla_proj",
    )(h, mod, g1.reshape(1, d), w, wz, wgate, bgate)


def _chunk_cumsum(x, c, reverse):
    rows = x.shape[0]
    pos = lax.broadcasted_iota(I32, x.shape, 0) & (c - 1)
    s = 1
    while s < c:
        if reverse:
            x = x + jnp.where(pos < c - s, pltpu.roll(x, rows - s, 0), 0.0)
        else:
            x = x + jnp.where(pos >= s, pltpu.roll(x, s, 0), 0.0)
        s *= 2
    return x


def _gla_scan(q_ref, k_ref, v_ref, g_ref, acc_ref, st_ref, *, n_rows, reverse, scale):
    c = GLA_CHUNK
    gr = min(GLA_GROUP * c, n_rows)
    cpg = gr // c
    gpi = min(GLA_GROUPS_PER_TRIP, n_rows // gr)
    rows_it = gpi * gr
    n_it = n_rows // rows_it
    ii = lax.broadcasted_iota(I32, (gr, gr), 0)
    jj = lax.broadcasted_iota(I32, (gr, gr), 1)
    sh = c.bit_length() - 1
    same = (ii >> sh) == (jj >> sh)
    tri = same & ((jj >= ii) if reverse else (jj <= ii))
    g_order = range(gpi - 1, -1, -1) if reverse else range(gpi)
    c_order = range(cpg - 1, -1, -1) if reverse else range(cpg)

    def body(i, carry):
        r0 = 0 if n_it == 1 else pl.multiple_of(((n_it - 1 - i) if reverse else i) * rows_it, rows_it)
        q_all = q_ref[0, pl.ds(r0, rows_it), :]
        k_all = k_ref[0, pl.ds(r0, rows_it), :]
        v_all = v_ref[0, pl.ds(r0, rows_it), :]
        g_all = g_ref[0, pl.ds(r0, rows_it), :]
        prev = acc_ref[pl.ds(r0, rows_it), :] if reverse else None
        st = st_ref[...]
        outs = [None] * gpi
        for gi in g_order:
            rs = slice(gi * gr, (gi + 1) * gr)
            k = k_all[rs]
            v = v_all[rs]
            cum = _chunk_cumsum(g_all[rs], c, reverse)
            cl = [cum[ci * c:ci * c + 1] if reverse else cum[(ci + 1) * c - 1:(ci + 1) * c] for ci in range(cpg)]
            cl_rows = jnp.concatenate([jnp.broadcast_to(x, (c, x.shape[1])) for x in cl], axis=0)
            q_dec = (q_all[rs] * scale * jnp.exp(cum)).astype(BF16)
            k_inv = (k * jnp.exp(-cum)).astype(BF16)
            k_st = (k * jnp.exp(cl_rows - cum)).astype(BF16)
            a = jnp.where(tri, _dot_nt(q_dec, k_inv), 0.0)
            o_intra = _dot(a.astype(BF16), v)
            parts = [None] * cpg
            for ci in c_order:
                cs = slice(ci * c, (ci + 1) * c)
                parts[ci] = o_intra[cs] + _dot_nt(q_dec[cs], st.astype(BF16))
                st = st * jnp.exp(cl[ci]) + _dot_tn(v[cs], k_st[cs])
            outs[gi] = jnp.concatenate(parts, axis=0)
        st_ref[...] = st
        o_all = outs[0] if gpi == 1 else jnp.concatenate(outs, axis=0)
        acc_ref[pl.ds(r0, rows_it), :] = (prev + o_all) if reverse else o_all
        return carry

    if n_it == 1:
        body(0, 0)
    else:
        lax.fori_loop(0, n_it, body, 0)


def _gla_merge(acc_ref, og_ref, gain, o_ref, t):
    tm = min(512, t)
    for r in range(t // tm):
        o = acc_ref[r * tm:(r + 1) * tm, :]
        ms = jnp.mean(o * o, axis=-1, keepdims=True)
        y = (o * lax.rsqrt(ms + EPS) * gain) * _silu(og_ref[0, r * tm:(r + 1) * tm, :])
        o_ref[0, r * tm:(r + 1) * tm, :] = y.astype(BF16)


def _gla_kernel(*refs, n, l, dk, ctx_out):
    q_ref, k_ref, v_ref, gf_ref, gb_ref, og_ref, cq_ref, ck_ref, cv_ref, cgf_ref, cgb_ref, cog_ref, on_ref = refs[:13]
    if ctx_out:
        o_ref, co_ref, acc_ref, cacc_ref, st_ref = refs[13:]
    else:
        o_ref, acc_ref, cacc_ref, st_ref = refs[13:]
    scale = dk ** -0.5
    for reverse, g_ref, cg_ref in ((False, gf_ref, cgf_ref), (True, gb_ref, cgb_ref)):
        st_ref[...] = jnp.zeros_like(st_ref)
        _gla_scan(cq_ref, ck_ref, cv_ref, cg_ref, cacc_ref, st_ref, n_rows=l, reverse=reverse, scale=scale)
        _gla_scan(q_ref, k_ref, v_ref, g_ref, acc_ref, st_ref, n_rows=n, reverse=reverse, scale=scale)
    gain = on_ref[...]
    _gla_merge(acc_ref, og_ref, gain, o_ref, n)
    if ctx_out:
        _gla_merge(cacc_ref, cog_ref, gain, co_ref, l)


def _gla(lat, ctx, onorm, ctx_out):
    q, k, v, og, gf, gb = lat
    cq, ck, cv, cog, cgf, cgb = ctx
    b, n, dkt = q.shape
    l = cq.shape[1]
    dvt = v.shape[2]
    dk = dkt // GLA_HEADS
    dv = dvt // GLA_HEADS

    def spec(t, w):
        return pl.BlockSpec((1, t, w), lambda i, h: (i, 0, h))

    in_specs = [spec(n, dk), spec(n, dk), spec(n, dv), spec(n, dk), spec(n, dk), spec(n, dv),
                spec(l, dk), spec(l, dk), spec(l, dv), spec(l, dk), spec(l, dk), spec(l, dv),
                pl.BlockSpec((1, dv), lambda i, h: (0, 0))]
    out_specs = [spec(n, dv)]
    out_shape = [jax.ShapeDtypeStruct((b, n, dvt), BF16)]
    if ctx_out:
        out_specs.append(spec(l, dv))
        out_shape.append(jax.ShapeDtypeStruct((b, l, dvt), BF16))
    res = pl.pallas_call(
        functools.partial(_gla_kernel, n=n, l=l, dk=dk, ctx_out=ctx_out),
        grid=(b, GLA_HEADS),
        in_specs=in_specs,
        out_specs=out_specs,
        out_shape=out_shape,
        scratch_shapes=[pltpu.VMEM((n, dv), F32), pltpu.VMEM((l, dv), F32), pltpu.VMEM((dv, dk), F32)],
        compiler_params=_cparams(("parallel", "parallel")),
        name="gla",
    )(q, k, v, gf, gb, og, cq, ck, cv, cgf, cgb, cog, onorm.reshape(1, dv))
    return (res[0], res[1]) if ctx_out else (res[0], None)


def _rope_tables(n):
    half = HEAD_DIM // 2
    t = jnp.arange(n, dtype=I32)
    inv = ROPE_THETA ** (-jnp.arange(0, half, 2, dtype=F32) / half)

    def tab(pos):
        ang = pos.astype(F32)[:, None] * inv[None, :]
        ang = jnp.concatenate([ang, ang], axis=-1)
        return jnp.cos(ang), jnp.sin(ang)

    cos_r, sin_r = tab(t // GRID_W)
    cos_c, sin_c = tab(t % GRID_W)
    cos = jnp.concatenate([cos_r, cos_c], axis=-1)
    sin = jnp.concatenate([sin_r, sin_c], axis=-1)
    sign = jnp.where((jnp.arange(HEAD_DIM) % half) < half // 2, -1.0, 1.0).astype(F32)
    reps = LANES // HEAD_DIM
    return jnp.tile(cos, (1, reps)), jnp.tile(sin * sign[None, :], (1, reps))


def _block_ones():
    i = np.arange(LANES)
    return jnp.asarray((i[:, None] // HEAD_DIM) == (i[None, :] // HEAD_DIM), BF16)


def _attn_gains(qa_g, ka_g, qb_g, kb_g):
    qs = HEAD_DIM ** -0.5
    one_a = jnp.ones((A_KV_HEADS * HEAD_DIM,), F32)
    one_b = jnp.ones((B_HEADS * HEAD_DIM,), F32)
    return jnp.concatenate([jnp.tile(qa_g, A_HEADS) * qs, jnp.tile(ka_g, A_KV_HEADS), one_a,
                            jnp.tile(qb_g, B_HEADS) * qs, jnp.tile(kb_g, B_HEADS), one_b]).reshape(1, -1)


def kernel(x, c, ctx, c_ctx, ada_w, ada_b, norm1_g, norm2_g, attn_w_in, attn_w_out, a_q_norm, a_k_norm, b_q_norm,
           b_k_norm, na_rpb, gla_w_in, gla_gk_w_f, gla_gk_b_f, gla_gk_w_b, gla_gk_b_b, gla_o_norm, gla_w_out,
           moe_router, moe_w_gate, moe_w_up, moe_w_down):
    b, n, d = x.shape
    depth = ada_w.shape[0]
    dk = gla_gk_w_f.shape[2]
    dv = gla_w_out.shape[1]
    rows = -(-(b + 1) // 8) * 8
    cvec = jnp.zeros((rows, d), F32).at[:b].set(c).at[b].set(c_ctx)
    mods = _mods(cvec, ada_w, ada_b)
    cos, sin = _rope_tables(n)
    bd = _block_ones()
    h, hc = x, ctx
    for layer in range(depth):
        last = layer == depth - 1
        i = layer // 2
        ml = mods[layer, :b].reshape(b, 6, d)
        mc = mods[layer, b:b + 1].reshape(1, 6, d)
        if layer % 2 == 0:
            w_in = attn_w_in[i].astype(BF16)
            gains = _attn_gains(a_q_norm[i], a_k_norm[i], b_q_norm[i], b_k_norm[i])
            qa, ka, va, qb, kb, vb = _attn_proj(h, ml, norm1_g[layer], w_in, gains, bd, cos, sin)
            cqa, cka, cva, cqb, ckb, cvb = _attn_proj(hc, mc, norm1_g[layer], w_in, gains, bd)
            oa = _gqa(qa, jnp.concatenate([ka, cka], axis=1), jnp.concatenate([va, cva], axis=1), A_HEADS, A_KV_HEADS)
            ob = _na(qb, kb, vb, ckb, cvb, _na_bias_table(na_rpb[i], n // GRID_W))
            o_lat = [oa, ob]
            o_ctx = None if last else [_gqa(cqa, cka, cva, A_HEADS, A_KV_HEADS), _gqa(cqb, ckb, cvb, B_HEADS, B_HEADS)]
            w_out = attn_w_out[i].astype(BF16)
        else:
            w_main = gla_w_in[i][:, :2 * dk + 2 * dv].astype(BF16)
            wz = jnp.pad(gla_w_in[i][:, 2 * dk + 2 * dv:], ((0, 0), (0, LANES - 2 * GLA_GATE_RANK))).astype(BF16)
            wgate = jnp.zeros((LANES, 2 * dk), F32)
            wgate = wgate.at[:GLA_GATE_RANK, :dk].set(gla_gk_w_f[i]).at[GLA_GATE_RANK:2 * GLA_GATE_RANK, dk:].set(gla_gk_w_b[i])
            bgate = jnp.concatenate([gla_gk_b_f[i], gla_gk_b_b[i]]).reshape(1, 2 * dk)
            lat = _gla_proj(h, ml, norm1_g[layer], w_main, wz, wgate.astype(BF16), bgate, dk, dv)
            cx = _gla_proj(hc, mc, norm1_g[layer], w_main, wz, wgate.astype(BF16), bgate, dk, dv)
            o, co = _gla(lat, cx, gla_o_norm[i], not last)
            o_lat = [o]
            o_ctx = None if last else [co]
            w_out = gla_w_out[i].astype(BF16)
        wr_hi, wr_lo = _split2(moe_router[layer].T)
        h1, u2, lg = _outproj(o_lat, h, ml, norm2_g[layer], w_out, wr_hi, wr_lo)
        h = _moe(h1, u2, lg, ml, moe_w_gate, moe_w_up, moe_w_down, layer)
        if not last:
            hc1, uc2, lgc = _outproj(o_ctx, hc, mc, norm2_g[layer], w_out, wr_hi, wr_lo)
            hc = _moe(hc1, uc2, lgc, mc, moe_w_gate, moe_w_up, moe_w_down, layer)
    return h
```
